```python
import math
import jax
import jax.numpy as jnp
from jax import lax
import numpy as np

D_MODEL = 1024
BATCH = 8
SEQ = 4096
DEPTH = 2

N_MIXERS = 2
HEAD_DIM = 128
N_HEADS = D_MODEL // HEAD_DIM
ROPE_DIM = HEAD_DIM // 4
ROPE_THETA = 500000.0
MOBA_BLOCK = 256
MOBA_TOPK = 3
MOBA_QCHUNK = 16
HGRN_EXPAND = 128
HGRN_HEADS = D_MODEL // HGRN_EXPAND
HGRN_CHUNK = 64
D_FF = 2816
N_EXPERTS = 8
TOP_K = 2
D_FF_EXPERT = 3584
ALPHA = (2.0 * DEPTH) ** 0.25
BETA = (8.0 * DEPTH) ** -0.25
LN_EPS = 1e-5
RMS_EPS = 1e-6
N_EVEN = (DEPTH + 1) // 2
N_ODD = DEPTH // 2

kernel_name = 'hybrid_moba_hgrn2_moe_deepnorm'


def layer_norm(x, g, b):
    xf = x.astype(jnp.float32)
    mu = jnp.mean(xf, axis=-1, keepdims=True)
    var = jnp.mean(jnp.square(xf - mu), axis=-1, keepdims=True)
    return ((xf - mu) * lax.rsqrt(var + LN_EPS) * g.astype(jnp.float32) + b.astype(jnp.float32)).astype(x.dtype)


def partial_rotary(t, positions):
    half = ROPE_DIM // 2
    inv_freq = jnp.exp(-math.log(ROPE_THETA) * jnp.arange(half, dtype=jnp.float32) * (2.0 / ROPE_DIM))
    ang = positions.astype(jnp.float32)[..., None] * inv_freq
    cos = jnp.cos(ang)[:, :, None, :]
    sin = jnp.sin(ang)[:, :, None, :]
    t1 = t[..., :half].astype(jnp.float32)
    t2 = t[..., half:ROPE_DIM].astype(jnp.float32)
    rot = jnp.concatenate([t1 * cos - t2 * sin, t2 * cos + t1 * sin], axis=-1).astype(t.dtype)
    return jnp.concatenate([rot, t[..., ROPE_DIM:]], axis=-1)


def moba_attention(q, k, v):
    b_, h_, s_, hd = q.shape
    nb = -(-s_ // MOBA_BLOCK)
    pad = nb * MOBA_BLOCK - s_
    k = jnp.pad(k, ((0, 0), (0, 0), (0, pad), (0, 0)))
    v = jnp.pad(v, ((0, 0), (0, 0), (0, pad), (0, 0)))
    kb = k.reshape(b_, h_, nb, MOBA_BLOCK, hd)
    vb = v.reshape(b_, h_, nb, MOBA_BLOCK, hd)
    kmean = jnp.mean(kb.astype(jnp.float32), axis=3)
    n_top = min(MOBA_TOPK, nb)
    scale = hd ** -0.5
    take_blocks = jax.vmap(jax.vmap(lambda blocks, idx: blocks[idx]))
    blk_ids = jnp.arange(nb)
    sel_len = n_top * MOBA_BLOCK

    def chunk(ci):
        q0 = ci * MOBA_QCHUNK
        blk = q0 // MOBA_BLOCK
        qc = lax.dynamic_slice_in_dim(q, q0, MOBA_QCHUNK, axis=2).astype(jnp.float32) * scale
        qpos = q0 + jnp.arange(MOBA_QCHUNK)
        gate = jnp.einsum('bhqd,bhnd->bhqn', qc, kmean)
        gate = jnp.where(blk_ids < blk, gate, -jnp.inf)
        _, idx = lax.top_k(gate, n_top)
        valid = jnp.arange(n_top) < jnp.minimum(blk, n_top)
        k_sel = take_blocks(kb, idx)
        v_sel = take_blocks(vb, idx)
        s_sel = jnp.einsum('bhqd,bhqntd->bhqnt', qc, k_sel.astype(jnp.float32))
        s_sel = jnp.where(valid[:, None], s_sel, -jnp.inf).reshape(b_, h_, MOBA_QCHUNK, sel_len)
        k_own = lax.dynamic_slice_in_dim(k, blk * MOBA_BLOCK, MOBA_BLOCK, axis=2)
        v_own = lax.dynamic_slice_in_dim(v, blk * MOBA_BLOCK, MOBA_BLOCK, axis=2)
        kpos = blk * MOBA_BLOCK + jnp.arange(MOBA_BLOCK)
        s_own = jnp.einsum('bhqd,bhtd->bhqt', qc, k_own.astype(jnp.float32))
        s_own = jnp.where(kpos[None, :] <= qpos[:, None], s_own, -jnp.inf)
        p = jax.nn.softmax(jnp.concatenate([s_sel, s_own], axis=-1), axis=-1)
        p_sel = p[..., :sel_len].reshape(b_, h_, MOBA_QCHUNK, n_top, MOBA_BLOCK)
        p_own = p[..., sel_len:]
        o = (jnp.einsum('bhqnt,bhqntd->bhqd', p_sel, v_sel.astype(jnp.float32))
             + jnp.einsum('bhqt,bhtd->bhqd', p_own, v_own.astype(jnp.float32)))
        return o.astype(q.dtype)

    out = lax.map(chunk, jnp.arange(s_ // MOBA_QCHUNK))
    return out.transpose(1, 2, 0, 3, 4).reshape(b_, h_, s_, hd)


def moba_mixer(u, positions, w_in, w_out):
    b_, s_, d_ = u.shape
    q, k, v = jnp.split(u @ w_in, 3, axis=-1)
    heads = lambda t: t.reshape(b_, s_, N_HEADS, HEAD_DIM)
    q = partial_rotary(heads(q), positions)
    k = partial_rotary(heads(k), positions)
    v = heads(v)
    o = moba_attention(q.transpose(0, 2, 1, 3), k.transpose(0, 2, 1, 3), v.transpose(0, 2, 1, 3))
    return o.transpose(0, 2, 1, 3).reshape(b_, s_, d_) @ w_out


def hgrn2_recurrence(q, k, v, log_f):
    b_, h_, s_, dk = q.shape
    dv = v.shape[-1]
    nc = s_ // HGRN_CHUNK
    ch = lambda t: t.reshape(b_, h_, nc, HGRN_CHUNK, t.shape[-1])
    q, k, v, log_f = ch(q), ch(k), ch(v), ch(log_f)
    g_cum = jnp.cumsum(log_f, axis=3)
    q_dec = q * jnp.exp(g_cum)
    k_dec = k * jnp.exp(-g_cum)
    causal = jnp.tril(jnp.ones((HGRN_CHUNK, HGRN_CHUNK), dtype=bool))
    a = jnp.where(causal, jnp.einsum('bhncd,bhnsd->bhncs', q_dec, k_dec), 0.0)
    o_intra = jnp.einsum('bhncs,bhnsv->bhncv', a, v)
    g_last = g_cum[:, :, :, -1:, :]
    k_state = k * jnp.exp(g_last - g_cum)
    chunk_decay = jnp.exp(g_last[:, :, :, 0, :])

    def step(state, inp):
        qd, ks, vv, dec = inp
        o_inter = jnp.einsum('bhcd,bhdv->bhcv', qd, state)
        state = dec[..., None] * state + jnp.einsum('bhcd,bhcv->bhdv', ks, vv)
        return state, o_inter

    front = lambda t: t.transpose(2, 0, 1, 3, 4)
    s0 = jnp.zeros((b_, h_, dk, dv), jnp.float32)
    _, o_inter = lax.scan(step, s0, (front(q_dec), front(k_state), front(v), chunk_decay.transpose(2, 0, 1, 3)))
    o = o_intra + o_inter.transpose(1, 2, 0, 3, 4)
    return o.reshape(b_, h_, s_, dv)


def hgrn2_mixer(u, w_in, lb_logits, layer_idx, norm_w, w_out):
    b_, s_, d_ = u.shape
    q, f, i, g = jnp.split(u @ w_in, 4, axis=-1)
    sm = jax.nn.softmax(lb_logits.astype(jnp.float32), axis=0)
    lb = jnp.cumsum(sm, axis=0)[layer_idx] - sm[0]
    f_gate = lb + (1.0 - lb) * jax.nn.sigmoid(f.astype(jnp.float32))
    log_f = jnp.log(f_gate)
    key = 1.0 - f_gate
    qf = jax.nn.silu(q.astype(jnp.float32))
    heads = lambda t: t.reshape(b_, s_, HGRN_HEADS, -1).transpose(0, 2, 1, 3)
    o = hgrn2_recurrence(heads(qf), heads(key), heads(i.astype(jnp.float32)), heads(log_f))
    o = o * lax.rsqrt(jnp.mean(jnp.square(o), axis=-1, keepdims=True) + RMS_EPS)
    o = o.transpose(0, 2, 1, 3).reshape(b_, s_, d_) * norm_w.astype(jnp.float32)
    o = o * jax.nn.silu(g.astype(jnp.float32))
    return o.astype(u.dtype) @ w_out


def swiglu(u, w_gate, w_up, w_down):
    return (jax.nn.silu(u @ w_gate) * (u @ w_up)) @ w_down


def moe_swiglu(u, w_router, w_gate, w_up, w_down):
    b_, s_, d_ = u.shape
    t = u.reshape(-1, d_)
    logits = (t @ w_router).astype(jnp.float32)
    top_val, top_idx = lax.top_k(logits, TOP_K)
    top_w = jax.nn.softmax(top_val, axis=-1)
    combine = jnp.sum(jax.nn.one_hot(top_idx, N_EXPERTS, dtype=jnp.float32) * top_w[..., None], axis=1)
    y = jnp.zeros(t.shape, jnp.float32)
    for e in range(N_EXPERTS):
        y = y + combine[:, e:e + 1] * swiglu(t, w_gate[e], w_up[e], w_down[e]).astype(jnp.float32)
    return y.reshape(b_, s_, d_).astype(u.dtype)


def setup_inputs(seed: int = 0) -> dict:
    key = jax.random.key(seed)
    ks = jax.random.split(key, 24)
    d = D_MODEL
    nrm = lambda k, shape, s: jax.random.normal(k, shape, jnp.float32) * s
    x = nrm(ks[0], (BATCH, SEQ, d), 1.0)
    c = nrm(ks[1], (BATCH, d), 1.0)
    offsets = jax.random.randint(ks[2], (BATCH, 1), 0, 1024, dtype=jnp.int32)
    positions = offsets + jnp.arange(SEQ, dtype=jnp.int32)[None, :]
    ada_w = nrm(ks[3], (DEPTH, d, 6 * d), 0.1 * d ** -0.5)
    ada_b = nrm(ks[4], (DEPTH, 6 * d), 0.01)
    ln_g = 1.0 + nrm(ks[5], (DEPTH, 2, d), 0.02)
    ln_b = nrm(ks[6], (DEPTH, 2, d), 0.02)
    attn_w_in = nrm(ks[7], (N_EVEN, d, 3 * d), d ** -0.5)
    attn_w_out = nrm(ks[8], (N_EVEN, d, d), BETA * d ** -0.5)
    rec_w_in = nrm(ks[9], (N_ODD, d, 4 * d), d ** -0.5)
    rec_lb_logits = 1.0 + nrm(ks[10], (DEPTH, d), 0.1)
    rec_norm_w = 1.0 + nrm(ks[11], (N_ODD, d), 0.02)
    rec_w_out = nrm(ks[12], (N_ODD, d, d), BETA * d ** -0.5)
    ffn_w_gate = nrm(ks[13], (N_EVEN, d, D_FF), d ** -0.5)
    ffn_w_up = nrm(ks[14], (N_EVEN, d, D_FF), d ** -0.5)
    ffn_w_down = nrm(ks[15], (N_EVEN, D_FF, d), BETA * D_FF ** -0.5)
    router_w = nrm(ks[16], (N_ODD, d, N_EXPERTS), d ** -0.5)
    moe_w_gate = nrm(ks[17], (N_ODD, N_EXPERTS, d, D_FF_EXPERT), d ** -0.5)
    moe_w_up = nrm(ks[18], (N_ODD, N_EXPERTS, d, D_FF_EXPERT), d ** -0.5)
    moe_w_down = nrm(ks[19], (N_ODD, N_EXPERTS, D_FF_EXPERT, d), BETA * D_FF_EXPERT ** -0.5)
    return {'x': x, 'c': c, 'positions': positions, 'ada_w': ada_w, 'ada_b': ada_b,
            'ln_g': ln_g, 'ln_b': ln_b, 'attn_w_in': attn_w_in, 'attn_w_out': attn_w_out,
            'rec_w_in': rec_w_in, 'rec_lb_logits': rec_lb_logits, 'rec_norm_w': rec_norm_w,
            'rec_w_out': rec_w_out, 'ffn_w_gate': ffn_w_gate, 'ffn_w_up': ffn_w_up,
            'ffn_w_down': ffn_w_down, 'router_w': router_w, 'moe_w_gate': moe_w_gate,
            'moe_w_up': moe_w_up, 'moe_w_down': moe_w_down}


def reference(x, c, positions, ada_w, ada_b, ln_g, ln_b, attn_w_in, attn_w_out,
              rec_w_in, rec_lb_logits, rec_norm_w, rec_w_out, ffn_w_gate, ffn_w_up,
              ffn_w_down, router_w, moe_w_gate, moe_w_up, moe_w_down):
    mods = jnp.einsum('bd,ldm->lbm', jax.nn.silu(c), ada_w) + ada_b[:, None, :]
    for i in range(DEPTH):
        shift_m, scale_m, gate_m, shift_f, scale_f, gate_f = jnp.split(mods[i][:, None, :], 6, axis=-1)
        j = i // 2
        u = x * (1.0 + scale_m) + shift_m
        if i % N_MIXERS == 0:
            y = moba_mixer(u, positions, attn_w_in[j], attn_w_out[j])
        else:
            y = hgrn2_mixer(u, rec_w_in[j], rec_lb_logits, i, rec_norm_w[j], rec_w_out[j])
        x = layer_norm(ALPHA * x + (1.0 + gate_m) * y, ln_g[i, 0], ln_b[i, 0])
        u = x * (1.0 + scale_f) + shift_f
        if i % 2 == 0:
            y = swiglu(u, ffn_w_gate[j], ffn_w_up[j], ffn_w_down[j])
        else:
            y = moe_swiglu(u, router_w[j], moe_w_gate[j], moe_w_up[j], moe_w_down[j])
        x = layer_norm(ALPHA * x + (1.0 + gate_f) * y, ln_g[i, 1], ln_b[i, 1])
    return x
```

```python
import functools
import math

import jax
import jax.numpy as jnp
from jax import lax
from jax.experimental import pallas as pl
from jax.experimental.pallas import tpu as pltpu

HEAD_DIM = 128
ROPE_DIM = HEAD_DIM // 4
ROPE_THETA = 500000.0
MOBA_BLOCK = 256
MOBA_TOPK = 3
HGRN_EXPAND = 128
HGRN_CHUNK = 64
N_EXPERTS = 8
TOP_K = 2
LN_EPS = 1e-5
RMS_EPS = 1e-6

LANES = 128
MXU_N = 256
VMEM_LIMIT = 56 * 1024 * 1024
NEG_BIG = -1e30

F32 = jnp.float32
BF16 = jnp.bfloat16
HIGHEST = lax.Precision.HIGHEST


def _params(*sem):
    return pltpu.CompilerParams(dimension_semantics=sem, vmem_limit_bytes=VMEM_LIMIT)


def _dot(a, b):
    return jnp.dot(a, b, preferred_element_type=F32)


def _dot_nt(a, b, precision=None):
    return lax.dot_general(a, b, (((1,), (1,)), ((), ())), precision=precision,
                           preferred_element_type=F32)


def _silu(x):
    return x * jax.nn.sigmoid(x)


def _layer_norm(z, g, b):
    mu = jnp.mean(z, axis=-1, keepdims=True)
    d = z - mu
    var = jnp.mean(d * d, axis=-1, keepdims=True)
    return d * lax.rsqrt(var + LN_EPS) * g + b


def _ada_kernel(c_ref, w_ref, b_ref, o_ref):
    a = _silu(c_ref[...])
    o_ref[0] = jnp.dot(a, w_ref[0], precision=HIGHEST, preferred_element_type=F32) + b_ref[0]


def _ada_mods(c, ada_w, ada_b):
    depth, d, m = ada_w.shape
    b = c.shape[0]
    tn = m // 4
    return pl.pallas_call(
        _ada_kernel,
        grid=(depth, m // tn),
        in_specs=[pl.BlockSpec((b, d), lambda l, j: (0, 0)),
                  pl.BlockSpec((1, d, tn), lambda l, j: (l, 0, j)),
                  pl.BlockSpec((1, 1, tn), lambda l, j: (l, 0, j))],
        out_specs=pl.BlockSpec((1, b, tn), lambda l, j: (l, 0, j)),
        out_shape=jax.ShapeDtypeStruct((depth, b, m), F32),
        compiler_params=_params("arbitrary", "arbitrary"),
        name="ada_mods",
    )(c, ada_w, ada_b.reshape(depth, 1, m))


def _qkv_kernel(x_ref, sc_ref, sh_ref, pos_ref, w_ref, o_ref, *, n_rot):
    u = (x_ref[0] * sc_ref[0] + sh_ref[0]).astype(BF16)
    half = ROPE_DIM // 2
    lane = lax.broadcasted_iota(jnp.int32, (1, LANES), 1)
    fidx = (lane % half).astype(F32)
    inv = jnp.exp(-math.log(ROPE_THETA) * fidx * (2.0 / ROPE_DIM))
    inv = jnp.where(lane < ROPE_DIM, inv, 0.0)
    ang = pos_ref[0].astype(F32) * inv
    cos_t = jnp.cos(ang)
    sin_t = jnp.sin(ang)
    sin_lo = jnp.where(lane < half, -sin_t, 0.0)
    sin_hi = jnp.where((lane >= half) & (lane < ROPE_DIM), sin_t, 0.0)
    n_cols = w_ref.shape[1]
    for c in range(n_cols // MXU_N):
        y = _dot(u, w_ref[:, c * MXU_N:(c + 1) * MXU_N])
        for s in range(MXU_N // HEAD_DIM):
            col = c * MXU_N + s * HEAD_DIM
            t = y[:, s * HEAD_DIM:(s + 1) * HEAD_DIM]
            if col < n_rot:
                t = (t * cos_t + pltpu.roll(t, HEAD_DIM - half, 1) * sin_lo
                     + pltpu.roll(t, half, 1) * sin_hi)
            o_ref[0, :, col:col + HEAD_DIM] = t.astype(BF16)


def _qkv_proj(x, sc, sh, pos, w, ts):
    b, s, d = x.shape
    n = w.shape[1]
    return pl.pallas_call(
        functools.partial(_qkv_kernel, n_rot=2 * d),
        grid=(b, s // ts),
        in_specs=[pl.BlockSpec((1, ts, d), lambda i, j: (i, j, 0)),
                  pl.BlockSpec((1, 1, d), lambda i, j: (i, 0, 0)),
                  pl.BlockSpec((1, 1, d), lambda i, j: (i, 0, 0)),
                  pl.BlockSpec((1, ts, 1), lambda i, j: (i, j, 0)),
                  pl.BlockSpec((d, n), lambda i, j: (0, 0))],
        out_specs=pl.BlockSpec((1, ts, n), lambda i, j: (i, j, 0)),
        out_shape=jax.ShapeDtypeStruct((b, s, n), BF16),
        compiler_params=_params("arbitrary", "arbitrary"),
        name="qkv_proj",
    )(x, sc, sh, pos, w)


def _moba_kernel(q_ref, k_ref, v_ref, o_ref, kmean_scr, vt_scr, bias_scr, *, nb, scale):
    blk = MOBA_BLOCK
    i = pl.program_id(2)

    @pl.when(i == 0)
    def _():
        for n in range(nb):
            kb = k_ref[0, n * blk:(n + 1) * blk, :].astype(F32)
            kmean_scr[n:n + 1, :] = jnp.mean(kb, axis=0, keepdims=True)
            vt_scr[n] = v_ref[0, n * blk:(n + 1) * blk, :].astype(F32).T.astype(BF16)

    q = q_ref[0]
    gate = _dot_nt(kmean_scr[...], q.astype(F32), precision=HIGHEST)
    blk_id = lax.broadcasted_iota(jnp.int32, gate.shape, 0)
    past = blk_id < i
    g = jnp.where(past, gate, -jnp.inf)
    rank = jnp.zeros(gate.shape, jnp.int32)
    for m in range(nb):
        gm = g[m:m + 1, :]
        beats = jnp.where(gm > g, 1, jnp.where(gm == g, jnp.where(blk_id > m, 1, 0), 0))
        rank = rank + beats
    bias_scr[...] = jnp.where(past, jnp.where(rank < MOBA_TOPK, 0.0, NEG_BIG), NEG_BIG)

    own0 = pl.multiple_of(i * blk, blk)
    s = _dot_nt(k_ref[0, pl.ds(own0, blk), :], q) * scale
    kpos = lax.broadcasted_iota(jnp.int32, s.shape, 0)
    qpos = lax.broadcasted_iota(jnp.int32, s.shape, 1)
    s = jnp.where(kpos <= qpos, s, NEG_BIG)
    m0 = jnp.max(s, axis=0, keepdims=True)
    p = jnp.exp(s - m0)
    l0 = jnp.sum(p, axis=0, keepdims=True)
    acc0 = _dot(vt_scr[i], p.astype(BF16))

    def body(n, carry):
        m_run, l_run, acc = carry
        k0 = pl.multiple_of(n * blk, blk)
        s = _dot_nt(k_ref[0, pl.ds(k0, blk), :], q) * scale + bias_scr[pl.ds(n, 1), :]
        m_new = jnp.maximum(m_run, jnp.max(s, axis=0, keepdims=True))
        alpha = jnp.exp(m_run - m_new)
        p = jnp.exp(s - m_new)
        l_new = alpha * l_run + jnp.sum(p, axis=0, keepdims=True)
        acc = alpha * acc + _dot(vt_scr[n], p.astype(BF16))
        return m_new, l_new, acc

    _, l_fin, acc = lax.fori_loop(0, i, body, (m0, l0, acc0))
    o_ref[0] = (acc / l_fin).T.astype(BF16)


def _moba_attention(qkv, d):
    b, s, _ = qkv.shape
    h = d // HEAD_DIM
    blk = MOBA_BLOCK
    nb = s // blk
    return pl.pallas_call(
        functools.partial(_moba_kernel, nb=nb, scale=HEAD_DIM ** -0.5),
        grid=(b, h, nb),
        in_specs=[pl.BlockSpec((1, blk, HEAD_DIM), lambda bi, hi, i: (bi, i, hi)),
                  pl.BlockSpec((1, s, HEAD_DIM), lambda bi, hi, i: (bi, 0, h + hi)),
                  pl.BlockSpec((1, s, HEAD_DIM), lambda bi, hi, i: (bi, 0, 2 * h + hi))],
        out_specs=pl.BlockSpec((1, blk, HEAD_DIM), lambda bi, hi, i: (bi, i, hi)),
        out_shape=jax.ShapeDtypeStruct((b, s, d), BF16),
        scratch_shapes=[pltpu.VMEM((nb, HEAD_DIM), F32),
                        pltpu.VMEM((nb, HEAD_DIM, blk), BF16),
                        pltpu.VMEM((nb, blk), F32)],
        compiler_params=_params("arbitrary", "arbitrary", "arbitrary"),
        name="moba_attention",
    )(qkv, qkv, qkv)


def _route_top2(u, wr_ref):
    logits = jnp.dot(u, wr_ref[...], precision=HIGHEST, preferred_element_type=F32)
    lane = lax.broadcasted_iota(jnp.int32, logits.shape, 1)
    logits = jnp.where(lane < N_EXPERTS, logits, -jnp.inf)
    m1 = jnp.max(logits, axis=-1, keepdims=True)
    i1 = jnp.min(jnp.where(logits == m1, lane, LANES), axis=-1, keepdims=True)
    rest = jnp.where(lane == i1, -jnp.inf, logits)
    m2 = jnp.max(rest, axis=-1, keepdims=True)
    i2 = jnp.min(jnp.where(rest == m2, lane, LANES), axis=-1, keepdims=True)
    e2 = jnp.exp(m2 - m1)
    w1 = 1.0 / (1.0 + e2)
    w2 = e2 / (1.0 + e2)
    return jnp.where(lane == 0, i1.astype(F32),
                     jnp.where(lane == 1, i2.astype(F32),
                               jnp.where(lane == 2, w1, jnp.where(lane == 3, w2, 0.0))))


def _proj_ln_kernel(a_ref, w_ref, x_ref, g1p_ref, lng_ref, lnb_ref, sc_ref, sh_ref, *rest,
                    alpha, route):
    if route:
        wr_ref, xo_ref, uo_ref, ro_ref = rest
    else:
        xo_ref, uo_ref = rest
    y = _dot(a_ref[0], w_ref[...])
    xn = _layer_norm(alpha * x_ref[0] + g1p_ref[0] * y, lng_ref[...], lnb_ref[...])
    xo_ref[0] = xn
    u = xn * sc_ref[0] + sh_ref[0]
    uo_ref[0] = u.astype(BF16)
    if route:
        ro_ref[0] = _route_top2(u, wr_ref)


def _proj_ln(a, w, x, g1p, lng, lnb, sc, sh, alpha, tm, wr=None):
    b, s, d = x.shape
    route = wr is not None
    tile = pl.BlockSpec((1, tm, d), lambda i, j: (i, j, 0))
    per_b = pl.BlockSpec((1, 1, d), lambda i, j: (i, 0, 0))
    vec = pl.BlockSpec((1, d), lambda i, j: (0, 0))
    in_specs = [tile, pl.BlockSpec((d, d), lambda i, j: (0, 0)), tile, per_b, vec, vec, per_b, per_b]
    out_specs = [tile, tile]
    out_shape = [jax.ShapeDtypeStruct((b, s, d), F32), jax.ShapeDtypeStruct((b, s, d), BF16)]
    args = [a, w, x, g1p, lng, lnb, sc, sh]
    if route:
        in_specs.append(pl.BlockSpec((d, LANES), lambda i, j: (0, 0)))
        out_specs.append(pl.BlockSpec((1, tm, LANES), lambda i, j: (i, j, 0)))
        out_shape.append(jax.ShapeDtypeStruct((b, s, LANES), F32))
        args.append(wr)
    return pl.pallas_call(
        functools.partial(_proj_ln_kernel, alpha=alpha, route=route),
        grid=(b, s // tm),
        in_specs=in_specs, out_specs=out_specs, out_shape=out_shape,
        compiler_params=_params("arbitrary", "arbitrary"),
        name="proj_ln_route" if route else "proj_ln",
    )(*args)


def _ffn_kernel(u_ref, wg_ref, wu_ref, wd_ref, x_ref, g1p_ref, lng_ref, lnb_ref, xo_ref, h_scr,
                *, alpha, tf):
    u = u_ref[0]
    f = wg_ref.shape[1]
    for j in range(f // tf):
        cols = slice(j * tf, (j + 1) * tf)
        hj = _silu(_dot(u, wg_ref[:, cols])) * _dot(u, wu_ref[:, cols])
        h_scr[:, cols] = hj.astype(BF16)
    y = _dot(h_scr[...], wd_ref[...])
    xo_ref[0] = _layer_norm(alpha * x_ref[0] + g1p_ref[0] * y, lng_ref[...], lnb_ref[...])


def _ffn_dense(u, wg, wu, wd, x, g1p, lng, lnb, alpha, tm):
    b, s, d = x.shape
    f = wg.shape[1]
    tf = MXU_N
    tile = pl.BlockSpec((1, tm, d), lambda i, j: (i, j, 0))
    per_b = pl.BlockSpec((1, 1, d), lambda i, j: (i, 0, 0))
    vec = pl.BlockSpec((1, d), lambda i, j: (0, 0))
    resident = dict(pipeline_mode=pl.Buffered(1))
    return pl.pallas_call(
        functools.partial(_ffn_kernel, alpha=alpha, tf=tf),
        grid=(b, s // tm),
        in_specs=[tile,
                  pl.BlockSpec((d, f), lambda i, j: (0, 0), **resident),
                  pl.BlockSpec((d, f), lambda i, j: (0, 0), **resident),
                  pl.BlockSpec((f, d), lambda i, j: (0, 0), **resident),
                  tile, per_b, vec, vec],
        out_specs=tile,
        out_shape=jax.ShapeDtypeStruct((b, s, d), F32),
        scratch_shapes=[pltpu.VMEM((tm, f), BF16)],
        compiler_params=_params("arbitrary", "arbitrary"),
        name="ffn_dense",
    )(u, wg, wu, wd, x, g1p, lng, lnb)


def _rec_in_kernel(x_ref, sc_ref, sh_ref, w_ref, lbl_ref, qf_ref, key_ref, v_ref, gs_ref, lf_ref,
                   *, layer_idx):
    u = (x_ref[0] * sc_ref[0] + sh_ref[0]).astype(BF16)
    d = x_ref.shape[2]
    lbl = lbl_ref[...]
    e = jnp.exp(lbl - jnp.max(lbl, axis=0, keepdims=True))
    sm = e / jnp.sum(e, axis=0, keepdims=True)
    lb = jnp.zeros((1, d), F32)
    for r in range(1, layer_idx + 1):
        lb = lb + sm[r:r + 1, :]
    for c in range(d // MXU_N):
        cols = slice(c * MXU_N, (c + 1) * MXU_N)
        q = _dot(u, w_ref[:, c * MXU_N:(c + 1) * MXU_N])
        qf_ref[0, :, cols] = _silu(q).astype(BF16)
        f = _dot(u, w_ref[:, d + c * MXU_N:d + (c + 1) * MXU_N])
        lbc = lb[:, cols]
        f_gate = lbc + (1.0 - lbc) * jax.nn.sigmoid(f)
        lf_ref[0, :, cols] = jnp.log(f_gate)
        key_ref[0, :, cols] = (1.0 - f_gate).astype(BF16)
        v = _dot(u, w_ref[:, 2 * d + c * MXU_N:2 * d + (c + 1) * MXU_N])
        v_ref[0, :, cols] = v.astype(BF16)
        g = _dot(u, w_ref[:, 3 * d + c * MXU_N:3 * d + (c + 1) * MXU_N])
        gs_ref[0, :, cols] = _silu(g).astype(BF16)


def _rec_in(x, sc, sh, w, lb_logits, layer_idx, ts):
    b, s, d = x.shape
    depth = lb_logits.shape[0]
    tile = pl.BlockSpec((1, ts, d), lambda i, j: (i, j, 0))
    per_b = pl.BlockSpec((1, 1, d), lambda i, j: (i, 0, 0))
    bf = jax.ShapeDtypeStruct((b, s, d), BF16)
    return pl.pallas_call(
        functools.partial(_rec_in_kernel, layer_idx=layer_idx),
        grid=(b, s // ts),
        in_specs=[tile, per_b, per_b,
                  pl.BlockSpec((d, 4 * d), lambda i, j: (0, 0)),
                  pl.BlockSpec((depth, d), lambda i, j: (0, 0))],
        out_specs=[tile] * 5,
        out_shape=[bf, bf, bf, bf, jax.ShapeDtypeStruct((b, s, d), F32)],
        compiler_params=_params("arbitrary", "arbitrary"),
        name="rec_in",
    )(x, sc, sh, w, lb_logits)


def _hgrn_kernel(qf_ref, key_ref, v_ref, lf_ref, gs_ref, nw_ref, o_ref, st_scr, *, n_heads):
    cs = HGRN_CHUNK
    dk = HGRN_EXPAND

    @pl.when(pl.program_id(1) == 0)
    def _():
        st_scr[...] = jnp.zeros(st_scr.shape, F32)

    row = lax.broadcasted_iota(jnp.int32, (cs, cs), 0)
    col = lax.broadcasted_iota(jnp.int32, (cs, cs), 1)
    causal = row >= col
    tri = jnp.where(causal, 1.0, 0.0).astype(BF16)
    n_chunks = qf_ref.shape[1] // cs

    def chunk(c, carry):
        rows = pl.ds(pl.multiple_of(c * cs, cs), cs)
        lf = lf_ref[0, rows, :]
        hi = lf.astype(BF16)
        r1 = lf - hi.astype(F32)
        mid = r1.astype(BF16)
        lo = (r1 - mid.astype(F32)).astype(BF16)
        g_cum = _dot(tri, hi) + _dot(tri, mid) + _dot(tri, lo)
        for h in range(n_heads):
            cols = slice(h * dk, (h + 1) * dk)
            g = g_cum[:, cols]
            g_last = g[cs - 1:cs, :]
            qf = qf_ref[0, rows, cols].astype(F32)
            key = key_ref[0, rows, cols].astype(F32)
            vh = v_ref[0, rows, cols]
            q_dec = (qf * jnp.exp(g)).astype(BF16)
            k_dec = (key * jnp.exp(-g)).astype(BF16)
            k_state = (key * jnp.exp(g_last - g)).astype(BF16)
            a = jnp.where(causal, _dot_nt(q_dec, k_dec), 0.0).astype(BF16)
            st = st_scr[h]
            o = _dot(a, vh) + _dot_nt(q_dec, st.astype(BF16))
            v_t = vh.astype(F32).T.astype(BF16)
            st_scr[h] = st * jnp.exp(g_last) + _dot(v_t, k_state)
            ms = jnp.mean(o * o, axis=-1, keepdims=True)
            on = o * lax.rsqrt(ms + RMS_EPS) * nw_ref[:, cols] * gs_ref[0, rows, cols].astype(F32)
            o_ref[0, rows, cols] = on.astype(BF16)
        return carry

    lax.fori_loop(0, n_chunks, chunk, 0)


def _hgrn_recurrence(qf, key, v, lf, gs, norm_w, ts):
    b, s, d = qf.shape
    h = d // HGRN_EXPAND
    tile = pl.BlockSpec((1, ts, d), lambda i, j: (i, j, 0))
    return pl.pallas_call(
        functools.partial(_hgrn_kernel, n_heads=h),
        grid=(b, s // ts),
        in_specs=[tile, tile, tile, tile, tile, pl.BlockSpec((1, d), lambda i, j: (0, 0))],
        out_specs=tile,
        out_shape=jax.ShapeDtypeStruct((b, s, d), BF16),
        scratch_shapes=[pltpu.VMEM((h, HGRN_EXPAND, HGRN_EXPAND), F32)],
        compiler_params=_params("arbitrary", "arbitrary"),
        name="hgrn_recurrence",
    )(qf, key, v, lf, gs, norm_w)


def _gather_kernel(idx_ref, src_ref, dst_ref, sem, *, rows):
    base = pl.program_id(0) * rows

    def issue(r, carry):
        pltpu.make_async_copy(src_ref.at[idx_ref[r]], dst_ref.at[base + r], sem).start()
        return carry

    lax.fori_loop(0, rows, issue, 0)

    def drain(r, carry):
        pltpu.make_async_copy(src_ref.at[0], dst_ref.at[base + r], sem).wait()
        return carry

    lax.fori_loop(0, rows, drain, 0)


def _gather_rows(src, idx, rows_per_step=1024):
    n_src, d = src.shape
    n_out = idx.shape[0]
    sub = d // LANES
    src3 = src.reshape(n_src, sub, LANES)
    out = pl.pallas_call(
        functools.partial(_gather_kernel, rows=rows_per_step),
        grid=(n_out // rows_per_step,),
        in_specs=[pl.BlockSpec((rows_per_step,), lambda i: (i,), memory_space=pltpu.SMEM),
                  pl.BlockSpec(memory_space=pl.ANY)],
        out_specs=pl.BlockSpec(memory_space=pl.ANY),
        out_shape=jax.ShapeDtypeStruct((n_out, sub, LANES), src.dtype),
        scratch_shapes=[pltpu.SemaphoreType.DMA(())],
        compiler_params=pltpu.CompilerParams(dimension_semantics=("arbitrary",)),
        name="gather_rows",
    )(idx, src3)
    return out.reshape(n_out, d)


def _moe_ffn_kernel(te_ref, na_ref, x_ref, wg_ref, wu_ref, wd_ref, o_ref, acc_scr):
    i = pl.program_id(0)
    j = pl.program_id(1)
    active = i < na_ref[0]

    @pl.when(active)
    def _():
        x = x_ref[...]
        h = (_silu(_dot(x, wg_ref[0])) * _dot(x, wu_ref[0])).astype(BF16)
        y = _dot(h, wd_ref[0])

        @pl.when(j == 0)
        def _():
            acc_scr[...] = y

        @pl.when(j > 0)
        def _():
            acc_scr[...] += y

    @pl.when(j == pl.num_programs(1) - 1)
    def _():
        o_ref[...] = jnp.where(active, acc_scr[...], 0.0).astype(o_ref.dtype)


def _moe_ffn(xs, wg, wu, wd, tile_expert, n_active, tm, tf):
    p, d = xs.shape
    f = wg.shape[2]
    nf = f // tf

    def frozen(j, i, na):
        return jnp.where(i < na[0], j, nf - 1)

    grid_spec = pltpu.PrefetchScalarGridSpec(
        num_scalar_prefetch=2,
        grid=(p // tm, nf),
        in_specs=[pl.BlockSpec((tm, d), lambda i, j, te, na: (jnp.minimum(i, na[0] - 1), 0)),
                  pl.BlockSpec((1, d, tf), lambda i, j, te, na: (te[i], 0, frozen(j, i, na))),
                  pl.BlockSpec((1, d, tf), lambda i, j, te, na: (te[i], 0, frozen(j, i, na))),
                  pl.BlockSpec((1, tf, d), lambda i, j, te, na: (te[i], frozen(j, i, na), 0))],
        out_specs=pl.BlockSpec((tm, d), lambda i, j, te, na: (i, 0)),
        scratch_shapes=[pltpu.VMEM((tm, d), F32)],
    )
    return pl.pallas_call(
        _moe_ffn_kernel,
        grid_spec=grid_spec,
        out_shape=jax.ShapeDtypeStruct((p, d), BF16),
        compiler_params=_params("arbitrary", "arbitrary"),
        name="moe_ffn",
    )(tile_expert, n_active, xs, wg, wu, wd)


def _combine_ln_kernel(ya_ref, yb_ref, r_ref, x_ref, g1p_ref, lng_ref, lnb_ref, xo_ref, *, alpha):
    r = r_ref[0]
    y = r[:, 2:3] * ya_ref[0].astype(F32) + r[:, 3:4] * yb_ref[0].astype(F32)
    xo_ref[0] = _layer_norm(alpha * x_ref[0] + g1p_ref[0] * y, lng_ref[...], lnb_ref[...])


def _combine_ln(ya, yb, route, x, g1p, lng, lnb, alpha, tm):
    b, s, d = x.shape
    tile = pl.BlockSpec((1, tm, d), lambda i, j: (i, j, 0))
    per_b = pl.BlockSpec((1, 1, d), lambda i, j: (i, 0, 0))
    vec = pl.BlockSpec((1, d), lambda i, j: (0, 0))
    return pl.pallas_call(
        functools.partial(_combine_ln_kernel, alpha=alpha),
        grid=(b, s // tm),
        in_specs=[tile, tile, pl.BlockSpec((1, tm, LANES), lambda i, j: (i, j, 0)), tile, per_b, vec, vec],
        out_specs=tile,
        out_shape=jax.ShapeDtypeStruct((b, s, d), F32),
        compiler_params=_params("arbitrary", "arbitrary"),
        name="combine_ln",
    )(ya, yb, route, x, g1p, lng, lnb)


def _moe_plan(route, tm):
    n = route.shape[0]
    e = N_EXPERTS
    n_tiles = TOP_K * n // tm + e
    flat_e = route[:, :TOP_K].astype(jnp.int32).reshape(-1)
    onehot = (flat_e[:, None] == jnp.arange(e, dtype=jnp.int32)[None, :]).astype(jnp.int32)
    incl = jnp.cumsum(onehot, axis=0)
    counts = incl[-1]
    padded = (counts + tm - 1) // tm * tm
    ends = jnp.cumsum(padded)
    offs = ends - padded
    slot = jnp.sum(onehot * (offs[None, :] + incl - 1), axis=1)
    src = jnp.zeros((n_tiles * tm,), jnp.int32).at[slot].set(jnp.arange(TOP_K * n, dtype=jnp.int32) // TOP_K)
    n_active = ends[-1] // tm
    tile_id = jnp.minimum(jnp.arange(n_tiles, dtype=jnp.int32), n_active - 1)
    tile_expert = jnp.minimum(jnp.sum((tile_id[:, None] * tm >= ends[None, :]).astype(jnp.int32), axis=1), e - 1)
    return slot.reshape(n, TOP_K), src, tile_expert.astype(jnp.int32), n_active.reshape(1).astype(jnp.int32)


def _moe_swiglu_ln(u, route, wg, wu, wd, x, g1p, lng, lnb, alpha, tm_moe=512, tf=512, tm_ln=512):
    b, s, d = x.shape
    n = b * s
    slot, src, tile_expert, n_active = _moe_plan(route.reshape(n, LANES), tm_moe)
    xs = _gather_rows(u.reshape(n, d), src)
    ys = _moe_ffn(xs, wg, wu, wd, tile_expert, n_active, tm_moe, tf)
    ya = _gather_rows(ys, slot[:, 0]).reshape(b, s, d)
    yb = _gather_rows(ys, slot[:, 1]).reshape(b, s, d)
    return _combine_ln(ya, yb, route, x, g1p, lng, lnb, alpha, tm_ln)


def kernel(x, c, positions, ada_w, ada_b, ln_g, ln_b, attn_w_in, attn_w_out, rec_w_in, rec_lb_logits,
           rec_norm_w, rec_w_out, ffn_w_gate, ffn_w_up, ffn_w_down, router_w, moe_w_gate, moe_w_up,
           moe_w_down):
    b, s, d = x.shape
    depth = ada_w.shape[0]
    alpha = (2.0 * depth) ** 0.25
    ts = 512

    mods = _ada_mods(c, ada_w, ada_b).reshape(depth, b, 6, 1, d)
    one_plus = lambda t: 1.0 + t
    pos = positions.reshape(b, s, 1)

    for i in range(depth):
        shift_m, scale_m, gate_m, shift_f, scale_f, gate_f = (mods[i, :, r] for r in range(6))
        j = i // 2
        lng = ln_g[i].reshape(2, 1, d)
        lnb = ln_b[i].reshape(2, 1, d)
        if i % 2 == 0:
            qkv = _qkv_proj(x, one_plus(scale_m), shift_m, pos, attn_w_in[j].astype(BF16), ts)
            o = _moba_attention(qkv, d)
            x, u = _proj_ln(o, attn_w_out[j].astype(BF16), x, one_plus(gate_m), lng[0], lnb[0],
                            one_plus(scale_f), shift_f, alpha, ts)
            x = _ffn_dense(u, ffn_w_gate[j].astype(BF16), ffn_w_up[j].astype(BF16),
                           ffn_w_down[j].astype(BF16), x, one_plus(gate_f), lng[1], lnb[1], alpha, ts)
        else:
            qf, key, v, gs, lf = _rec_in(x, one_plus(scale_m), shift_m, rec_w_in[j].astype(BF16),
                                         rec_lb_logits, i, ts)
            o = _hgrn_recurrence(qf, key, v, lf, gs, rec_norm_w[j].reshape(1, d), ts)
            wr = jnp.zeros((d, LANES), F32).at[:, :N_EXPERTS].set(router_w[j])
            x, u, route = _proj_ln(o, rec_w_out[j].astype(BF16), x, one_plus(gate_m), lng[0], lnb[0],
                                   one_plus(scale_f), shift_f, alpha, ts, wr=wr)
            x = _moe_swiglu_ln(u, route, moe_w_gate[j].astype(BF16), moe_w_up[j].astype(BF16),
                               moe_w_down[j].astype(BF16), x, one_plus(gate_f), lng[1], lnb[1], alpha)
    return x
```

```python
import functools
import math

import jax
import jax.numpy as jnp
from jax import lax
from jax.experimental import pallas as pl
from jax.experimental.pallas import tpu as pltpu

HEAD_DIM = 128
ROPE_DIM = HEAD_DIM // 4
ROPE_THETA = 500000.0
MOBA_BLOCK = 256
MOBA_TOPK = 3
HGRN_EXPAND = 128
HGRN_CHUNK = 64
N_EXPERTS = 8
TOP_K = 2
LN_EPS = 1e-5
RMS_EPS = 1e-6

LANES = 128
MXU_N = 256
VMEM_LIMIT = 56 * 1024 * 1024
NEG_BIG = -1e30

F32 = jnp.float32
BF16 = jnp.bfloat16
HIGHEST = lax.Precision.HIGHEST


def _params(*sem):
    return pltpu.CompilerParams(dimension_semantics=sem, vmem_limit_bytes=VMEM_LIMIT)


def _dot(a, b):
    return jnp.dot(a, b, preferred_element_type=F32)


def _dot_nt(a, b, precision=None):
    return lax.dot_general(a, b, (((1,), (1,)), ((), ())), precision=precision,
                           preferred_element_type=F32)


def _silu(x):
    return x * jax.nn.sigmoid(x)


def _layer_norm(z, g, b):
    mu = jnp.mean(z, axis=-1, keepdims=True)
    d = z - mu
    var = jnp.mean(d * d, axis=-1, keepdims=True)
    return d * lax.rsqrt(var + LN_EPS) * g + b


def _ada_kernel(c_ref, w_ref, b_ref, o_ref):
    a = _silu(c_ref[...])
    o_ref[0] = jnp.dot(a, w_ref[0], precision=HIGHEST, preferred_element_type=F32) + b_ref[0]


def _ada_mods(c, ada_w, ada_b):
    depth, d, m = ada_w.shape
    b = c.shape[0]
    tn = m // 4
    return pl.pallas_call(
        _ada_kernel,
        grid=(depth, m // tn),
        in_specs=[pl.BlockSpec((b, d), lambda l, j: (0, 0)),
                  pl.BlockSpec((1, d, tn), lambda l, j: (l, 0, j)),
                  pl.BlockSpec((1, 1, tn), lambda l, j: (l, 0, j))],
        out_specs=pl.BlockSpec((1, b, tn), lambda l, j: (l, 0, j)),
        out_shape=jax.ShapeDtypeStruct((depth, b, m), F32),
        compiler_params=_params("arbitrary", "arbitrary"),
        name="ada_mods",
    )(c, ada_w, ada_b.reshape(depth, 1, m))


def _qkv_kernel(x_ref, sc_ref, sh_ref, pos_ref, w_ref, o_ref, *, n_rot):
    u = (x_ref[0] * sc_ref[0] + sh_ref[0]).astype(BF16)
    half = ROPE_DIM // 2
    lane = lax.broadcasted_iota(jnp.int32, (1, LANES), 1)
    fidx = (lane % half).astype(F32)
    inv = jnp.exp(-math.log(ROPE_THETA) * fidx * (2.0 / ROPE_DIM))
    inv = jnp.where(lane < ROPE_DIM, inv, 0.0)
    ang = pos_ref[0].astype(F32) * inv
    cos_t = jnp.cos(ang)
    sin_t = jnp.sin(ang)
    sin_lo = jnp.where(lane < half, -sin_t, 0.0)
    sin_hi = jnp.where((lane >= half) & (lane < ROPE_DIM), sin_t, 0.0)
    n_cols = w_ref.shape[1]
    for c in range(n_cols // MXU_N):
        y = _dot(u, w_ref[:, c * MXU_N:(c + 1) * MXU_N])
        for s in range(MXU_N // HEAD_DIM):
            col = c * MXU_N + s * HEAD_DIM
            t = y[:, s * HEAD_DIM:(s + 1) * HEAD_DIM]
            if col < n_rot:
                t = (t * cos_t + pltpu.roll(t, HEAD_DIM - half, 1) * sin_lo
                     + pltpu.roll(t, half, 1) * sin_hi)
            if col < n_rot // 2:
                t = t * (HEAD_DIM ** -0.5 * math.log2(math.e))
            o_ref[0, :, col:col + HEAD_DIM] = t.astype(BF16)


def _qkv_proj(x, sc, sh, pos, w, ts):
    b, s, d = x.shape
    n = w.shape[1]
    return pl.pallas_call(
        functools.partial(_qkv_kernel, n_rot=2 * d),
        grid=(b, s // ts),
        in_specs=[pl.BlockSpec((1, ts, d), lambda i, j: (i, j, 0)),
                  pl.BlockSpec((1, 1, d), lambda i, j: (i, 0, 0)),
                  pl.BlockSpec((1, 1, d), lambda i, j: (i, 0, 0)),
                  pl.BlockSpec((1, ts, 1), lambda i, j: (i, j, 0)),
                  pl.BlockSpec((d, n), lambda i, j: (0, 0))],
        out_specs=pl.BlockSpec((1, ts, n), lambda i, j: (i, j, 0)),
        out_shape=jax.ShapeDtypeStruct((b, s, n), BF16),
        compiler_params=_params("arbitrary", "arbitrary"),
        name="qkv_proj",
    )(x, sc, sh, pos, w)


def _moba_kernel(q_ref, k_ref, v_ref, o_ref, kmean_scr, vt_scr, bias_scr, m_scr, l_scr, acc_scr,
                 *, nb, n_grp, ch):
    blk = MOBA_BLOCK
    hd = HEAD_DIM
    cw = ch * blk
    i = pl.program_id(2)

    @pl.when(i == 0)
    def _():
        for hh in range(n_grp):
            for n in range(nb):
                kb = k_ref[0, n * blk:(n + 1) * blk, hh * hd:(hh + 1) * hd].astype(F32)
                kmean_scr[hh, n:n + 1, :] = jnp.mean(kb, axis=0, keepdims=True)
                vb = v_ref[0, n * blk:(n + 1) * blk, hh * hd:(hh + 1) * hd]
                vt_scr[hh, n // ch, :, (n % ch) * blk:(n % ch + 1) * blk] = vb.astype(F32).T.astype(BF16)

    def sweep(hh, c, own):
        q = q_ref[0, :, hh * hd:(hh + 1) * hd]
        k0 = pl.multiple_of(c * cw, cw)
        s = _dot_nt(k_ref[0, pl.ds(k0, cw), hh * hd:(hh + 1) * hd], q)
        if own:
            kpos = k0 + lax.broadcasted_iota(jnp.int32, s.shape, 0)
            qpos = i * blk + lax.broadcasted_iota(jnp.int32, s.shape, 1)
            s = jnp.where(kpos <= qpos, s, NEG_BIG)
        parts = [s[t * blk:(t + 1) * blk] + bias_scr[hh, pl.ds(c * ch + t, 1), :] for t in range(ch)]
        m_blk = parts[0]
        for t in range(1, ch):
            m_blk = jnp.maximum(m_blk, parts[t])
        m_new = jnp.max(m_blk, axis=0, keepdims=True)
        if not own:
            m_new = jnp.maximum(m_new, m_scr[hh])
        ps = [jnp.exp2(part - m_new) for part in parts]
        l_new = ps[0]
        for t in range(1, ch):
            l_new = l_new + ps[t]
        l_new = jnp.sum(l_new, axis=0, keepdims=True)
        pv = _dot(vt_scr[hh, c], jnp.concatenate([p.astype(BF16) for p in ps], axis=0))
        if own:
            l_scr[hh] = l_new
            acc_scr[hh] = pv
        else:
            alpha = jnp.exp2(m_scr[hh] - m_new)
            l_scr[hh] = alpha * l_scr[hh] + l_new
            acc_scr[hh] = alpha * acc_scr[hh] + pv
        m_scr[hh] = m_new

    c_own = i // ch
    for hh in range(n_grp):
        q = q_ref[0, :, hh * hd:(hh + 1) * hd]
        gate = _dot_nt(kmean_scr[hh], q.astype(F32), precision=HIGHEST)
        blk_id = lax.broadcasted_iota(jnp.int32, gate.shape, 0)
        past = blk_id < i
        g = jnp.where(past, gate, -jnp.inf)
        rank = jnp.zeros(gate.shape, jnp.int32)
        for m in range(nb):
            gm = g[m:m + 1, :]
            beats = jnp.where(gm > g, 1, jnp.where(gm == g, jnp.where(blk_id > m, 1, 0), 0))
            rank = rank + beats
        keep = jnp.where(past, jnp.where(rank < MOBA_TOPK, 0.0, NEG_BIG), NEG_BIG)
        bias_scr[hh] = jnp.where(blk_id == i, 0.0, keep)
        sweep(hh, c_own, True)

    def body(c, carry):
        for hh in range(n_grp):
            sweep(hh, c, False)
        return carry

    lax.fori_loop(0, c_own, body, 0)
    for hh in range(n_grp):
        o_ref[0, :, hh * hd:(hh + 1) * hd] = (acc_scr[hh] / l_scr[hh]).T.astype(BF16)


def _moba_attention(qkv, d, n_grp, ch):
    b, s, _ = qkv.shape
    h = d // HEAD_DIM
    blk = MOBA_BLOCK
    nb = s // blk
    gw = n_grp * HEAD_DIM
    ng = h // n_grp
    return pl.pallas_call(
        functools.partial(_moba_kernel, nb=nb, n_grp=n_grp, ch=ch),
        grid=(b, ng, nb),
        in_specs=[pl.BlockSpec((1, blk, gw), lambda bi, hi, i: (bi, i, hi)),
                  pl.BlockSpec((1, s, gw), lambda bi, hi, i: (bi, 0, ng + hi)),
                  pl.BlockSpec((1, s, gw), lambda bi, hi, i: (bi, 0, 2 * ng + hi))],
        out_specs=pl.BlockSpec((1, blk, gw), lambda bi, hi, i: (bi, i, hi)),
        out_shape=jax.ShapeDtypeStruct((b, s, d), BF16),
        scratch_shapes=[pltpu.VMEM((n_grp, nb, HEAD_DIM), F32),
                        pltpu.VMEM((n_grp, nb // ch, HEAD_DIM, ch * blk), BF16),
                        pltpu.VMEM((n_grp, nb, blk), F32),
                        pltpu.VMEM((n_grp, 1, blk), F32),
                        pltpu.VMEM((n_grp, 1, blk), F32),
                        pltpu.VMEM((n_grp, HEAD_DIM, blk), F32)],
        compiler_params=_params("arbitrary", "arbitrary", "arbitrary"),
        name="moba_attention",
    )(qkv, qkv, qkv)


def _route_top2(u, wr_ref):
    logits = jnp.dot(u, wr_ref[...], precision=HIGHEST, preferred_element_type=F32)
    lane = lax.broadcasted_iota(jnp.int32, logits.shape, 1)
    logits = jnp.where(lane < N_EXPERTS, logits, -jnp.inf)
    m1 = jnp.max(logits, axis=-1, keepdims=True)
    i1 = jnp.min(jnp.where(logits == m1, lane, LANES), axis=-1, keepdims=True)
    rest = jnp.where(lane == i1, -jnp.inf, logits)
    m2 = jnp.max(rest, axis=-1, keepdims=True)
    i2 = jnp.min(jnp.where(rest == m2, lane, LANES), axis=-1, keepdims=True)
    e2 = jnp.exp(m2 - m1)
    w1 = 1.0 / (1.0 + e2)
    w2 = e2 / (1.0 + e2)
    return jnp.where(lane == 0, i1.astype(F32),
                     jnp.where(lane == 1, i2.astype(F32),
                               jnp.where(lane == 2, w1, jnp.where(lane == 3, w2, 0.0))))


def _proj_ln_kernel(a_ref, w_ref, x_ref, g1p_ref, lng_ref, lnb_ref, sc_ref, sh_ref, *rest,
                    alpha, route):
    if route:
        wr_ref, xo_ref, uo_ref, ro_ref = rest
    else:
        xo_ref, uo_ref = rest
    y = _dot(a_ref[0], w_ref[...])
    xn = _layer_norm(alpha * x_ref[0] + g1p_ref[0] * y, lng_ref[...], lnb_ref[...])
    xo_ref[0] = xn
    u = xn * sc_ref[0] + sh_ref[0]
    uo_ref[0] = u.astype(uo_ref.dtype)
    if route:
        ro_ref[0] = _route_top2(u, wr_ref)


def _proj_ln(a, w, x, g1p, lng, lnb, sc, sh, alpha, tm, wr=None):
    b, s, d = x.shape
    route = wr is not None
    tile = pl.BlockSpec((1, tm, d), lambda i, j: (i, j, 0))
    per_b = pl.BlockSpec((1, 1, d), lambda i, j: (i, 0, 0))
    vec = pl.BlockSpec((1, d), lambda i, j: (0, 0))
    in_specs = [tile, pl.BlockSpec((d, d), lambda i, j: (0, 0)), tile, per_b, vec, vec, per_b, per_b]
    out_specs = [tile, tile]
    out_shape = [jax.ShapeDtypeStruct((b, s, d), F32),
                 jax.ShapeDtypeStruct((b, s, d), F32 if route else BF16)]
    args = [a, w, x, g1p, lng, lnb, sc, sh]
    if route:
        in_specs.append(pl.BlockSpec((d, LANES), lambda i, j: (0, 0)))
        out_specs.append(pl.BlockSpec((1, tm, LANES), lambda i, j: (i, j, 0)))
        out_shape.append(jax.ShapeDtypeStruct((b, s, LANES), F32))
        args.append(wr)
    return pl.pallas_call(
        functools.partial(_proj_ln_kernel, alpha=alpha, route=route),
        grid=(b, s // tm),
        in_specs=in_specs, out_specs=out_specs, out_shape=out_shape,
        compiler_params=_params("arbitrary", "arbitrary"),
        name="proj_ln_route" if route else "proj_ln",
    )(*args)


def _ffn_kernel(u_ref, wg_ref, wu_ref, wd_ref, x_ref, g1p_ref, lng_ref, lnb_ref, xo_ref, h_scr,
                *, alpha, tf):
    u = u_ref[0]
    f = wg_ref.shape[1]
    for j in range(f // tf):
        cols = slice(j * tf, (j + 1) * tf)
        hj = _silu(_dot(u, wg_ref[:, cols])) * _dot(u, wu_ref[:, cols])
        h_scr[:, cols] = hj.astype(BF16)
    y = _dot(h_scr[...], wd_ref[...])
    xo_ref[0] = _layer_norm(alpha * x_ref[0] + g1p_ref[0] * y, lng_ref[...], lnb_ref[...])


def _ffn_dense(u, wg, wu, wd, x, g1p, lng, lnb, alpha, tm):
    b, s, d = x.shape
    f = wg.shape[1]
    tf = MXU_N
    tile = pl.BlockSpec((1, tm, d), lambda i, j: (i, j, 0))
    per_b = pl.BlockSpec((1, 1, d), lambda i, j: (i, 0, 0))
    vec = pl.BlockSpec((1, d), lambda i, j: (0, 0))
    resident = dict(pipeline_mode=pl.Buffered(1))
    return pl.pallas_call(
        functools.partial(_ffn_kernel, alpha=alpha, tf=tf),
        grid=(b, s // tm),
        in_specs=[tile,
                  pl.BlockSpec((d, f), lambda i, j: (0, 0), **resident),
                  pl.BlockSpec((d, f), lambda i, j: (0, 0), **resident),
                  pl.BlockSpec((f, d), lambda i, j: (0, 0), **resident),
                  tile, per_b, vec, vec],
        out_specs=tile,
        out_shape=jax.ShapeDtypeStruct((b, s, d), F32),
        scratch_shapes=[pltpu.VMEM((tm, f), BF16)],
        compiler_params=_params("arbitrary", "arbitrary"),
        name="ffn_dense",
    )(u, wg, wu, wd, x, g1p, lng, lnb)


def _rec_in_kernel(x_ref, sc_ref, sh_ref, w_ref, lbl_ref, qf_ref, key_ref, v_ref, gs_ref, lf_ref,
                   *, layer_idx):
    u = (x_ref[0] * sc_ref[0] + sh_ref[0]).astype(BF16)
    d = x_ref.shape[2]
    lbl = lbl_ref[...]
    e = jnp.exp(lbl - jnp.max(lbl, axis=0, keepdims=True))
    sm = e / jnp.sum(e, axis=0, keepdims=True)
    lb = jnp.zeros((1, d), F32)
    for r in range(1, layer_idx + 1):
        lb = lb + sm[r:r + 1, :]
    for c in range(d // MXU_N):
        cols = slice(c * MXU_N, (c + 1) * MXU_N)
        q = _dot(u, w_ref[:, c * MXU_N:(c + 1) * MXU_N])
        qf_ref[0, :, cols] = _silu(q).astype(BF16)
        f = _dot(u, w_ref[:, d + c * MXU_N:d + (c + 1) * MXU_N])
        lbc = lb[:, cols]
        f_gate = lbc + (1.0 - lbc) * jax.nn.sigmoid(f)
        lf_ref[0, :, cols] = jnp.log(f_gate)
        key_ref[0, :, cols] = (1.0 - f_gate).astype(BF16)
        v = _dot(u, w_ref[:, 2 * d + c * MXU_N:2 * d + (c + 1) * MXU_N])
        v_ref[0, :, cols] = v.astype(BF16)
        g = _dot(u, w_ref[:, 3 * d + c * MXU_N:3 * d + (c + 1) * MXU_N])
        gs_ref[0, :, cols] = _silu(g).astype(BF16)


def _rec_in(x, sc, sh, w, lb_logits, layer_idx, ts):
    b, s, d = x.shape
    depth = lb_logits.shape[0]
    tile = pl.BlockSpec((1, ts, d), lambda i, j: (i, j, 0))
    per_b = pl.BlockSpec((1, 1, d), lambda i, j: (i, 0, 0))
    bf = jax.ShapeDtypeStruct((b, s, d), BF16)
    return pl.pallas_call(
        functools.partial(_rec_in_kernel, layer_idx=layer_idx),
        grid=(b, s // ts),
        in_specs=[tile, per_b, per_b,
                  pl.BlockSpec((d, 4 * d), lambda i, j: (0, 0)),
                  pl.BlockSpec((depth, d), lambda i, j: (0, 0))],
        out_specs=[tile] * 5,
        out_shape=[bf, bf, bf, bf, jax.ShapeDtypeStruct((b, s, d), F32)],
        compiler_params=_params("arbitrary", "arbitrary"),
        name="rec_in",
    )(x, sc, sh, w, lb_logits)


def _hgrn_kernel(qf_ref, key_ref, v_ref, lf_ref, gs_ref, nw_ref, o_ref, st_scr, *, n_heads):
    cs = HGRN_CHUNK
    dk = HGRN_EXPAND

    @pl.when(pl.program_id(1) == 0)
    def _():
        st_scr[...] = jnp.zeros(st_scr.shape, F32)

    row = lax.broadcasted_iota(jnp.int32, (cs, cs), 0)
    col = lax.broadcasted_iota(jnp.int32, (cs, cs), 1)
    causal = row >= col
    tri = jnp.where(causal, 1.0, 0.0).astype(BF16)
    n_chunks = qf_ref.shape[1] // cs

    def chunk(c, carry):
        rows = pl.ds(pl.multiple_of(c * cs, cs), cs)
        lf = lf_ref[0, rows, :]
        hi = lf.astype(BF16)
        r1 = lf - hi.astype(F32)
        mid = r1.astype(BF16)
        lo = (r1 - mid.astype(F32)).astype(BF16)
        g_cum = _dot(tri, hi) + _dot(tri, mid) + _dot(tri, lo)
        for h in range(n_heads):
            cols = slice(h * dk, (h + 1) * dk)
            g = g_cum[:, cols]
            g_last = g[cs - 1:cs, :]
            qf = qf_ref[0, rows, cols].astype(F32)
            key = key_ref[0, rows, cols].astype(F32)
            vh = v_ref[0, rows, cols]
            q_dec = (qf * jnp.exp(g)).astype(BF16)
            k_dec = (key * jnp.exp(-g)).astype(BF16)
            k_state = (key * jnp.exp(g_last - g)).astype(BF16)
            a = jnp.where(causal, _dot_nt(q_dec, k_dec), 0.0).astype(BF16)
            st = st_scr[h]
            o = _dot(a, vh) + _dot_nt(q_dec, st.astype(BF16))
            v_t = vh.astype(F32).T.astype(BF16)
            st_scr[h] = st * jnp.exp(g_last) + _dot(v_t, k_state)
            ms = jnp.mean(o * o, axis=-1, keepdims=True)
            on = o * lax.rsqrt(ms + RMS_EPS) * nw_ref[:, cols] * gs_ref[0, rows, cols].astype(F32)
            o_ref[0, rows, cols] = on.astype(BF16)
        return carry

    lax.fori_loop(0, n_chunks, chunk, 0)


def _hgrn_recurrence(qf, key, v, lf, gs, norm_w, ts):
    b, s, d = qf.shape
    h = d // HGRN_EXPAND
    tile = pl.BlockSpec((1, ts, d), lambda i, j: (i, j, 0))
    return pl.pallas_call(
        functools.partial(_hgrn_kernel, n_heads=h),
        grid=(b, s // ts),
        in_specs=[tile, tile, tile, tile, tile, pl.BlockSpec((1, d), lambda i, j: (0, 0))],
        out_specs=tile,
        out_shape=jax.ShapeDtypeStruct((b, s, d), BF16),
        scratch_shapes=[pltpu.VMEM((h, HGRN_EXPAND, HGRN_EXPAND), F32)],
        compiler_params=_params("arbitrary", "arbitrary"),
        name="hgrn_recurrence",
    )(qf, key, v, lf, gs, norm_w)


def _start_row_gather(idx_ref, src_hbm, dst, sem, n_rows):
    def one(r, carry):
        pltpu.make_async_copy(src_hbm.at[pl.ds(idx_ref[r], 1), :], dst.at[pl.ds(r, 1), :], sem).start()
        return carry

    lax.fori_loop(0, n_rows, one, 0)


def _wait_rows(buf, sem):
    pltpu.make_async_copy(buf, buf, sem).wait()


def _moe_ffn_kernel(te_ref, na_ref, src_cur_ref, src_nxt_ref, u_hbm, wg_ref, wu_ref, wd_ref, o_ref,
                    xbuf, xb_scr, acc_scr, sems, *, tm):
    i = pl.program_id(0)
    j = pl.program_id(1)
    n_active = na_ref[0]
    active = i < n_active
    slot = i % 2

    @pl.when(j == 0)
    def _():
        @pl.when(i == 0)
        def _():
            _start_row_gather(src_cur_ref, u_hbm, xbuf.at[0], sems.at[0], tm)

        @pl.when(i + 1 < n_active)
        def _():
            _start_row_gather(src_nxt_ref, u_hbm, xbuf.at[1 - slot], sems.at[1 - slot], tm)

        @pl.when(active)
        def _():
            _wait_rows(xbuf.at[slot], sems.at[slot])
            xb_scr[...] = xbuf[slot].astype(BF16)

    @pl.when(active)
    def _():
        x = xb_scr[...]
        h = (_silu(_dot(x, wg_ref[0])) * _dot(x, wu_ref[0])).astype(BF16)
        y = _dot(h, wd_ref[0])

        @pl.when(j == 0)
        def _():
            acc_scr[...] = y

        @pl.when(j > 0)
        def _():
            acc_scr[...] += y

    @pl.when(j == pl.num_programs(1) - 1)
    def _():
        o_ref[...] = jnp.where(active, acc_scr[...], 0.0)


def _moe_ffn(u, src, wg, wu, wd, tile_expert, n_active, tm, tf):
    n, d = u.shape
    p = src.shape[0]
    n_tiles = p // tm
    f = wg.shape[2]
    nf = f // tf

    def frozen(j, i, na):
        return jnp.where(i < na[0], j, nf - 1)

    grid_spec = pltpu.PrefetchScalarGridSpec(
        num_scalar_prefetch=2,
        grid=(n_tiles, nf),
        in_specs=[pl.BlockSpec((tm,), lambda i, j, te, na: (i,), memory_space=pltpu.SMEM),
                  pl.BlockSpec((tm,), lambda i, j, te, na: (jnp.minimum(i + 1, n_tiles - 1),),
                               memory_space=pltpu.SMEM),
                  pl.BlockSpec(memory_space=pl.ANY),
                  pl.BlockSpec((1, d, tf), lambda i, j, te, na: (te[i], 0, frozen(j, i, na))),
                  pl.BlockSpec((1, d, tf), lambda i, j, te, na: (te[i], 0, frozen(j, i, na))),
                  pl.BlockSpec((1, tf, d), lambda i, j, te, na: (te[i], frozen(j, i, na), 0))],
        out_specs=pl.BlockSpec((tm, d), lambda i, j, te, na: (i, 0)),
        scratch_shapes=[pltpu.VMEM((2, tm, d), F32),
                        pltpu.VMEM((tm, d), BF16),
                        pltpu.VMEM((tm, d), F32),
                        pltpu.SemaphoreType.DMA((2,))],
    )
    return pl.pallas_call(
        functools.partial(_moe_ffn_kernel, tm=tm),
        grid_spec=grid_spec,
        out_shape=jax.ShapeDtypeStruct((p, d), F32),
        compiler_params=_params("arbitrary", "arbitrary"),
        name="moe_ffn",
    )(tile_expert, n_active, src, src, u, wg, wu, wd)


def _combine_ln_kernel(sa_cur, sb_cur, sa_nxt, sb_nxt, ys_hbm, r_ref, x_ref, g1p_ref, lng_ref, lnb_ref,
                       xo_ref, ybuf, sems, *, alpha, tm):
    i = pl.program_id(0)
    slot = i % 2

    def start(sa, sb, s):
        _start_row_gather(sa, ys_hbm, ybuf.at[s, 0], sems.at[s], tm)
        _start_row_gather(sb, ys_hbm, ybuf.at[s, 1], sems.at[s], tm)

    @pl.when(i == 0)
    def _():
        start(sa_cur, sb_cur, 0)

    @pl.when(i + 1 < pl.num_programs(0))
    def _():
        start(sa_nxt, sb_nxt, 1 - slot)

    _wait_rows(ybuf.at[slot], sems.at[slot])
    r = r_ref[...]
    y = r[:, 2:3] * ybuf[slot, 0] + r[:, 3:4] * ybuf[slot, 1]
    xo_ref[...] = _layer_norm(alpha * x_ref[...] + g1p_ref[0] * y, lng_ref[...], lnb_ref[...])


def _combine_ln(ys, slot_a, slot_b, route, x, g1p, lng, lnb, alpha, tm):
    b, s, d = x.shape
    n = b * s
    n_tiles = n // tm
    per_seq = s // tm
    cur = pl.BlockSpec((tm,), lambda i: (i,), memory_space=pltpu.SMEM)
    nxt = pl.BlockSpec((tm,), lambda i: (jnp.minimum(i + 1, n_tiles - 1),), memory_space=pltpu.SMEM)
    tile = pl.BlockSpec((tm, d), lambda i: (i, 0))
    vec = pl.BlockSpec((1, d), lambda i: (0, 0))
    out = pl.pallas_call(
        functools.partial(_combine_ln_kernel, alpha=alpha, tm=tm),
        grid=(n_tiles,),
        in_specs=[cur, cur, nxt, nxt, pl.BlockSpec(memory_space=pl.ANY),
                  pl.BlockSpec((tm, LANES), lambda i: (i, 0)), tile,
                  pl.BlockSpec((1, 1, d), lambda i: (i // per_seq, 0, 0)), vec, vec],
        out_specs=tile,
        out_shape=jax.ShapeDtypeStruct((n, d), F32),
        scratch_shapes=[pltpu.VMEM((2, 2, tm, d), F32), pltpu.SemaphoreType.DMA((2,))],
        compiler_params=_params("arbitrary"),
        name="combine_ln",
    )(slot_a, slot_b, slot_a, slot_b, ys, route.reshape(n, LANES), x.reshape(n, d), g1p, lng, lnb)
    return out.reshape(b, s, d)


def _moe_plan(route, tm):
    n = route.shape[0]
    e = N_EXPERTS
    n_tiles = TOP_K * n // tm + e
    flat_e = route[:, :TOP_K].astype(jnp.int32).reshape(-1)
    onehot = (flat_e[:, None] == jnp.arange(e, dtype=jnp.int32)[None, :]).astype(jnp.int32)
    incl = jnp.cumsum(onehot, axis=0)
    counts = incl[-1]
    padded = (counts + tm - 1) // tm * tm
    ends = jnp.cumsum(padded)
    offs = ends - padded
    slot = jnp.sum(onehot * (offs[None, :] + incl - 1), axis=1)
    src = jnp.zeros((n_tiles * tm,), jnp.int32).at[slot].set(jnp.arange(TOP_K * n, dtype=jnp.int32) // TOP_K)
    n_active = ends[-1] // tm
    tile_id = jnp.minimum(jnp.arange(n_tiles, dtype=jnp.int32), n_active - 1)
    tile_expert = jnp.minimum(jnp.sum((tile_id[:, None] * tm >= ends[None, :]).astype(jnp.int32), axis=1), e - 1)
    return slot.reshape(n, TOP_K), src, tile_expert.astype(jnp.int32), n_active.reshape(1).astype(jnp.int32)


def _moe_swiglu_ln(u, route, wg, wu, wd, x, g1p, lng, lnb, alpha, tm_moe=512, tf=512, tm_ln=512):
    b, s, d = x.shape
    n = b * s
    slot, src, tile_expert, n_active = _moe_plan(route.reshape(n, LANES), tm_moe)
    ys = _moe_ffn(u.reshape(n, d), src, wg, wu, wd, tile_expert, n_active, tm_moe, tf)
    return _combine_ln(ys, slot[:, 0], slot[:, 1], route, x, g1p, lng, lnb, alpha, tm_ln)


def kernel(x, c, positions, ada_w, ada_b, ln_g, ln_b, attn_w_in, attn_w_out, rec_w_in, rec_lb_logits,
           rec_norm_w, rec_w_out, ffn_w_gate, ffn_w_up, ffn_w_down, router_w, moe_w_gate, moe_w_up,
           moe_w_down):
    b, s, d = x.shape
    depth = ada_w.shape[0]
    alpha = (2.0 * depth) ** 0.25
    ts = 512

    mods = _ada_mods(c, ada_w, ada_b).reshape(depth, b, 6, 1, d)
    one_plus = lambda t: 1.0 + t
    pos = positions.reshape(b, s, 1)

    for i in range(depth):
        shift_m, scale_m, gate_m, shift_f, scale_f, gate_f = (mods[i, :, r] for r in range(6))
        j = i // 2
        lng = ln_g[i].reshape(2, 1, d)
        lnb = ln_b[i].reshape(2, 1, d)
        if i % 2 == 0:
            qkv = _qkv_proj(x, one_plus(scale_m), shift_m, pos, attn_w_in[j].astype(BF16), ts)
            o = _moba_attention(qkv, d, n_grp=math.gcd(d // HEAD_DIM, 4),
                                ch=math.gcd(s // MOBA_BLOCK, 4))
            x, u = _proj_ln(o, attn_w_out[j].astype(BF16), x, one_plus(gate_m), lng[0], lnb[0],
                            one_plus(scale_f), shift_f, alpha, ts)
            x = _ffn_dense(u, ffn_w_gate[j].astype(BF16), ffn_w_up[j].astype(BF16),
                           ffn_w_down[j].astype(BF16), x, one_plus(gate_f), lng[1], lnb[1], alpha, ts)
        else:
            qf, key, v, gs, lf = _rec_in(x, one_plus(scale_m), shift_m, rec_w_in[j].astype(BF16),
                                         rec_lb_logits, i, ts)
            o = _hgrn_recurrence(qf, key, v, lf, gs, rec_norm_w[j].reshape(1, d), ts)
            wr = jnp.zeros((d, LANES), F32).at[:, :N_EXPERTS].set(router_w[j])
            x, u, route = _proj_ln(o, rec_w_out[j].astype(BF16), x, one_plus(gate_m), lng[0], lnb[0],
                                   one_plus(scale_f), shift_f, alpha, ts, wr=wr)
            x = _moe_swiglu_ln(u, route, moe_w_gate[j].astype(BF16), moe_w_up[j].astype(BF16),
                               moe_w_down[j].astype(BF16), x, one_plus(gate_f), lng[1], lnb[1], alpha)
    return x
```

```python
import functools
import math

import jax
import jax.numpy as jnp
from jax import lax
from jax.experimental import pallas as pl
from jax.experimental.pallas import tpu as pltpu

HEAD_DIM = 128
ROPE_DIM = HEAD_DIM // 4
ROPE_THETA = 500000.0
MOBA_BLOCK = 256
MOBA_TOPK = 3
HGRN_EXPAND = 128
HGRN_CHUNK = 64
N_EXPERTS = 8
TOP_K = 2
LN_EPS = 1e-5
RMS_EPS = 1e-6

LANES = 128
MXU_N = 256
VMEM_LIMIT = 56 * 1024 * 1024
NEG_BIG = -1e30

F32 = jnp.float32
BF16 = jnp.bfloat16
HIGHEST = lax.Precision.HIGHEST


def _params(*sem):
    return pltpu.CompilerParams(dimension_semantics=sem, vmem_limit_bytes=VMEM_LIMIT)


def _dot(a, b):
    return jnp.dot(a, b, preferred_element_type=F32)


def _dot_nt(a, b, precision=None):
    return lax.dot_general(a, b, (((1,), (1,)), ((), ())), precision=precision,
                           preferred_element_type=F32)


def _silu(x):
    return x * jax.nn.sigmoid(x)


def _layer_norm(z, g, b):
    mu = jnp.mean(z, axis=-1, keepdims=True)
    d = z - mu
    var = jnp.mean(d * d, axis=-1, keepdims=True)
    return d * lax.rsqrt(var + LN_EPS) * g + b


def _ada_kernel(c_ref, w_ref, b_ref, o_ref):
    a = _silu(c_ref[...])
    o_ref[0] = jnp.dot(a, w_ref[0], precision=HIGHEST, preferred_element_type=F32) + b_ref[0]


def _ada_mods(c, ada_w, ada_b):
    depth, d, m = ada_w.shape
    b = c.shape[0]
    tn = m // 4
    return pl.pallas_call(
        _ada_kernel,
        grid=(depth, m // tn),
        in_specs=[pl.BlockSpec((b, d), lambda l, j: (0, 0)),
                  pl.BlockSpec((1, d, tn), lambda l, j: (l, 0, j)),
                  pl.BlockSpec((1, 1, tn), lambda l, j: (l, 0, j))],
        out_specs=pl.BlockSpec((1, b, tn), lambda l, j: (l, 0, j)),
        out_shape=jax.ShapeDtypeStruct((depth, b, m), F32),
        compiler_params=_params("arbitrary", "arbitrary"),
        name="ada_mods",
    )(c, ada_w, ada_b.reshape(depth, 1, m))


def _qkv_kernel(x_ref, sc_ref, sh_ref, pos_ref, w_ref, o_ref, *, n_rot):
    u = (x_ref[0] * sc_ref[0] + sh_ref[0]).astype(BF16)
    half = ROPE_DIM // 2
    lane = lax.broadcasted_iota(jnp.int32, (1, LANES), 1)
    fidx = (lane % half).astype(F32)
    inv = jnp.exp(-math.log(ROPE_THETA) * fidx * (2.0 / ROPE_DIM))
    inv = jnp.where(lane < ROPE_DIM, inv, 0.0)
    ang = pos_ref[0].astype(F32) * inv
    cos_t = jnp.cos(ang)
    sin_t = jnp.sin(ang)
    sin_lo = jnp.where(lane < half, -sin_t, 0.0)
    sin_hi = jnp.where((lane >= half) & (lane < ROPE_DIM), sin_t, 0.0)
    n_cols = w_ref.shape[1]
    for c in range(n_cols // MXU_N):
        y = _dot(u, w_ref[:, c * MXU_N:(c + 1) * MXU_N])
        for s in range(MXU_N // HEAD_DIM):
            col = c * MXU_N + s * HEAD_DIM
            t = y[:, s * HEAD_DIM:(s + 1) * HEAD_DIM]
            if col < n_rot:
                t = (t * cos_t + pltpu.roll(t, HEAD_DIM - half, 1) * sin_lo
                     + pltpu.roll(t, half, 1) * sin_hi)
            if col < n_rot // 2:
                t = t * (HEAD_DIM ** -0.5 * math.log2(math.e))
            o_ref[0, :, col:col + HEAD_DIM] = t.astype(BF16)


def _qkv_proj(x, sc, sh, pos, w, ts):
    b, s, d = x.shape
    n = w.shape[1]
    return pl.pallas_call(
        functools.partial(_qkv_kernel, n_rot=2 * d),
        grid=(b, s // ts),
        in_specs=[pl.BlockSpec((1, ts, d), lambda i, j: (i, j, 0)),
                  pl.BlockSpec((1, 1, d), lambda i, j: (i, 0, 0)),
                  pl.BlockSpec((1, 1, d), lambda i, j: (i, 0, 0)),
                  pl.BlockSpec((1, ts, 1), lambda i, j: (i, j, 0)),
                  pl.BlockSpec((d, n), lambda i, j: (0, 0))],
        out_specs=pl.BlockSpec((1, ts, n), lambda i, j: (i, j, 0)),
        out_shape=jax.ShapeDtypeStruct((b, s, n), BF16),
        compiler_params=_params("arbitrary", "arbitrary"),
        name="qkv_proj",
    )(x, sc, sh, pos, w)


def _moba_kernel(q_ref, k_ref, v_ref, o_ref, kmean_scr, vt_scr, bias_scr, m_scr, l_scr, acc_scr,
                 *, nb, n_grp, ch):
    blk = MOBA_BLOCK
    hd = HEAD_DIM
    cw = ch * blk
    i = pl.program_id(2)

    @pl.when(i == 0)
    def _():
        for hh in range(n_grp):
            for n in range(nb):
                kb = k_ref[0, n * blk:(n + 1) * blk, hh * hd:(hh + 1) * hd].astype(F32)
                kmean_scr[hh, n:n + 1, :] = jnp.mean(kb, axis=0, keepdims=True)
                vb = v_ref[0, n * blk:(n + 1) * blk, hh * hd:(hh + 1) * hd]
                vt_scr[hh, n // ch, :, (n % ch) * blk:(n % ch + 1) * blk] = vb.astype(F32).T.astype(BF16)

    def sweep(hh, c, n_own):
        own = n_own > 0
        n_blk = n_own if own else ch
        q = q_ref[0, :, hh * hd:(hh + 1) * hd]
        k0 = pl.multiple_of(c * cw, cw)
        s = _dot_nt(k_ref[0, pl.ds(k0, n_blk * blk), hh * hd:(hh + 1) * hd], q)
        parts = [s[t * blk:(t + 1) * blk] + bias_scr[hh, pl.ds(c * ch + t, 1), :]
                 for t in range(n_blk - 1 if own else n_blk)]
        if own:
            s_own = s[(n_blk - 1) * blk:]
            kpos = lax.broadcasted_iota(jnp.int32, s_own.shape, 0)
            qpos = lax.broadcasted_iota(jnp.int32, s_own.shape, 1)
            parts.append(jnp.where(kpos <= qpos, s_own, NEG_BIG))
        m_blk = parts[0]
        for t in range(1, n_blk):
            m_blk = jnp.maximum(m_blk, parts[t])
        m_new = jnp.max(m_blk, axis=0, keepdims=True)
        if not own:
            m_new = jnp.maximum(m_new, m_scr[hh])
        ps = [jnp.exp2(part - m_new) for part in parts]
        l_new = ps[0]
        for t in range(1, n_blk):
            l_new = l_new + ps[t]
        l_new = jnp.sum(l_new, axis=0, keepdims=True)
        p_all = jnp.concatenate([p.astype(BF16) for p in ps], axis=0) if n_blk > 1 else ps[0].astype(BF16)
        pv = _dot(vt_scr[hh, c, :, 0:n_blk * blk], p_all)
        if own:
            l_scr[hh] = l_new
            acc_scr[hh] = pv
        else:
            alpha = jnp.exp2(m_scr[hh] - m_new)
            l_scr[hh] = alpha * l_scr[hh] + l_new
            acc_scr[hh] = alpha * acc_scr[hh] + pv
        m_scr[hh] = m_new

    c_own = i // ch
    for hh in range(n_grp):
        q = q_ref[0, :, hh * hd:(hh + 1) * hd]
        gate = _dot_nt(kmean_scr[hh], q.astype(F32), precision=HIGHEST)
        blk_id = lax.broadcasted_iota(jnp.int32, gate.shape, 0)
        past = blk_id < i
        g = jnp.where(past, gate, -jnp.inf)
        rank = jnp.zeros(gate.shape, jnp.int32)
        for m in range(nb):
            gm = g[m:m + 1, :]
            beats = jnp.where(gm > g, 1, jnp.where(gm == g, jnp.where(blk_id > m, 1, 0), 0))
            rank = rank + beats
        bias_scr[hh] = jnp.where(past, jnp.where(rank < MOBA_TOPK, 0.0, NEG_BIG), NEG_BIG)

    for r in range(ch):
        @pl.when(i % ch == r)
        def _():
            for hh in range(n_grp):
                sweep(hh, c_own, r + 1)

    def body(c, carry):
        for hh in range(n_grp):
            sweep(hh, c, 0)
        return carry

    lax.fori_loop(0, c_own, body, 0)
    for hh in range(n_grp):
        o_ref[0, :, hh * hd:(hh + 1) * hd] = (acc_scr[hh] / l_scr[hh]).T.astype(BF16)


def _moba_attention(qkv, d, n_grp, ch):
    b, s, _ = qkv.shape
    h = d // HEAD_DIM
    blk = MOBA_BLOCK
    nb = s // blk
    gw = n_grp * HEAD_DIM
    ng = h // n_grp
    return pl.pallas_call(
        functools.partial(_moba_kernel, nb=nb, n_grp=n_grp, ch=ch),
        grid=(b, ng, nb),
        in_specs=[pl.BlockSpec((1, blk, gw), lambda bi, hi, i: (bi, i, hi)),
                  pl.BlockSpec((1, s, gw), lambda bi, hi, i: (bi, 0, ng + hi)),
                  pl.BlockSpec((1, s, gw), lambda bi, hi, i: (bi, 0, 2 * ng + hi))],
        out_specs=pl.BlockSpec((1, blk, gw), lambda bi, hi, i: (bi, i, hi)),
        out_shape=jax.ShapeDtypeStruct((b, s, d), BF16),
        scratch_shapes=[pltpu.VMEM((n_grp, nb, HEAD_DIM), F32),
                        pltpu.VMEM((n_grp, nb // ch, HEAD_DIM, ch * blk), BF16),
                        pltpu.VMEM((n_grp, nb, blk), F32),
                        pltpu.VMEM((n_grp, 1, blk), F32),
                        pltpu.VMEM((n_grp, 1, blk), F32),
                        pltpu.VMEM((n_grp, HEAD_DIM, blk), F32)],
        compiler_params=_params("arbitrary", "arbitrary", "arbitrary"),
        name="moba_attention",
    )(qkv, qkv, qkv)


def _route_top2(u, wr_ref):
    logits = jnp.dot(u, wr_ref[...], precision=HIGHEST, preferred_element_type=F32)
    lane = lax.broadcasted_iota(jnp.int32, logits.shape, 1)
    logits = jnp.where(lane < N_EXPERTS, logits, -jnp.inf)
    m1 = jnp.max(logits, axis=-1, keepdims=True)
    i1 = jnp.min(jnp.where(logits == m1, lane, LANES), axis=-1, keepdims=True)
    rest = jnp.where(lane == i1, -jnp.inf, logits)
    m2 = jnp.max(rest, axis=-1, keepdims=True)
    i2 = jnp.min(jnp.where(rest == m2, lane, LANES), axis=-1, keepdims=True)
    e2 = jnp.exp(m2 - m1)
    w1 = 1.0 / (1.0 + e2)
    w2 = e2 / (1.0 + e2)
    return jnp.where(lane == 0, i1.astype(F32),
                     jnp.where(lane == 1, i2.astype(F32),
                               jnp.where(lane == 2, w1, jnp.where(lane == 3, w2, 0.0))))


def _proj_ln_kernel(a_ref, w_ref, x_ref, g1p_ref, lng_ref, lnb_ref, sc_ref, sh_ref, *rest,
                    alpha, route):
    if route:
        wr_ref, xo_ref, uo_ref, ro_ref = rest
    else:
        xo_ref, uo_ref = rest
    y = _dot(a_ref[0], w_ref[...])
    xn = _layer_norm(alpha * x_ref[0] + g1p_ref[0] * y, lng_ref[...], lnb_ref[...])
    xo_ref[0] = xn
    u = xn * sc_ref[0] + sh_ref[0]
    uo_ref[0] = u.astype(uo_ref.dtype)
    if route:
        ro_ref[0] = _route_top2(u, wr_ref)


def _proj_ln(a, w, x, g1p, lng, lnb, sc, sh, alpha, tm, wr=None):
    b, s, d = x.shape
    route = wr is not None
    tile = pl.BlockSpec((1, tm, d), lambda i, j: (i, j, 0))
    per_b = pl.BlockSpec((1, 1, d), lambda i, j: (i, 0, 0))
    vec = pl.BlockSpec((1, d), lambda i, j: (0, 0))
    in_specs = [tile, pl.BlockSpec((d, d), lambda i, j: (0, 0)), tile, per_b, vec, vec, per_b, per_b]
    out_specs = [tile, tile]
    out_shape = [jax.ShapeDtypeStruct((b, s, d), F32),
                 jax.ShapeDtypeStruct((b, s, d), F32 if route else BF16)]
    args = [a, w, x, g1p, lng, lnb, sc, sh]
    if route:
        in_specs.append(pl.BlockSpec((d, LANES), lambda i, j: (0, 0)))
        out_specs.append(pl.BlockSpec((1, tm, LANES), lambda i, j: (i, j, 0)))
        out_shape.append(jax.ShapeDtypeStruct((b, s, LANES), F32))
        args.append(wr)
    return pl.pallas_call(
        functools.partial(_proj_ln_kernel, alpha=alpha, route=route),
        grid=(b, s // tm),
        in_specs=in_specs, out_specs=out_specs, out_shape=out_shape,
        compiler_params=_params("arbitrary", "arbitrary"),
        name="proj_ln_route" if route else "proj_ln",
    )(*args)


def _ffn_kernel(u_ref, wg_ref, wu_ref, wd_ref, x_ref, g1p_ref, lng_ref, lnb_ref, xo_ref, h_scr,
                *, alpha, tf):
    u = u_ref[0]
    f = wg_ref.shape[1]
    for j in range(f // tf):
        cols = slice(j * tf, (j + 1) * tf)
        hj = _silu(_dot(u, wg_ref[:, cols])) * _dot(u, wu_ref[:, cols])
        h_scr[:, cols] = hj.astype(BF16)
    y = _dot(h_scr[...], wd_ref[...])
    xo_ref[0] = _layer_norm(alpha * x_ref[0] + g1p_ref[0] * y, lng_ref[...], lnb_ref[...])


def _ffn_dense(u, wg, wu, wd, x, g1p, lng, lnb, alpha, tm):
    b, s, d = x.shape
    f = wg.shape[1]
    tf = MXU_N
    tile = pl.BlockSpec((1, tm, d), lambda i, j: (i, j, 0))
    per_b = pl.BlockSpec((1, 1, d), lambda i, j: (i, 0, 0))
    vec = pl.BlockSpec((1, d), lambda i, j: (0, 0))
    resident = dict(pipeline_mode=pl.Buffered(1))
    return pl.pallas_call(
        functools.partial(_ffn_kernel, alpha=alpha, tf=tf),
        grid=(b, s // tm),
        in_specs=[tile,
                  pl.BlockSpec((d, f), lambda i, j: (0, 0), **resident),
                  pl.BlockSpec((d, f), lambda i, j: (0, 0), **resident),
                  pl.BlockSpec((f, d), lambda i, j: (0, 0), **resident),
                  tile, per_b, vec, vec],
        out_specs=tile,
        out_shape=jax.ShapeDtypeStruct((b, s, d), F32),
        scratch_shapes=[pltpu.VMEM((tm, f), BF16)],
        compiler_params=_params("arbitrary", "arbitrary"),
        name="ffn_dense",
    )(u, wg, wu, wd, x, g1p, lng, lnb)


def _rec_in_kernel(x_ref, sc_ref, sh_ref, w_ref, lbl_ref, qf_ref, key_ref, v_ref, gs_ref, lf_ref,
                   *, layer_idx):
    u = (x_ref[0] * sc_ref[0] + sh_ref[0]).astype(BF16)
    d = x_ref.shape[2]
    lbl = lbl_ref[...]
    e = jnp.exp(lbl - jnp.max(lbl, axis=0, keepdims=True))
    sm = e / jnp.sum(e, axis=0, keepdims=True)
    lb = jnp.zeros((1, d), F32)
    for r in range(1, layer_idx + 1):
        lb = lb + sm[r:r + 1, :]
    for c in range(d // MXU_N):
        cols = slice(c * MXU_N, (c + 1) * MXU_N)
        q = _dot(u, w_ref[:, c * MXU_N:(c + 1) * MXU_N])
        qf_ref[0, :, cols] = _silu(q).astype(BF16)
        f = _dot(u, w_ref[:, d + c * MXU_N:d + (c + 1) * MXU_N])
        lbc = lb[:, cols]
        f_gate = lbc + (1.0 - lbc) * jax.nn.sigmoid(f)
        lf_ref[0, :, cols] = jnp.log(f_gate)
        key_ref[0, :, cols] = (1.0 - f_gate).astype(BF16)
        v = _dot(u, w_ref[:, 2 * d + c * MXU_N:2 * d + (c + 1) * MXU_N])
        v_ref[0, :, cols] = v.astype(BF16)
        g = _dot(u, w_ref[:, 3 * d + c * MXU_N:3 * d + (c + 1) * MXU_N])
        gs_ref[0, :, cols] = _silu(g).astype(BF16)


def _rec_in(x, sc, sh, w, lb_logits, layer_idx, ts):
    b, s, d = x.shape
    depth = lb_logits.shape[0]
    tile = pl.BlockSpec((1, ts, d), lambda i, j: (i, j, 0))
    per_b = pl.BlockSpec((1, 1, d), lambda i, j: (i, 0, 0))
    bf = jax.ShapeDtypeStruct((b, s, d), BF16)
    return pl.pallas_call(
        functools.partial(_rec_in_kernel, layer_idx=layer_idx),
        grid=(b, s // ts),
        in_specs=[tile, per_b, per_b,
                  pl.BlockSpec((d, 4 * d), lambda i, j: (0, 0)),
                  pl.BlockSpec((depth, d), lambda i, j: (0, 0))],
        out_specs=[tile] * 5,
        out_shape=[bf, bf, bf, bf, jax.ShapeDtypeStruct((b, s, d), F32)],
        compiler_params=_params("arbitrary", "arbitrary"),
        name="rec_in",
    )(x, sc, sh, w, lb_logits)


def _hgrn_kernel(qf_ref, key_ref, v_ref, lf_ref, gs_ref, nw_ref, o_ref, st_scr, *, n_heads):
    cs = HGRN_CHUNK
    dk = HGRN_EXPAND

    @pl.when(pl.program_id(1) == 0)
    def _():
        st_scr[...] = jnp.zeros(st_scr.shape, F32)

    row = lax.broadcasted_iota(jnp.int32, (cs, cs), 0)
    col = lax.broadcasted_iota(jnp.int32, (cs, cs), 1)
    causal = row >= col
    tri = jnp.where(causal, 1.0, 0.0).astype(BF16)
    n_chunks = qf_ref.shape[1] // cs

    def chunk(c, carry):
        rows = pl.ds(pl.multiple_of(c * cs, cs), cs)
        lf = lf_ref[0, rows, :]
        hi = lf.astype(BF16)
        r1 = lf - hi.astype(F32)
        mid = r1.astype(BF16)
        lo = (r1 - mid.astype(F32)).astype(BF16)
        g_cum = _dot(tri, hi) + _dot(tri, mid) + _dot(tri, lo)
        for h in range(n_heads):
            cols = slice(h * dk, (h + 1) * dk)
            g = g_cum[:, cols]
            g_last = g[cs - 1:cs, :]
            qf = qf_ref[0, rows, cols].astype(F32)
            key = key_ref[0, rows, cols].astype(F32)
            vh = v_ref[0, rows, cols]
            q_dec = (qf * jnp.exp(g)).astype(BF16)
            k_dec = (key * jnp.exp(-g)).astype(BF16)
            k_state = (key * jnp.exp(g_last - g)).astype(BF16)
            a = jnp.where(causal, _dot_nt(q_dec, k_dec), 0.0).astype(BF16)
            st = st_scr[h]
            o = _dot(a, vh) + _dot_nt(q_dec, st.astype(BF16))
            v_t = vh.astype(F32).T.astype(BF16)
            st_scr[h] = st * jnp.exp(g_last) + _dot(v_t, k_state)
            ms = jnp.mean(o * o, axis=-1, keepdims=True)
            on = o * lax.rsqrt(ms + RMS_EPS) * nw_ref[:, cols] * gs_ref[0, rows, cols].astype(F32)
            o_ref[0, rows, cols] = on.astype(BF16)
        return carry

    lax.fori_loop(0, n_chunks, chunk, 0)


def _hgrn_recurrence(qf, key, v, lf, gs, norm_w, ts):
    b, s, d = qf.shape
    h = d // HGRN_EXPAND
    tile = pl.BlockSpec((1, ts, d), lambda i, j: (i, j, 0))
    return pl.pallas_call(
        functools.partial(_hgrn_kernel, n_heads=h),
        grid=(b, s // ts),
        in_specs=[tile, tile, tile, tile, tile, pl.BlockSpec((1, d), lambda i, j: (0, 0))],
        out_specs=tile,
        out_shape=jax.ShapeDtypeStruct((b, s, d), BF16),
        scratch_shapes=[pltpu.VMEM((h, HGRN_EXPAND, HGRN_EXPAND), F32)],
        compiler_params=_params("arbitrary", "arbitrary"),
        name="hgrn_recurrence",
    )(qf, key, v, lf, gs, norm_w)


GATHER_UNROLL = 8


def _start_row(idx_at, src_hbm, dst, sem, r):
    pltpu.make_async_copy(src_hbm.at[pl.ds(idx_at(r), 1), :], dst.at[pl.ds(r, 1), :], sem).start()


def _start_row_gather(idx_at, src_hbm, dst, sem, n_rows):
    def one(r, carry):
        _start_row(idx_at, src_hbm, dst, sem, r)
        return carry

    lax.fori_loop(0, n_rows, one, 0, unroll=GATHER_UNROLL)


def _wait_rows(buf, sem):
    pltpu.make_async_copy(buf, buf, sem).wait()


def _moe_ffn_kernel(te_ref, na_ref, src_cur_ref, src_nxt_ref, u_hbm, wg_ref, wu_ref, wd_ref, o_ref,
                    xbuf, xb_scr, acc_scr, sems, *, tm, rows_per_step):
    i = pl.program_id(0)
    j = pl.program_id(1)
    n_tiles = pl.num_programs(0)
    nf = pl.num_programs(1)
    n_active = na_ref[0]
    active = i < n_active
    slot = i % 2

    @pl.when((j == 0) & (i == 0))
    def _():
        _start_row_gather(lambda r: src_cur_ref[0, 0, r], u_hbm, xbuf.at[0], sems.at[0], tm)

    @pl.when((j == 0) & (i <= n_active))
    def _():
        _wait_rows(xbuf.at[slot], sems.at[slot])

    @pl.when(active & (j == 0))
    def _():
        xb_scr[...] = xbuf[slot].astype(BF16)

    @pl.when(active)
    def _():
        for k in range(rows_per_step):
            _start_row(lambda r: src_nxt_ref[0, 0, r], u_hbm, xbuf.at[1 - slot], sems.at[1 - slot],
                       j * rows_per_step + k)
        x = xb_scr[...]
        h = (_silu(_dot(x, wg_ref[0])) * _dot(x, wu_ref[0])).astype(BF16)
        y = _dot(h, wd_ref[0])

        @pl.when(j == 0)
        def _():
            acc_scr[...] = y

        @pl.when(j > 0)
        def _():
            acc_scr[...] += y

    @pl.when(j == nf - 1)
    def _():
        o_ref[...] = jnp.where(active, acc_scr[...], 0.0)

    @pl.when(active & (i == n_tiles - 1) & (j == nf - 1))
    def _():
        _wait_rows(xbuf.at[1 - slot], sems.at[1 - slot])


def _moe_ffn(u, src, wg, wu, wd, tile_expert, n_active, tm, tf):
    n, d = u.shape
    p = src.shape[0]
    n_tiles = p // tm
    f = wg.shape[2]
    nf = f // tf

    def frozen(j, i, na):
        return jnp.where(i < na[0], j, nf - 1)

    grid_spec = pltpu.PrefetchScalarGridSpec(
        num_scalar_prefetch=2,
        grid=(n_tiles, nf),
        in_specs=[pl.BlockSpec((1, 1, tm), lambda i, j, te, na: (i, 0, 0), memory_space=pltpu.SMEM),
                  pl.BlockSpec((1, 1, tm), lambda i, j, te, na: (jnp.minimum(i + 1, n_tiles - 1), 0, 0),
                               memory_space=pltpu.SMEM),
                  pl.BlockSpec(memory_space=pl.ANY),
                  pl.BlockSpec((1, d, tf), lambda i, j, te, na: (te[i], 0, frozen(j, i, na))),
                  pl.BlockSpec((1, d, tf), lambda i, j, te, na: (te[i], 0, frozen(j, i, na))),
                  pl.BlockSpec((1, tf, d), lambda i, j, te, na: (te[i], frozen(j, i, na), 0))],
        out_specs=pl.BlockSpec((tm, d), lambda i, j, te, na: (i, 0)),
        scratch_shapes=[pltpu.VMEM((2, tm, d), F32),
                        pltpu.VMEM((tm, d), BF16),
                        pltpu.VMEM((tm, d), F32),
                        pltpu.SemaphoreType.DMA((2,))],
    )
    return pl.pallas_call(
        functools.partial(_moe_ffn_kernel, tm=tm, rows_per_step=tm // nf),
        grid_spec=grid_spec,
        out_shape=jax.ShapeDtypeStruct((p, d), F32),
        compiler_params=_params("arbitrary", "arbitrary"),
        name="moe_ffn",
    )(tile_expert, n_active, src.reshape(n_tiles, 1, tm), src.reshape(n_tiles, 1, tm), u, wg, wu, wd)


def _combine_ln_kernel(sa_cur, sb_cur, sa_nxt, sb_nxt, ys_hbm, r_ref, x_ref, g1p_ref, lng_ref, lnb_ref,
                       xo_ref, ybuf, sems, *, alpha, tm):
    i = pl.program_id(0)
    slot = i % 2

    def start(sa, sb, s):
        _start_row_gather(lambda r: sa[r], ys_hbm, ybuf.at[s, 0], sems.at[s], tm)
        _start_row_gather(lambda r: sb[r], ys_hbm, ybuf.at[s, 1], sems.at[s], tm)

    @pl.when(i == 0)
    def _():
        start(sa_cur, sb_cur, 0)

    @pl.when(i + 1 < pl.num_programs(0))
    def _():
        start(sa_nxt, sb_nxt, 1 - slot)

    _wait_rows(ybuf.at[slot], sems.at[slot])
    r = r_ref[...]
    y = r[:, 2:3] * ybuf[slot, 0] + r[:, 3:4] * ybuf[slot, 1]
    xo_ref[...] = _layer_norm(alpha * x_ref[...] + g1p_ref[0] * y, lng_ref[...], lnb_ref[...])


def _combine_ln(ys, slot_a, slot_b, route, x, g1p, lng, lnb, alpha, tm):
    b, s, d = x.shape
    n = b * s
    n_tiles = n // tm
    per_seq = s // tm
    cur = pl.BlockSpec((tm,), lambda i: (i,), memory_space=pltpu.SMEM)
    nxt = pl.BlockSpec((tm,), lambda i: (jnp.minimum(i + 1, n_tiles - 1),), memory_space=pltpu.SMEM)
    tile = pl.BlockSpec((tm, d), lambda i: (i, 0))
    vec = pl.BlockSpec((1, d), lambda i: (0, 0))
    out = pl.pallas_call(
        functools.partial(_combine_ln_kernel, alpha=alpha, tm=tm),
        grid=(n_tiles,),
        in_specs=[cur, cur, nxt, nxt, pl.BlockSpec(memory_space=pl.ANY),
                  pl.BlockSpec((tm, LANES), lambda i: (i, 0)), tile,
                  pl.BlockSpec((1, 1, d), lambda i: (i // per_seq, 0, 0)), vec, vec],
        out_specs=tile,
        out_shape=jax.ShapeDtypeStruct((n, d), F32),
        scratch_shapes=[pltpu.VMEM((2, 2, tm, d), F32), pltpu.SemaphoreType.DMA((2,))],
        compiler_params=_params("arbitrary"),
        name="combine_ln",
    )(slot_a, slot_b, slot_a, slot_b, ys, route.reshape(n, LANES), x.reshape(n, d), g1p, lng, lnb)
    return out.reshape(b, s, d)


def _moe_plan(route, tm):
    n = route.shape[0]
    e = N_EXPERTS
    n_tiles = (TOP_K * n + e * (tm - 1)) // tm
    flat_e = route[:, :TOP_K].astype(jnp.int32).reshape(-1)
    onehot = (flat_e[:, None] == jnp.arange(e, dtype=jnp.int32)[None, :]).astype(jnp.int32)
    incl = jnp.cumsum(onehot, axis=0)
    counts = incl[-1]
    padded = (counts + tm - 1) // tm * tm
    ends = jnp.cumsum(padded)
    offs = ends - padded
    slot = jnp.sum(onehot * (offs[None, :] + incl - 1), axis=1)
    src = jnp.zeros((n_tiles * tm,), jnp.int32).at[slot].set(jnp.arange(TOP_K * n, dtype=jnp.int32) // TOP_K)
    n_active = ends[-1] // tm
    tile_id = jnp.minimum(jnp.arange(n_tiles, dtype=jnp.int32), n_active - 1)
    tile_expert = jnp.minimum(jnp.sum((tile_id[:, None] * tm >= ends[None, :]).astype(jnp.int32), axis=1), e - 1)
    return slot.reshape(n, TOP_K), src, tile_expert.astype(jnp.int32), n_active.reshape(1).astype(jnp.int32)


def _moe_swiglu_ln(u, route, wg, wu, wd, x, g1p, lng, lnb, alpha, tm_moe=896, tf=512, tm_ln=512):
    b, s, d = x.shape
    n = b * s
    assert tm_moe % (wg.shape[2] // tf) == 0
    slot, src, tile_expert, n_active = _moe_plan(route.reshape(n, LANES), tm_moe)
    ys = _moe_ffn(u.reshape(n, d), src, wg, wu, wd, tile_expert, n_active, tm_moe, tf)
    return _combine_ln(ys, slot[:, 0], slot[:, 1], route, x, g1p, lng, lnb, alpha, tm_ln)


def kernel(x, c, positions, ada_w, ada_b, ln_g, ln_b, attn_w_in, attn_w_out, rec_w_in, rec_lb_logits,
           rec_norm_w, rec_w_out, ffn_w_gate, ffn_w_up, ffn_w_down, router_w, moe_w_gate, moe_w_up,
           moe_w_down):
    b, s, d = x.shape
    depth = ada_w.shape[0]
    alpha = (2.0 * depth) ** 0.25
    ts = 512

    mods = _ada_mods(c, ada_w, ada_b).reshape(depth, b, 6, 1, d)
    one_plus = lambda t: 1.0 + t
    pos = positions.reshape(b, s, 1)

    for i in range(depth):
        shift_m, scale_m, gate_m, shift_f, scale_f, gate_f = (mods[i, :, r] for r in range(6))
        j = i // 2
        lng = ln_g[i].reshape(2, 1, d)
        lnb = ln_b[i].reshape(2, 1, d)
        if i % 2 == 0:
            qkv = _qkv_proj(x, one_plus(scale_m), shift_m, pos, attn_w_in[j].astype(BF16), ts)
            o = _moba_attention(qkv, d, n_grp=math.gcd(d // HEAD_DIM, 4),
                                ch=math.gcd(s // MOBA_BLOCK, 4))
            x, u = _proj_ln(o, attn_w_out[j].astype(BF16), x, one_plus(gate_m), lng[0], lnb[0],
                            one_plus(scale_f), shift_f, alpha, ts)
            x = _ffn_dense(u, ffn_w_gate[j].astype(BF16), ffn_w_up[j].astype(BF16),
                           ffn_w_down[j].astype(BF16), x, one_plus(gate_f), lng[1], lnb[1], alpha, ts)
        else:
            qf, key, v, gs, lf = _rec_in(x, one_plus(scale_m), shift_m, rec_w_in[j].astype(BF16),
                                         rec_lb_logits, i, ts)
            o = _hgrn_recurrence(qf, key, v, lf, gs, rec_norm_w[j].reshape(1, d), ts)
            wr = jnp.zeros((d, LANES), F32).at[:, :N_EXPERTS].set(router_w[j])
            x, u, route = _proj_ln(o, rec_w_out[j].astype(BF16), x, one_plus(gate_m), lng[0], lnb[0],
                                   one_plus(scale_f), shift_f, alpha, ts, wr=wr)
            x = _moe_swiglu_ln(u, route, moe_w_gate[j].astype(BF16), moe_w_up[j].astype(BF16),
                               moe_w_down[j].astype(BF16), x, one_plus(gate_f), lng[1], lnb[1], alpha)
    return x
```

```python
import functools
import math

import jax
import jax.numpy as jnp
from jax import lax
from jax.experimental import pallas as pl
from jax.experimental.pallas import tpu as pltpu

HEAD_DIM = 128
ROPE_DIM = HEAD_DIM // 4
ROPE_THETA = 500000.0
MOBA_BLOCK = 256
MOBA_TOPK = 3
HGRN_EXPAND = 128
HGRN_CHUNK = 64
N_EXPERTS = 8
TOP_K = 2
LN_EPS = 1e-5
RMS_EPS = 1e-6

LANES = 128
MXU_N = 256
VMEM_LIMIT = 56 * 1024 * 1024
NEG_BIG = -1e30

F32 = jnp.float32
BF16 = jnp.bfloat16
HIGHEST = lax.Precision.HIGHEST


def _params(*sem):
    return pltpu.CompilerParams(dimension_semantics=sem, vmem_limit_bytes=VMEM_LIMIT)


def _dot(a, b):
    return jnp.dot(a, b, preferred_element_type=F32)


def _dot_nt(a, b, precision=None):
    return lax.dot_general(a, b, (((1,), (1,)), ((), ())), precision=precision,
                           preferred_element_type=F32)


def _silu(x):
    return x * jax.nn.sigmoid(x)


def _layer_norm(z, g, b):
    mu = jnp.mean(z, axis=-1, keepdims=True)
    d = z - mu
    var = jnp.mean(d * d, axis=-1, keepdims=True)
    return d * lax.rsqrt(var + LN_EPS) * g + b


def _ada_kernel(c_ref, w_ref, b_ref, o_ref):
    a = _silu(c_ref[...])
    o_ref[0] = jnp.dot(a, w_ref[0], precision=HIGHEST, preferred_element_type=F32) + b_ref[0]


def _ada_mods(c, ada_w, ada_b):
    depth, d, m = ada_w.shape
    b = c.shape[0]
    tn = m // 4
    return pl.pallas_call(
        _ada_kernel,
        grid=(depth, m // tn),
        in_specs=[pl.BlockSpec((b, d), lambda l, j: (0, 0)),
                  pl.BlockSpec((1, d, tn), lambda l, j: (l, 0, j)),
                  pl.BlockSpec((1, 1, tn), lambda l, j: (l, 0, j))],
        out_specs=pl.BlockSpec((1, b, tn), lambda l, j: (l, 0, j)),
        out_shape=jax.ShapeDtypeStruct((depth, b, m), F32),
        compiler_params=_params("arbitrary", "arbitrary"),
        name="ada_mods",
    )(c, ada_w, ada_b.reshape(depth, 1, m))


def _qkv_kernel(x_ref, sc_ref, sh_ref, pos_ref, w_ref, o_ref, *, n_rot):
    u = (x_ref[0] * sc_ref[0] + sh_ref[0]).astype(BF16)
    half = ROPE_DIM // 2
    lane = lax.broadcasted_iota(jnp.int32, (1, LANES), 1)
    fidx = (lane % half).astype(F32)
    inv = jnp.exp(-math.log(ROPE_THETA) * fidx * (2.0 / ROPE_DIM))
    inv = jnp.where(lane < ROPE_DIM, inv, 0.0)
    ang = pos_ref[0].astype(F32) * inv
    cos_t = jnp.cos(ang)
    sin_t = jnp.sin(ang)
    sin_lo = jnp.where(lane < half, -sin_t, 0.0)
    sin_hi = jnp.where((lane >= half) & (lane < ROPE_DIM), sin_t, 0.0)
    n_cols = w_ref.shape[1]
    for c in range(n_cols // MXU_N):
        y = _dot(u, w_ref[:, c * MXU_N:(c + 1) * MXU_N])
        for s in range(MXU_N // HEAD_DIM):
            col = c * MXU_N + s * HEAD_DIM
            t = y[:, s * HEAD_DIM:(s + 1) * HEAD_DIM]
            if col < n_rot:
                t = (t * cos_t + pltpu.roll(t, HEAD_DIM - half, 1) * sin_lo
                     + pltpu.roll(t, half, 1) * sin_hi)
            if col < n_rot // 2:
                t = t * (HEAD_DIM ** -0.5 * math.log2(math.e))
            o_ref[0, :, col:col + HEAD_DIM] = t.astype(BF16)


def _qkv_proj(x, sc, sh, pos, w, ts):
    b, s, d = x.shape
    n = w.shape[1]
    return pl.pallas_call(
        functools.partial(_qkv_kernel, n_rot=2 * d),
        grid=(b, s // ts),
        in_specs=[pl.BlockSpec((1, ts, d), lambda i, j: (i, j, 0)),
                  pl.BlockSpec((1, 1, d), lambda i, j: (i, 0, 0)),
                  pl.BlockSpec((1, 1, d), lambda i, j: (i, 0, 0)),
                  pl.BlockSpec((1, ts, 1), lambda i, j: (i, j, 0)),
                  pl.BlockSpec((d, n), lambda i, j: (0, 0))],
        out_specs=pl.BlockSpec((1, ts, n), lambda i, j: (i, j, 0)),
        out_shape=jax.ShapeDtypeStruct((b, s, n), BF16),
        compiler_params=_params("arbitrary", "arbitrary"),
        name="qkv_proj",
    )(x, sc, sh, pos, w)


def _moba_kernel(q_ref, k_ref, v_ref, o_ref, kmean_scr, vt_scr, bias_scr, m_scr, l_scr, acc_scr,
                 *, nb, n_grp, ch):
    blk = MOBA_BLOCK
    hd = HEAD_DIM
    cw = ch * blk
    i = pl.program_id(2)

    @pl.when(i == 0)
    def _():
        for hh in range(n_grp):
            for n in range(nb):
                kb = k_ref[0, n * blk:(n + 1) * blk, hh * hd:(hh + 1) * hd].astype(F32)
                kmean_scr[hh, n:n + 1, :] = jnp.mean(kb, axis=0, keepdims=True)
                vb = v_ref[0, n * blk:(n + 1) * blk, hh * hd:(hh + 1) * hd]
                vt_scr[hh, n // ch, :, (n % ch) * blk:(n % ch + 1) * blk] = vb.astype(F32).T.astype(BF16)

    def sweep(c, n_own):
        own = n_own > 0
        n_blk = n_own if own else ch
        k0 = pl.multiple_of(c * cw, cw)
        heads = [slice(hh * hd, (hh + 1) * hd) for hh in range(n_grp)]
        scores = [_dot_nt(k_ref[0, pl.ds(k0, n_blk * blk), cols], q_ref[0, :, cols])
                  for cols in heads]
        m_news, l_news, probs = [], [], []
        for hh, s in enumerate(scores):
            parts = [s[t * blk:(t + 1) * blk] + bias_scr[hh, pl.ds(c * ch + t, 1), :]
                     for t in range(n_blk - 1 if own else n_blk)]
            if own:
                s_own = s[(n_blk - 1) * blk:]
                kpos = lax.broadcasted_iota(jnp.int32, s_own.shape, 0)
                qpos = lax.broadcasted_iota(jnp.int32, s_own.shape, 1)
                parts.append(jnp.where(kpos <= qpos, s_own, NEG_BIG))
            m_blk = parts[0]
            for t in range(1, n_blk):
                m_blk = jnp.maximum(m_blk, parts[t])
            m_new = jnp.max(m_blk, axis=0, keepdims=True)
            if not own:
                m_new = jnp.maximum(m_new, m_scr[hh])
            ps = [jnp.exp2(part - m_new) for part in parts]
            l_new = ps[0]
            for t in range(1, n_blk):
                l_new = l_new + ps[t]
            m_news.append(m_new)
            l_news.append(jnp.sum(l_new, axis=0, keepdims=True))
            probs.append(jnp.concatenate([p.astype(BF16) for p in ps], axis=0) if n_blk > 1
                         else ps[0].astype(BF16))
        pvs = [_dot(vt_scr[hh, c, :, 0:n_blk * blk], probs[hh]) for hh in range(n_grp)]
        for hh in range(n_grp):
            if own:
                l_scr[hh] = l_news[hh]
                acc_scr[hh] = pvs[hh]
            else:
                alpha = jnp.exp2(m_scr[hh] - m_news[hh])
                l_scr[hh] = alpha * l_scr[hh] + l_news[hh]
                acc_scr[hh] = alpha * acc_scr[hh] + pvs[hh]
            m_scr[hh] = m_news[hh]

    c_own = i // ch
    for hh in range(n_grp):
        q = q_ref[0, :, hh * hd:(hh + 1) * hd]
        gate = _dot_nt(kmean_scr[hh], q.astype(F32), precision=HIGHEST)
        blk_id = lax.broadcasted_iota(jnp.int32, gate.shape, 0)
        past = blk_id < i
        g = jnp.where(past, gate, -jnp.inf)
        rank = jnp.zeros(gate.shape, jnp.int32)
        for m in range(nb):
            gm = g[m:m + 1, :]
            beats = jnp.where(gm > g, 1, jnp.where(gm == g, jnp.where(blk_id > m, 1, 0), 0))
            rank = rank + beats
        bias_scr[hh] = jnp.where(past, jnp.where(rank < MOBA_TOPK, 0.0, NEG_BIG), NEG_BIG)

    for r in range(ch):
        @pl.when(i % ch == r)
        def _():
            sweep(c_own, r + 1)

    def body(c, carry):
        sweep(c, 0)
        return carry

    lax.fori_loop(0, c_own, body, 0)
    for hh in range(n_grp):
        o_ref[0, :, hh * hd:(hh + 1) * hd] = (acc_scr[hh] / l_scr[hh]).T.astype(BF16)


def _moba_attention(qkv, d, n_grp, ch):
    b, s, _ = qkv.shape
    h = d // HEAD_DIM
    blk = MOBA_BLOCK
    nb = s // blk
    gw = n_grp * HEAD_DIM
    ng = h // n_grp
    return pl.pallas_call(
        functools.partial(_moba_kernel, nb=nb, n_grp=n_grp, ch=ch),
        grid=(b, ng, nb),
        in_specs=[pl.BlockSpec((1, blk, gw), lambda bi, hi, i: (bi, i, hi)),
                  pl.BlockSpec((1, s, gw), lambda bi, hi, i: (bi, 0, ng + hi)),
                  pl.BlockSpec((1, s, gw), lambda bi, hi, i: (bi, 0, 2 * ng + hi))],
        out_specs=pl.BlockSpec((1, blk, gw), lambda bi, hi, i: (bi, i, hi)),
        out_shape=jax.ShapeDtypeStruct((b, s, d), BF16),
        scratch_shapes=[pltpu.VMEM((n_grp, nb, HEAD_DIM), F32),
                        pltpu.VMEM((n_grp, nb // ch, HEAD_DIM, ch * blk), BF16),
                        pltpu.VMEM((n_grp, nb, blk), F32),
                        pltpu.VMEM((n_grp, 1, blk), F32),
                        pltpu.VMEM((n_grp, 1, blk), F32),
                        pltpu.VMEM((n_grp, HEAD_DIM, blk), F32)],
        compiler_params=_params("arbitrary", "arbitrary", "arbitrary"),
        name="moba_attention",
    )(qkv, qkv, qkv)


def _route_top2(u, wr_ref):
    u_hi = u.astype(BF16)
    u_lo = (u - u_hi.astype(F32)).astype(BF16)
    both = _dot(u_hi, wr_ref[...])
    logits = both[:, :LANES] + both[:, LANES:] + _dot(u_lo, wr_ref[:, :LANES])
    lane = lax.broadcasted_iota(jnp.int32, logits.shape, 1)
    logits = jnp.where(lane < N_EXPERTS, logits, -jnp.inf)
    m1 = jnp.max(logits, axis=-1, keepdims=True)
    i1 = jnp.min(jnp.where(logits == m1, lane, LANES), axis=-1, keepdims=True)
    rest = jnp.where(lane == i1, -jnp.inf, logits)
    m2 = jnp.max(rest, axis=-1, keepdims=True)
    i2 = jnp.min(jnp.where(rest == m2, lane, LANES), axis=-1, keepdims=True)
    e2 = jnp.exp(m2 - m1)
    w1 = 1.0 / (1.0 + e2)
    w2 = e2 / (1.0 + e2)
    return jnp.where(lane == 0, i1.astype(F32),
                     jnp.where(lane == 1, i2.astype(F32),
                               jnp.where(lane == 2, w1, jnp.where(lane == 3, w2, 0.0))))


def _proj_ln_kernel(a_ref, w_ref, x_ref, g1p_ref, lng_ref, lnb_ref, sc_ref, sh_ref, *rest,
                    alpha, route):
    if route:
        wr_ref, xo_ref, uo_ref, ro_ref = rest
    else:
        xo_ref, uo_ref = rest
    y = _dot(a_ref[0], w_ref[...])
    xn = _layer_norm(alpha * x_ref[0] + g1p_ref[0] * y, lng_ref[...], lnb_ref[...])
    xo_ref[0] = xn
    u = xn * sc_ref[0] + sh_ref[0]
    uo_ref[0] = u.astype(uo_ref.dtype)
    if route:
        ro_ref[0] = _route_top2(u, wr_ref)


def _proj_ln(a, w, x, g1p, lng, lnb, sc, sh, alpha, tm, wr=None):
    b, s, d = x.shape
    route = wr is not None
    tile = pl.BlockSpec((1, tm, d), lambda i, j: (i, j, 0))
    per_b = pl.BlockSpec((1, 1, d), lambda i, j: (i, 0, 0))
    vec = pl.BlockSpec((1, d), lambda i, j: (0, 0))
    in_specs = [tile, pl.BlockSpec((d, d), lambda i, j: (0, 0)), tile, per_b, vec, vec, per_b, per_b]
    out_specs = [tile, tile]
    out_shape = [jax.ShapeDtypeStruct((b, s, d), F32),
                 jax.ShapeDtypeStruct((b, s, d), F32 if route else BF16)]
    args = [a, w, x, g1p, lng, lnb, sc, sh]
    if route:
        in_specs.append(pl.BlockSpec((d, 2 * LANES), lambda i, j: (0, 0)))
        out_specs.append(pl.BlockSpec((1, tm, LANES), lambda i, j: (i, j, 0)))
        out_shape.append(jax.ShapeDtypeStruct((b, s, LANES), F32))
        args.append(wr)
    return pl.pallas_call(
        functools.partial(_proj_ln_kernel, alpha=alpha, route=route),
        grid=(b, s // tm),
        in_specs=in_specs, out_specs=out_specs, out_shape=out_shape,
        compiler_params=_params("arbitrary", "arbitrary"),
        name="proj_ln_route" if route else "proj_ln",
    )(*args)


def _ffn_kernel(u_ref, wg_ref, wu_ref, wd_ref, x_ref, g1p_ref, lng_ref, lnb_ref, xo_ref, h_scr,
                *, alpha, tf):
    u = u_ref[0]
    f = wg_ref.shape[1]
    for j in range(f // tf):
        cols = slice(j * tf, (j + 1) * tf)
        hj = _silu(_dot(u, wg_ref[:, cols])) * _dot(u, wu_ref[:, cols])
        h_scr[:, cols] = hj.astype(BF16)
    y = _dot(h_scr[...], wd_ref[...])
    xo_ref[0] = _layer_norm(alpha * x_ref[0] + g1p_ref[0] * y, lng_ref[...], lnb_ref[...])


def _ffn_dense(u, wg, wu, wd, x, g1p, lng, lnb, alpha, tm):
    b, s, d = x.shape
    f = wg.shape[1]
    tf = MXU_N
    tile = pl.BlockSpec((1, tm, d), lambda i, j: (i, j, 0))
    per_b = pl.BlockSpec((1, 1, d), lambda i, j: (i, 0, 0))
    vec = pl.BlockSpec((1, d), lambda i, j: (0, 0))
    resident = dict(pipeline_mode=pl.Buffered(1))
    return pl.pallas_call(
        functools.partial(_ffn_kernel, alpha=alpha, tf=tf),
        grid=(b, s // tm),
        in_specs=[tile,
                  pl.BlockSpec((d, f), lambda i, j: (0, 0), **resident),
                  pl.BlockSpec((d, f), lambda i, j: (0, 0), **resident),
                  pl.BlockSpec((f, d), lambda i, j: (0, 0), **resident),
                  tile, per_b, vec, vec],
        out_specs=tile,
        out_shape=jax.ShapeDtypeStruct((b, s, d), F32),
        scratch_shapes=[pltpu.VMEM((tm, f), BF16)],
        compiler_params=_params("arbitrary", "arbitrary"),
        name="ffn_dense",
    )(u, wg, wu, wd, x, g1p, lng, lnb)


def _rec_in_kernel(x_ref, sc_ref, sh_ref, w_ref, lbl_ref, qf_ref, key_ref, v_ref, gs_ref, lf_ref,
                   *, layer_idx):
    u = (x_ref[0] * sc_ref[0] + sh_ref[0]).astype(BF16)
    d = x_ref.shape[2]
    lbl = lbl_ref[...]
    e = jnp.exp(lbl - jnp.max(lbl, axis=0, keepdims=True))
    sm = e / jnp.sum(e, axis=0, keepdims=True)
    lb = jnp.zeros((1, d), F32)
    for r in range(1, layer_idx + 1):
        lb = lb + sm[r:r + 1, :]
    for c in range(d // MXU_N):
        cols = slice(c * MXU_N, (c + 1) * MXU_N)
        q = _dot(u, w_ref[:, c * MXU_N:(c + 1) * MXU_N])
        qf_ref[0, :, cols] = _silu(q).astype(BF16)
        f = _dot(u, w_ref[:, d + c * MXU_N:d + (c + 1) * MXU_N])
        lbc = lb[:, cols]
        f_gate = lbc + (1.0 - lbc) * jax.nn.sigmoid(f)
        lf_ref[0, :, cols] = jnp.log(f_gate)
        key_ref[0, :, cols] = (1.0 - f_gate).astype(BF16)
        v = _dot(u, w_ref[:, 2 * d + c * MXU_N:2 * d + (c + 1) * MXU_N])
        v_ref[0, :, cols] = v.astype(BF16)
        g = _dot(u, w_ref[:, 3 * d + c * MXU_N:3 * d + (c + 1) * MXU_N])
        gs_ref[0, :, cols] = _silu(g).astype(BF16)


def _rec_in(x, sc, sh, w, lb_logits, layer_idx, ts):
    b, s, d = x.shape
    depth = lb_logits.shape[0]
    tile = pl.BlockSpec((1, ts, d), lambda i, j: (i, j, 0))
    per_b = pl.BlockSpec((1, 1, d), lambda i, j: (i, 0, 0))
    bf = jax.ShapeDtypeStruct((b, s, d), BF16)
    return pl.pallas_call(
        functools.partial(_rec_in_kernel, layer_idx=layer_idx),
        grid=(b, s // ts),
        in_specs=[tile, per_b, per_b,
                  pl.BlockSpec((d, 4 * d), lambda i, j: (0, 0)),
                  pl.BlockSpec((depth, d), lambda i, j: (0, 0))],
        out_specs=[tile] * 5,
        out_shape=[bf, bf, bf, bf, jax.ShapeDtypeStruct((b, s, d), F32)],
        compiler_params=_params("arbitrary", "arbitrary"),
        name="rec_in",
    )(x, sc, sh, w, lb_logits)


def _hgrn_kernel(qf_ref, key_ref, v_ref, lf_ref, gs_ref, nw_ref, o_ref, st_scr, *, n_heads):
    cs = HGRN_CHUNK
    dk = HGRN_EXPAND

    @pl.when(pl.program_id(1) == 0)
    def _():
        st_scr[...] = jnp.zeros(st_scr.shape, F32)

    row = lax.broadcasted_iota(jnp.int32, (cs, cs), 0)
    col = lax.broadcasted_iota(jnp.int32, (cs, cs), 1)
    causal = row >= col
    tri = jnp.where(causal, 1.0, 0.0).astype(BF16)
    n_chunks = qf_ref.shape[1] // cs

    def chunk(c, carry):
        rows = pl.ds(pl.multiple_of(c * cs, cs), cs)
        lf = lf_ref[0, rows, :]
        hi = lf.astype(BF16)
        r1 = lf - hi.astype(F32)
        mid = r1.astype(BF16)
        lo = (r1 - mid.astype(F32)).astype(BF16)
        g = _dot(tri, hi) + _dot(tri, mid) + _dot(tri, lo)
        g_last = g[cs - 1:cs, :]
        qf = qf_ref[0, rows, :].astype(F32)
        key = key_ref[0, rows, :].astype(F32)
        v = v_ref[0, rows, :]
        q_dec = (qf * jnp.exp(g)).astype(BF16)
        k_dec = (key * jnp.exp(-g)).astype(BF16)
        k_state = (key * jnp.exp(g_last - g)).astype(BF16)
        decay = jnp.exp(g_last)
        heads = [slice(h * dk, (h + 1) * dk) for h in range(n_heads)]
        a = [jnp.where(causal, _dot_nt(q_dec[:, c], k_dec[:, c]), 0.0).astype(BF16) for c in heads]
        st = [st_scr[h] for h in range(n_heads)]
        o = [_dot(a[h], v[:, c]) + _dot_nt(q_dec[:, c], st[h].astype(BF16))
             for h, c in enumerate(heads)]
        for h, c in enumerate(heads):
            v_t = v[:, c].astype(F32).T.astype(BF16)
            st_scr[h] = st[h] * decay[:, c] + _dot(v_t, k_state[:, c])
        on = jnp.concatenate(
            [oh * lax.rsqrt(jnp.mean(oh * oh, axis=-1, keepdims=True) + RMS_EPS) for oh in o], axis=1)
        o_ref[0, rows, :] = (on * nw_ref[...] * gs_ref[0, rows, :].astype(F32)).astype(BF16)
        return carry

    lax.fori_loop(0, n_chunks, chunk, 0)


def _hgrn_recurrence(qf, key, v, lf, gs, norm_w, ts):
    b, s, d = qf.shape
    h = d // HGRN_EXPAND
    tile = pl.BlockSpec((1, ts, d), lambda i, j: (i, j, 0))
    return pl.pallas_call(
        functools.partial(_hgrn_kernel, n_heads=h),
        grid=(b, s // ts),
        in_specs=[tile, tile, tile, tile, tile, pl.BlockSpec((1, d), lambda i, j: (0, 0))],
        out_specs=tile,
        out_shape=jax.ShapeDtypeStruct((b, s, d), BF16),
        scratch_shapes=[pltpu.VMEM((h, HGRN_EXPAND, HGRN_EXPAND), F32)],
        compiler_params=_params("arbitrary", "arbitrary"),
        name="hgrn_recurrence",
    )(qf, key, v, lf, gs, norm_w)


GATHER_UNROLL = 8


def _start_row(idx_at, src_hbm, dst, sem, r):
    pltpu.make_async_copy(src_hbm.at[pl.ds(idx_at(r), 1), :], dst.at[pl.ds(r, 1), :], sem).start()


def _start_row_gather(idx_at, src_hbm, dst, sem, n_rows):
    def one(r, carry):
        _start_row(idx_at, src_hbm, dst, sem, r)
        return carry

    lax.fori_loop(0, n_rows, one, 0, unroll=GATHER_UNROLL)


def _wait_rows(buf, sem):
    pltpu.make_async_copy(buf, buf, sem).wait()


def _moe_ffn_kernel(te_ref, na_ref, src_cur_ref, src_nxt_ref, u_hbm, wg_ref, wu_ref, wd_ref, o_ref,
                    xbuf, xb_scr, h_scr, acc_scr, sems, *, tm, nf):
    i = pl.program_id(0)
    j = pl.program_id(1)
    n_tiles = pl.num_programs(0)
    rows_per_step = tm // nf
    n_active = na_ref[0]
    active = i < n_active
    slot = i % 2

    @pl.when((j == 0) & (i == 0))
    def _():
        _start_row_gather(lambda r: src_cur_ref[0, 0, r], u_hbm, xbuf.at[0], sems.at[0], tm)

    @pl.when((j == 0) & (i <= n_active))
    def _():
        _wait_rows(xbuf.at[slot], sems.at[slot])

    @pl.when(active & (j == 0))
    def _():
        xb_scr[...] = xbuf[slot].astype(BF16)

    @pl.when(active)
    def _():
        for k in range(rows_per_step):
            _start_row(lambda r: src_nxt_ref[0, 0, r], u_hbm, xbuf.at[1 - slot], sems.at[1 - slot],
                       j * rows_per_step + k)
        x = xb_scr[...]
        for c in range(wg_ref.shape[2] // MXU_N):
            cols = slice(c * MXU_N, (c + 1) * MXU_N)
            hc = _silu(_dot(x, wg_ref[0, :, cols])) * _dot(x, wu_ref[0, :, cols])
            h_scr[:, cols] = hc.astype(BF16)
        y = _dot(h_scr[...], wd_ref[0])
        if nf == 1:
            o_ref[...] = y
        else:
            @pl.when(j == 0)
            def _():
                acc_scr[...] = y

            @pl.when((j > 0) & (j < nf - 1))
            def _():
                acc_scr[...] += y

            @pl.when(j == nf - 1)
            def _():
                o_ref[...] = acc_scr[...] + y

    @pl.when(jnp.logical_not(active) & (j == nf - 1))
    def _():
        o_ref[...] = jnp.zeros(o_ref.shape, F32)

    @pl.when(active & (i == n_tiles - 1) & (j == nf - 1))
    def _():
        _wait_rows(xbuf.at[1 - slot], sems.at[1 - slot])


def _moe_ffn(u, src, wg, wu, wd, tile_expert, n_active, tm, tf):
    n, d = u.shape
    p = src.shape[0]
    n_tiles = p // tm
    f = wg.shape[2]
    nf = f // tf

    def frozen(j, i, na):
        return jnp.where(i < na[0], j, nf - 1)

    grid_spec = pltpu.PrefetchScalarGridSpec(
        num_scalar_prefetch=2,
        grid=(n_tiles, nf),
        in_specs=[pl.BlockSpec((1, 1, tm), lambda i, j, te, na: (i, 0, 0), memory_space=pltpu.SMEM),
                  pl.BlockSpec((1, 1, tm), lambda i, j, te, na: (jnp.minimum(i + 1, n_tiles - 1), 0, 0),
                               memory_space=pltpu.SMEM),
                  pl.BlockSpec(memory_space=pl.ANY),
                  pl.BlockSpec((1, d, tf), lambda i, j, te, na: (te[i], 0, frozen(j, i, na))),
                  pl.BlockSpec((1, d, tf), lambda i, j, te, na: (te[i], 0, frozen(j, i, na))),
                  pl.BlockSpec((1, tf, d), lambda i, j, te, na: (te[i], frozen(j, i, na), 0))],
        out_specs=pl.BlockSpec((tm, d), lambda i, j, te, na: (i, 0)),
        scratch_shapes=[pltpu.VMEM((2, tm, d), F32),
                        pltpu.VMEM((tm, d), BF16),
                        pltpu.VMEM((tm, tf), BF16),
                        pltpu.VMEM((tm, d), F32),
                        pltpu.SemaphoreType.DMA((2,))],
    )
    return pl.pallas_call(
        functools.partial(_moe_ffn_kernel, tm=tm, nf=nf),
        grid_spec=grid_spec,
        out_shape=jax.ShapeDtypeStruct((p, d), F32),
        compiler_params=_params("arbitrary", "arbitrary"),
        name="moe_ffn",
    )(tile_expert, n_active, src.reshape(n_tiles, 1, tm), src.reshape(n_tiles, 1, tm), u, wg, wu, wd)


def _combine_ln_kernel(sa_cur, sb_cur, sa_nxt, sb_nxt, ys_hbm, r_ref, x_ref, g1p_ref, lng_ref, lnb_ref,
                       xo_ref, ybuf, sems, *, alpha, tm):
    i = pl.program_id(0)
    slot = i % 2

    def start(sa, sb, s):
        _start_row_gather(lambda r: sa[r], ys_hbm, ybuf.at[s, 0], sems.at[s], tm)
        _start_row_gather(lambda r: sb[r], ys_hbm, ybuf.at[s, 1], sems.at[s], tm)

    @pl.when(i == 0)
    def _():
        start(sa_cur, sb_cur, 0)

    @pl.when(i + 1 < pl.num_programs(0))
    def _():
        start(sa_nxt, sb_nxt, 1 - slot)

    _wait_rows(ybuf.at[slot], sems.at[slot])
    r = r_ref[...]
    y = r[:, 2:3] * ybuf[slot, 0] + r[:, 3:4] * ybuf[slot, 1]
    xo_ref[...] = _layer_norm(alpha * x_ref[...] + g1p_ref[0] * y, lng_ref[...], lnb_ref[...])


def _combine_ln(ys, slot_a, slot_b, route, x, g1p, lng, lnb, alpha, tm):
    b, s, d = x.shape
    n = b * s
    n_tiles = n // tm
    per_seq = s // tm
    cur = pl.BlockSpec((tm,), lambda i: (i,), memory_space=pltpu.SMEM)
    nxt = pl.BlockSpec((tm,), lambda i: (jnp.minimum(i + 1, n_tiles - 1),), memory_space=pltpu.SMEM)
    tile = pl.BlockSpec((tm, d), lambda i: (i, 0))
    vec = pl.BlockSpec((1, d), lambda i: (0, 0))
    out = pl.pallas_call(
        functools.partial(_combine_ln_kernel, alpha=alpha, tm=tm),
        grid=(n_tiles,),
        in_specs=[cur, cur, nxt, nxt, pl.BlockSpec(memory_space=pl.ANY),
                  pl.BlockSpec((tm, LANES), lambda i: (i, 0)), tile,
                  pl.BlockSpec((1, 1, d), lambda i: (i // per_seq, 0, 0)), vec, vec],
        out_specs=tile,
        out_shape=jax.ShapeDtypeStruct((n, d), F32),
        scratch_shapes=[pltpu.VMEM((2, 2, tm, d), F32), pltpu.SemaphoreType.DMA((2,))],
        compiler_params=_params("arbitrary"),
        name="combine_ln",
    )(slot_a, slot_b, slot_a, slot_b, ys, route.reshape(n, LANES), x.reshape(n, d), g1p, lng, lnb)
    return out.reshape(b, s, d)


def _moe_plan(route, tm):
    n = route.shape[0]
    e = N_EXPERTS
    n_tiles = (TOP_K * n + e * (tm - 1)) // tm
    flat_e = route[:, :TOP_K].astype(jnp.int32).reshape(-1)
    onehot = (flat_e[:, None] == jnp.arange(e, dtype=jnp.int32)[None, :]).astype(jnp.int32)
    incl = jnp.cumsum(onehot, axis=0)
    counts = incl[-1]
    padded = (counts + tm - 1) // tm * tm
    ends = jnp.cumsum(padded)
    offs = ends - padded
    slot = jnp.sum(onehot * (offs[None, :] + incl - 1), axis=1)
    src = jnp.zeros((n_tiles * tm,), jnp.int32).at[slot].set(jnp.arange(TOP_K * n, dtype=jnp.int32) // TOP_K)
    n_active = ends[-1] // tm
    tile_id = jnp.minimum(jnp.arange(n_tiles, dtype=jnp.int32), n_active - 1)
    tile_expert = jnp.minimum(jnp.sum((tile_id[:, None] * tm >= ends[None, :]).astype(jnp.int32), axis=1), e - 1)
    return slot.reshape(n, TOP_K), src, tile_expert.astype(jnp.int32), n_active.reshape(1).astype(jnp.int32)


def _moe_swiglu_ln(u, route, wg, wu, wd, x, g1p, lng, lnb, alpha, tm_moe=896, tm_ln=512):
    b, s, d = x.shape
    n = b * s
    f = wg.shape[2]
    nf = 2 if f % (2 * MXU_N) == 0 else 1
    tf = f // nf
    assert tm_moe % nf == 0
    slot, src, tile_expert, n_active = _moe_plan(route.reshape(n, LANES), tm_moe)
    ys = _moe_ffn(u.reshape(n, d), src, wg, wu, wd, tile_expert, n_active, tm_moe, tf)
    return _combine_ln(ys, slot[:, 0], slot[:, 1], route, x, g1p, lng, lnb, alpha, tm_ln)


def kernel(x, c, positions, ada_w, ada_b, ln_g, ln_b, attn_w_in, attn_w_out, rec_w_in, rec_lb_logits,
           rec_norm_w, rec_w_out, ffn_w_gate, ffn_w_up, ffn_w_down, router_w, moe_w_gate, moe_w_up,
           moe_w_down):
    b, s, d = x.shape
    depth = ada_w.shape[0]
    alpha = (2.0 * depth) ** 0.25
    ts = 512

    mods = _ada_mods(c, ada_w, ada_b).reshape(depth, b, 6, 1, d)
    one_plus = lambda t: 1.0 + t
    pos = positions.reshape(b, s, 1)

    for i in range(depth):
        shift_m, scale_m, gate_m, shift_f, scale_f, gate_f = (mods[i, :, r] for r in range(6))
        j = i // 2
        lng = ln_g[i].reshape(2, 1, d)
        lnb = ln_b[i].reshape(2, 1, d)
        if i % 2 == 0:
            qkv = _qkv_proj(x, one_plus(scale_m), shift_m, pos, attn_w_in[j].astype(BF16), ts)
            o = _moba_attention(qkv, d, n_grp=math.gcd(d // HEAD_DIM, 4),
                                ch=math.gcd(s // MOBA_BLOCK, 4))
            x, u = _proj_ln(o, attn_w_out[j].astype(BF16), x, one_plus(gate_m), lng[0], lnb[0],
                            one_plus(scale_f), shift_f, alpha, ts)
            x = _ffn_dense(u, ffn_w_gate[j].astype(BF16), ffn_w_up[j].astype(BF16),
                           ffn_w_down[j].astype(BF16), x, one_plus(gate_f), lng[1], lnb[1], alpha, ts)
        else:
            qf, key, v, gs, lf = _rec_in(x, one_plus(scale_m), shift_m, rec_w_in[j].astype(BF16),
                                         rec_lb_logits, i, ts)
            o = _hgrn_recurrence(qf, key, v, lf, gs, rec_norm_w[j].reshape(1, d), ts)
            w_hi = router_w[j].astype(BF16)
            w_lo = (router_w[j] - w_hi.astype(F32)).astype(BF16)
            wr = (jnp.zeros((d, 2 * LANES), BF16).at[:, :N_EXPERTS].set(w_hi)
                  .at[:, LANES:LANES + N_EXPERTS].set(w_lo))
            x, u, route = _proj_ln(o, rec_w_out[j].astype(BF16), x, one_plus(gate_m), lng[0], lnb[0],
                                   one_plus(scale_f), shift_f, alpha, ts, wr=wr)
            x = _moe_swiglu_ln(u, route, moe_w_gate[j].astype(BF16), moe_w_up[j].astype(BF16),
                               moe_w_down[j].astype(BF16), x, one_plus(gate_f), lng[1], lnb[1], alpha)
    return x
```

```python
import functools
import math

import jax
import jax.numpy as jnp
from jax import lax
from jax.experimental import pallas as pl
from jax.experimental.pallas import tpu as pltpu

HEAD_DIM = 128
ROPE_DIM = HEAD_DIM // 4
ROPE_THETA = 500000.0
MOBA_BLOCK = 256
MOBA_TOPK = 3
HGRN_EXPAND = 128
HGRN_CHUNK = 64
N_EXPERTS = 8
TOP_K = 2
LN_EPS = 1e-5
RMS_EPS = 1e-6

LANES = 128
MXU_N = 256
VMEM_LIMIT = 56 * 1024 * 1024
NEG_BIG = -1e30

F32 = jnp.float32
BF16 = jnp.bfloat16
HIGHEST = lax.Precision.HIGHEST


def _params(*sem):
    return pltpu.CompilerParams(dimension_semantics=sem, vmem_limit_bytes=VMEM_LIMIT)


def _dot(a, b):
    return jnp.dot(a, b, preferred_element_type=F32)


def _dot_nt(a, b, precision=None):
    return lax.dot_general(a, b, (((1,), (1,)), ((), ())), precision=precision,
                           preferred_element_type=F32)


def _silu(x):
    return x * jax.nn.sigmoid(x)


def _layer_norm(z, g, b):
    mu = jnp.mean(z, axis=-1, keepdims=True)
    d = z - mu
    var = jnp.mean(d * d, axis=-1, keepdims=True)
    return d * lax.rsqrt(var + LN_EPS) * g + b


def _ada_kernel(c_ref, w_ref, b_ref, o_ref):
    a = _silu(c_ref[...])
    o_ref[0] = jnp.dot(a, w_ref[0], precision=HIGHEST, preferred_element_type=F32) + b_ref[0]


def _ada_mods(c, ada_w, ada_b):
    depth, d, m = ada_w.shape
    b = c.shape[0]
    tn = m // 4
    return pl.pallas_call(
        _ada_kernel,
        grid=(depth, m // tn),
        in_specs=[pl.BlockSpec((b, d), lambda l, j: (0, 0)),
                  pl.BlockSpec((1, d, tn), lambda l, j: (l, 0, j)),
                  pl.BlockSpec((1, 1, tn), lambda l, j: (l, 0, j))],
        out_specs=pl.BlockSpec((1, b, tn), lambda l, j: (l, 0, j)),
        out_shape=jax.ShapeDtypeStruct((depth, b, m), F32),
        compiler_params=_params("arbitrary", "arbitrary"),
        name="ada_mods",
    )(c, ada_w, ada_b.reshape(depth, 1, m))


def _qkv_kernel(x_ref, sc_ref, sh_ref, pos_ref, w_ref, o_ref, *, n_rot):
    u = (x_ref[0] * sc_ref[0] + sh_ref[0]).astype(BF16)
    half = ROPE_DIM // 2
    lane = lax.broadcasted_iota(jnp.int32, (1, LANES), 1)
    fidx = (lane % half).astype(F32)
    inv = jnp.exp(-math.log(ROPE_THETA) * fidx * (2.0 / ROPE_DIM))
    inv = jnp.where(lane < ROPE_DIM, inv, 0.0)
    ang = pos_ref[0].astype(F32) * inv
    cos_t = jnp.cos(ang)
    sin_t = jnp.sin(ang)
    sin_lo = jnp.where(lane < half, -sin_t, 0.0)
    sin_hi = jnp.where((lane >= half) & (lane < ROPE_DIM), sin_t, 0.0)
    n_cols = w_ref.shape[1]
    for c in range(n_cols // MXU_N):
        y = _dot(u, w_ref[:, c * MXU_N:(c + 1) * MXU_N])
        for s in range(MXU_N // HEAD_DIM):
            col = c * MXU_N + s * HEAD_DIM
            t = y[:, s * HEAD_DIM:(s + 1) * HEAD_DIM]
            if col < n_rot:
                t = (t * cos_t + pltpu.roll(t, HEAD_DIM - half, 1) * sin_lo
                     + pltpu.roll(t, half, 1) * sin_hi)
            if col < n_rot // 2:
                t = t * (HEAD_DIM ** -0.5 * math.log2(math.e))
            o_ref[0, :, col:col + HEAD_DIM] = t.astype(BF16)


def _qkv_proj(x, sc, sh, pos, w, ts):
    b, s, d = x.shape
    n = w.shape[1]
    return pl.pallas_call(
        functools.partial(_qkv_kernel, n_rot=2 * d),
        grid=(b, s // ts),
        in_specs=[pl.BlockSpec((1, ts, d), lambda i, j: (i, j, 0)),
                  pl.BlockSpec((1, 1, d), lambda i, j: (i, 0, 0)),
                  pl.BlockSpec((1, 1, d), lambda i, j: (i, 0, 0)),
                  pl.BlockSpec((1, ts, 1), lambda i, j: (i, j, 0)),
                  pl.BlockSpec((d, n), lambda i, j: (0, 0))],
        out_specs=pl.BlockSpec((1, ts, n), lambda i, j: (i, j, 0)),
        out_shape=jax.ShapeDtypeStruct((b, s, n), BF16),
        compiler_params=_params("arbitrary", "arbitrary"),
        name="qkv_proj",
    )(x, sc, sh, pos, w)


def _moba_kernel(q_ref, k_ref, v_ref, o_ref, kmean_scr, vt_scr, bias_scr, m_scr, l_scr, acc_scr,
                 *, nb, n_grp, ch):
    blk = MOBA_BLOCK
    hd = HEAD_DIM
    cw = ch * blk
    i = pl.program_id(2)

    @pl.when(i == 0)
    def _():
        for hh in range(n_grp):
            means = []
            for n in range(nb):
                kb = k_ref[0, n * blk:(n + 1) * blk, hh * hd:(hh + 1) * hd].astype(F32)
                means.append(jnp.mean(kb, axis=0, keepdims=True))
                vb = v_ref[0, n * blk:(n + 1) * blk, hh * hd:(hh + 1) * hd]
                vt_scr[hh, n // ch, :, (n % ch) * blk:(n % ch + 1) * blk] = vb.astype(F32).T.astype(BF16)
            km = jnp.concatenate(means, axis=0)
            km_hi = km.astype(BF16)
            km_lo = (km - km_hi.astype(F32)).astype(BF16)
            kmean_scr[hh] = jnp.concatenate([km_hi, km_lo], axis=0)

    def sweep(c, n_own):
        own = n_own > 0
        n_blk = n_own if own else ch
        k0 = pl.multiple_of(c * cw, cw)
        heads = [slice(hh * hd, (hh + 1) * hd) for hh in range(n_grp)]
        scores = [_dot_nt(k_ref[0, pl.ds(k0, n_blk * blk), cols], q_ref[0, :, cols])
                  for cols in heads]
        m_news, l_news, probs = [], [], []
        for hh, s in enumerate(scores):
            parts = [s[t * blk:(t + 1) * blk] + bias_scr[hh, pl.ds(c * ch + t, 1), :]
                     for t in range(n_blk - 1 if own else n_blk)]
            if own:
                s_own = s[(n_blk - 1) * blk:]
                kpos = lax.broadcasted_iota(jnp.int32, s_own.shape, 0)
                qpos = lax.broadcasted_iota(jnp.int32, s_own.shape, 1)
                parts.append(jnp.where(kpos <= qpos, s_own, NEG_BIG))
            m_blk = parts[0]
            for t in range(1, n_blk):
                m_blk = jnp.maximum(m_blk, parts[t])
            m_new = jnp.max(m_blk, axis=0, keepdims=True)
            if not own:
                m_new = jnp.maximum(m_new, m_scr[hh])
            ps = [jnp.exp2(part - m_new) for part in parts]
            l_new = ps[0]
            for t in range(1, n_blk):
                l_new = l_new + ps[t]
            m_news.append(m_new)
            l_news.append(jnp.sum(l_new, axis=0, keepdims=True))
            probs.append(jnp.concatenate([p.astype(BF16) for p in ps], axis=0) if n_blk > 1
                         else ps[0].astype(BF16))
        pvs = [_dot(vt_scr[hh, c, :, 0:n_blk * blk], probs[hh]) for hh in range(n_grp)]
        for hh in range(n_grp):
            if own:
                l_scr[hh] = l_news[hh]
                acc_scr[hh] = pvs[hh]
            else:
                alpha = jnp.exp2(m_scr[hh] - m_news[hh])
                l_scr[hh] = alpha * l_scr[hh] + l_news[hh]
                acc_scr[hh] = alpha * acc_scr[hh] + pvs[hh]
            m_scr[hh] = m_news[hh]

    c_own = i // ch
    for hh in range(n_grp):
        q = q_ref[0, :, hh * hd:(hh + 1) * hd]
        gate2 = _dot_nt(kmean_scr[hh], q)
        gate = gate2[:nb] + gate2[nb:]
        blk_id = lax.broadcasted_iota(jnp.int32, gate.shape, 0)
        past = blk_id < i
        g = jnp.where(past, gate, -jnp.inf)
        rank = jnp.zeros(gate.shape, jnp.int32)
        for m in range(nb):
            gm = g[m:m + 1, :]
            beats = jnp.where(gm > g, 1, jnp.where(gm == g, jnp.where(blk_id > m, 1, 0), 0))
            rank = rank + beats
        bias_scr[hh] = jnp.where(past, jnp.where(rank < MOBA_TOPK, 0.0, NEG_BIG), NEG_BIG)

    for r in range(ch):
        @pl.when(i % ch == r)
        def _():
            sweep(c_own, r + 1)

    def body(c, carry):
        sweep(c, 0)
        return carry

    lax.fori_loop(0, c_own, body, 0)
    for hh in range(n_grp):
        o_ref[0, :, hh * hd:(hh + 1) * hd] = (acc_scr[hh] / l_scr[hh]).T.astype(BF16)


def _moba_attention(qkv, d, n_grp, ch):
    b, s, _ = qkv.shape
    h = d // HEAD_DIM
    blk = MOBA_BLOCK
    nb = s // blk
    gw = n_grp * HEAD_DIM
    ng = h // n_grp
    return pl.pallas_call(
        functools.partial(_moba_kernel, nb=nb, n_grp=n_grp, ch=ch),
        grid=(b, ng, nb),
        in_specs=[pl.BlockSpec((1, blk, gw), lambda bi, hi, i: (bi, i, hi)),
                  pl.BlockSpec((1, s, gw), lambda bi, hi, i: (bi, 0, ng + hi)),
                  pl.BlockSpec((1, s, gw), lambda bi, hi, i: (bi, 0, 2 * ng + hi))],
        out_specs=pl.BlockSpec((1, blk, gw), lambda bi, hi, i: (bi, i, hi)),
        out_shape=jax.ShapeDtypeStruct((b, s, d), BF16),
        scratch_shapes=[pltpu.VMEM((n_grp, 2 * nb, HEAD_DIM), BF16),
                        pltpu.VMEM((n_grp, nb // ch, HEAD_DIM, ch * blk), BF16),
                        pltpu.VMEM((n_grp, nb, blk), F32),
                        pltpu.VMEM((n_grp, 1, blk), F32),
                        pltpu.VMEM((n_grp, 1, blk), F32),
                        pltpu.VMEM((n_grp, HEAD_DIM, blk), F32)],
        compiler_params=_params("arbitrary", "arbitrary", "arbitrary"),
        name="moba_attention",
    )(qkv, qkv, qkv)


def _store_row_tiles(ref, val):
    rows, d = val.shape
    sub = d // LANES
    for c in range(sub):
        ref[pl.ds(c, rows, stride=sub), :] = val[:, c * LANES:(c + 1) * LANES]


def _load_row_tiles(ref, rows, dtype):
    sub = ref.shape[0] // rows
    return jnp.concatenate([ref[pl.ds(c, rows, stride=sub), :].astype(dtype) for c in range(sub)], axis=1)


def _route_top2(u, wr_ref):
    u_hi = u.astype(BF16)
    u_lo = (u - u_hi.astype(F32)).astype(BF16)
    both = _dot(u_hi, wr_ref[...])
    logits = both[:, :LANES] + both[:, LANES:] + _dot(u_lo, wr_ref[:, :LANES])
    lane = lax.broadcasted_iota(jnp.int32, logits.shape, 1)
    logits = jnp.where(lane < N_EXPERTS, logits, -jnp.inf)
    m1 = jnp.max(logits, axis=-1, keepdims=True)
    i1 = jnp.min(jnp.where(logits == m1, lane, LANES), axis=-1, keepdims=True)
    rest = jnp.where(lane == i1, -jnp.inf, logits)
    m2 = jnp.max(rest, axis=-1, keepdims=True)
    i2 = jnp.min(jnp.where(rest == m2, lane, LANES), axis=-1, keepdims=True)
    e2 = jnp.exp(m2 - m1)
    w1 = 1.0 / (1.0 + e2)
    w2 = e2 / (1.0 + e2)
    return jnp.where(lane == 0, i1.astype(F32),
                     jnp.where(lane == 1, i2.astype(F32),
                               jnp.where(lane == 2, w1, jnp.where(lane == 3, w2, 0.0))))


def _proj_ln_kernel(a_ref, w_ref, x_ref, g1p_ref, lng_ref, lnb_ref, sc_ref, sh_ref, *rest,
                    alpha, route):
    if route:
        wr_ref, xo_ref, uo_ref, ro_ref = rest
    else:
        xo_ref, uo_ref = rest
    y = _dot(a_ref[0], w_ref[...])
    xn = _layer_norm(alpha * x_ref[0] + g1p_ref[0] * y, lng_ref[...], lnb_ref[...])
    xo_ref[0] = xn
    u = xn * sc_ref[0] + sh_ref[0]
    if route:
        _store_row_tiles(uo_ref.at[0], u)
        ro_ref[0] = _route_top2(u, wr_ref)
    else:
        uo_ref[0] = u.astype(BF16)


def _proj_ln(a, w, x, g1p, lng, lnb, sc, sh, alpha, tm, wr=None):
    b, s, d = x.shape
    route = wr is not None
    tile = pl.BlockSpec((1, tm, d), lambda i, j: (i, j, 0))
    per_b = pl.BlockSpec((1, 1, d), lambda i, j: (i, 0, 0))
    vec = pl.BlockSpec((1, d), lambda i, j: (0, 0))
    in_specs = [tile, pl.BlockSpec((d, d), lambda i, j: (0, 0)), tile, per_b, vec, vec, per_b, per_b]
    out_specs = [tile, tile]
    out_shape = [jax.ShapeDtypeStruct((b, s, d), F32), jax.ShapeDtypeStruct((b, s, d), BF16)]
    args = [a, w, x, g1p, lng, lnb, sc, sh]
    if route:
        sub = d // LANES
        out_specs[1] = pl.BlockSpec((1, tm * sub, LANES), lambda i, j: (i, j, 0))
        out_shape[1] = jax.ShapeDtypeStruct((b, s * sub, LANES), F32)
        in_specs.append(pl.BlockSpec((d, 2 * LANES), lambda i, j: (0, 0)))
        out_specs.append(pl.BlockSpec((1, tm, LANES), lambda i, j: (i, j, 0)))
        out_shape.append(jax.ShapeDtypeStruct((b, s, LANES), F32))
        args.append(wr)
    return pl.pallas_call(
        functools.partial(_proj_ln_kernel, alpha=alpha, route=route),
        grid=(b, s // tm),
        in_specs=in_specs, out_specs=out_specs, out_shape=out_shape,
        compiler_params=_params("arbitrary", "arbitrary"),
        name="proj_ln_route" if route else "proj_ln",
    )(*args)


def _ffn_kernel(u_ref, wg_ref, wu_ref, wd_ref, x_ref, g1p_ref, lng_ref, lnb_ref, xo_ref, h_scr,
                *, alpha, tf):
    u = u_ref[0]
    f = wg_ref.shape[1]
    for j in range(f // tf):
        cols = slice(j * tf, (j + 1) * tf)
        hj = _silu(_dot(u, wg_ref[:, cols])) * _dot(u, wu_ref[:, cols])
        h_scr[:, cols] = hj.astype(BF16)
    y = _dot(h_scr[...], wd_ref[...])
    xo_ref[0] = _layer_norm(alpha * x_ref[0] + g1p_ref[0] * y, lng_ref[...], lnb_ref[...])


def _ffn_dense(u, wg, wu, wd, x, g1p, lng, lnb, alpha, tm):
    b, s, d = x.shape
    f = wg.shape[1]
    tf = MXU_N
    tile = pl.BlockSpec((1, tm, d), lambda i, j: (i, j, 0))
    per_b = pl.BlockSpec((1, 1, d), lambda i, j: (i, 0, 0))
    vec = pl.BlockSpec((1, d), lambda i, j: (0, 0))
    resident = dict(pipeline_mode=pl.Buffered(1))
    return pl.pallas_call(
        functools.partial(_ffn_kernel, alpha=alpha, tf=tf),
        grid=(b, s // tm),
        in_specs=[tile,
                  pl.BlockSpec((d, f), lambda i, j: (0, 0), **resident),
                  pl.BlockSpec((d, f), lambda i, j: (0, 0), **resident),
                  pl.BlockSpec((f, d), lambda i, j: (0, 0), **resident),
                  tile, per_b, vec, vec],
        out_specs=tile,
        out_shape=jax.ShapeDtypeStruct((b, s, d), F32),
        scratch_shapes=[pltpu.VMEM((tm, f), BF16)],
        compiler_params=_params("arbitrary", "arbitrary"),
        name="ffn_dense",
    )(u, wg, wu, wd, x, g1p, lng, lnb)


def _rec_in_kernel(x_ref, sc_ref, sh_ref, w_ref, lbl_ref, qf_ref, key_ref, v_ref, gs_ref, lf_ref,
                   *, layer_idx):
    u = (x_ref[0] * sc_ref[0] + sh_ref[0]).astype(BF16)
    d = x_ref.shape[2]
    lbl = lbl_ref[...]
    e = jnp.exp(lbl - jnp.max(lbl, axis=0, keepdims=True))
    sm = e / jnp.sum(e, axis=0, keepdims=True)
    lb = jnp.zeros((1, d), F32)
    for r in range(1, layer_idx + 1):
        lb = lb + sm[r:r + 1, :]
    for c in range(d // MXU_N):
        cols = slice(c * MXU_N, (c + 1) * MXU_N)
        q = _dot(u, w_ref[:, c * MXU_N:(c + 1) * MXU_N])
        qf_ref[0, :, cols] = _silu(q).astype(BF16)
        f = _dot(u, w_ref[:, d + c * MXU_N:d + (c + 1) * MXU_N])
        lbc = lb[:, cols]
        f_gate = lbc + (1.0 - lbc) * jax.nn.sigmoid(f)
        lf_ref[0, :, cols] = jnp.log(f_gate)
        key_ref[0, :, cols] = (1.0 - f_gate).astype(BF16)
        v = _dot(u, w_ref[:, 2 * d + c * MXU_N:2 * d + (c + 1) * MXU_N])
        v_ref[0, :, cols] = v.astype(BF16)
        g = _dot(u, w_ref[:, 3 * d + c * MXU_N:3 * d + (c + 1) * MXU_N])
        gs_ref[0, :, cols] = _silu(g).astype(BF16)


def _rec_in(x, sc, sh, w, lb_logits, layer_idx, ts):
    b, s, d = x.shape
    depth = lb_logits.shape[0]
    tile = pl.BlockSpec((1, ts, d), lambda i, j: (i, j, 0))
    per_b = pl.BlockSpec((1, 1, d), lambda i, j: (i, 0, 0))
    bf = jax.ShapeDtypeStruct((b, s, d), BF16)
    return pl.pallas_call(
        functools.partial(_rec_in_kernel, layer_idx=layer_idx),
        grid=(b, s // ts),
        in_specs=[tile, per_b, per_b,
                  pl.BlockSpec((d, 4 * d), lambda i, j: (0, 0)),
                  pl.BlockSpec((depth, d), lambda i, j: (0, 0))],
        out_specs=[tile] * 5,
        out_shape=[bf, bf, bf, bf, jax.ShapeDtypeStruct((b, s, d), F32)],
        compiler_params=_params("arbitrary", "arbitrary"),
        name="rec_in",
    )(x, sc, sh, w, lb_logits)


def _hgrn_kernel(qf_ref, key_ref, v_ref, lf_ref, gs_ref, nw_ref, o_ref, st_scr, *, n_heads):
    cs = HGRN_CHUNK
    dk = HGRN_EXPAND

    @pl.when(pl.program_id(1) == 0)
    def _():
        st_scr[...] = jnp.zeros(st_scr.shape, F32)

    row = lax.broadcasted_iota(jnp.int32, (cs, cs), 0)
    col = lax.broadcasted_iota(jnp.int32, (cs, cs), 1)
    causal = row >= col
    tri = jnp.where(causal, 1.0, 0.0).astype(BF16)
    n_chunks = qf_ref.shape[1] // cs

    def chunk(c, carry):
        rows = pl.ds(pl.multiple_of(c * cs, cs), cs)
        lf = lf_ref[0, rows, :]
        hi = lf.astype(BF16)
        r1 = lf - hi.astype(F32)
        mid = r1.astype(BF16)
        lo = (r1 - mid.astype(F32)).astype(BF16)
        g = _dot(tri, hi) + _dot(tri, mid) + _dot(tri, lo)
        g_last = g[cs - 1:cs, :]
        qf = qf_ref[0, rows, :].astype(F32)
        key = key_ref[0, rows, :].astype(F32)
        v = v_ref[0, rows, :]
        q_dec = (qf * jnp.exp(g)).astype(BF16)
        k_dec = (key * jnp.exp(-g)).astype(BF16)
        k_state = (key * jnp.exp(g_last - g)).astype(BF16)
        decay = jnp.exp(g_last)
        heads = [slice(h * dk, (h + 1) * dk) for h in range(n_heads)]
        a = [jnp.where(causal, _dot_nt(q_dec[:, c], k_dec[:, c]), 0.0).astype(BF16) for c in heads]
        st = [st_scr[h] for h in range(n_heads)]
        o = [_dot(a[h], v[:, c]) + _dot_nt(q_dec[:, c], st[h].astype(BF16))
             for h, c in enumerate(heads)]
        for h, c in enumerate(heads):
            v_t = v[:, c].astype(F32).T.astype(BF16)
            st_scr[h] = st[h] * decay[:, c] + _dot(v_t, k_state[:, c])
        on = jnp.concatenate(
            [oh * lax.rsqrt(jnp.mean(oh * oh, axis=-1, keepdims=True) + RMS_EPS) for oh in o], axis=1)
        o_ref[0, rows, :] = (on * nw_ref[...] * gs_ref[0, rows, :].astype(F32)).astype(BF16)
        return carry

    lax.fori_loop(0, n_chunks, chunk, 0)


def _hgrn_recurrence(qf, key, v, lf, gs, norm_w, ts):
    b, s, d = qf.shape
    h = d // HGRN_EXPAND
    tile = pl.BlockSpec((1, ts, d), lambda i, j: (i, j, 0))
    return pl.pallas_call(
        functools.partial(_hgrn_kernel, n_heads=h),
        grid=(b, s // ts),
        in_specs=[tile, tile, tile, tile, tile, pl.BlockSpec((1, d), lambda i, j: (0, 0))],
        out_specs=tile,
        out_shape=jax.ShapeDtypeStruct((b, s, d), BF16),
        scratch_shapes=[pltpu.VMEM((h, HGRN_EXPAND, HGRN_EXPAND), F32)],
        compiler_params=_params("arbitrary", "arbitrary"),
        name="hgrn_recurrence",
    )(qf, key, v, lf, gs, norm_w)


GATHER_UNROLL = 8


ROW_TILE = 8


def _start_row(idx_at, src_hbm, dst, sem, r):
    first = pl.multiple_of(idx_at(r), ROW_TILE)
    pltpu.make_async_copy(src_hbm.at[pl.ds(first, ROW_TILE), :],
                          dst.at[pl.ds(r * ROW_TILE, ROW_TILE), :], sem).start()


def _start_row_gather(idx_at, src_hbm, dst, sem, n_rows):
    def one(r, carry):
        _start_row(idx_at, src_hbm, dst, sem, r)
        return carry

    lax.fori_loop(0, n_rows, one, 0, unroll=GATHER_UNROLL)


def _wait_rows(buf, sem):
    pltpu.make_async_copy(buf, buf, sem).wait()


def _moe_ffn_kernel(te_ref, na_ref, src_cur_ref, src_nxt_ref, u_hbm, wg_ref, wu_ref, wd_ref, o_ref,
                    xbuf, xb_scr, h_scr, acc_scr, sems, *, tm, nf):
    i = pl.program_id(0)
    j = pl.program_id(1)
    n_tiles = pl.num_programs(0)
    rows_per_step = tm // nf
    n_active = na_ref[0]
    active = i < n_active
    slot = i % 2

    @pl.when((j == 0) & (i == 0))
    def _():
        _start_row_gather(lambda r: src_cur_ref[0, 0, r], u_hbm, xbuf.at[0], sems.at[0], tm)

    @pl.when((j == 0) & (i <= n_active))
    def _():
        _wait_rows(xbuf.at[slot], sems.at[slot])

    @pl.when(active & (j == 0))
    def _():
        xb_scr[...] = _load_row_tiles(xbuf.at[slot], tm, BF16)

    @pl.when(active)
    def _():
        for k in range(rows_per_step):
            _start_row(lambda r: src_nxt_ref[0, 0, r], u_hbm, xbuf.at[1 - slot], sems.at[1 - slot],
                       j * rows_per_step + k)
        x = xb_scr[...]
        for c in range(wg_ref.shape[2] // MXU_N):
            cols = slice(c * MXU_N, (c + 1) * MXU_N)
            hc = _silu(_dot(x, wg_ref[0, :, cols])) * _dot(x, wu_ref[0, :, cols])
            h_scr[:, cols] = hc.astype(BF16)
        y = _dot(h_scr[...], wd_ref[0])
        if nf == 1:
            _store_row_tiles(o_ref, y)
        else:
            @pl.when(j == 0)
            def _():
                acc_scr[...] = y

            @pl.when((j > 0) & (j < nf - 1))
            def _():
                acc_scr[...] += y

            @pl.when(j == nf - 1)
            def _():
                _store_row_tiles(o_ref, acc_scr[...] + y)

    @pl.when(jnp.logical_not(active) & (j == nf - 1))
    def _():
        o_ref[...] = jnp.zeros(o_ref.shape, F32)

    @pl.when(active & (i == n_tiles - 1) & (j == nf - 1))
    def _():
        _wait_rows(xbuf.at[1 - slot], sems.at[1 - slot])


def _moe_ffn(u_tiles, src, wg, wu, wd, tile_expert, n_active, tm, tf):
    d = wg.shape[1]
    sub = d // LANES
    assert sub == ROW_TILE
    n_tiles = src.shape[0] // tm
    f = wg.shape[2]
    nf = f // tf

    def frozen(j, i, na):
        return jnp.where(i < na[0], j, nf - 1)

    grid_spec = pltpu.PrefetchScalarGridSpec(
        num_scalar_prefetch=2,
        grid=(n_tiles, nf),
        in_specs=[pl.BlockSpec((1, 1, tm), lambda i, j, te, na: (i, 0, 0), memory_space=pltpu.SMEM),
                  pl.BlockSpec((1, 1, tm), lambda i, j, te, na: (jnp.minimum(i + 1, n_tiles - 1), 0, 0),
                               memory_space=pltpu.SMEM),
                  pl.BlockSpec(memory_space=pl.ANY),
                  pl.BlockSpec((1, d, tf), lambda i, j, te, na: (te[i], 0, frozen(j, i, na))),
                  pl.BlockSpec((1, d, tf), lambda i, j, te, na: (te[i], 0, frozen(j, i, na))),
                  pl.BlockSpec((1, tf, d), lambda i, j, te, na: (te[i], frozen(j, i, na), 0))],
        out_specs=pl.BlockSpec((tm * sub, LANES), lambda i, j, te, na: (i, 0)),
        scratch_shapes=[pltpu.VMEM((2, tm * sub, LANES), F32),
                        pltpu.VMEM((tm, d), BF16),
                        pltpu.VMEM((tm, tf), BF16),
                        pltpu.VMEM((tm, d), F32),
                        pltpu.SemaphoreType.DMA((2,))],
    )
    return pl.pallas_call(
        functools.partial(_moe_ffn_kernel, tm=tm, nf=nf),
        grid_spec=grid_spec,
        out_shape=jax.ShapeDtypeStruct((n_tiles * tm * sub, LANES), F32),
        compiler_params=_params("arbitrary", "arbitrary"),
        name="moe_ffn",
    )(tile_expert, n_active, src.reshape(n_tiles, 1, tm), src.reshape(n_tiles, 1, tm), u_tiles, wg, wu, wd)


def _combine_ln_kernel(sa_cur, sb_cur, sa_nxt, sb_nxt, ys_hbm, r_ref, x_ref, g1p_ref, lng_ref, lnb_ref,
                       xo_ref, ybuf, sems, *, alpha, tm):
    i = pl.program_id(0)
    slot = i % 2

    def start(sa, sb, s):
        _start_row_gather(lambda r: sa[r], ys_hbm, ybuf.at[s, 0], sems.at[s], tm)
        _start_row_gather(lambda r: sb[r], ys_hbm, ybuf.at[s, 1], sems.at[s], tm)

    @pl.when(i == 0)
    def _():
        start(sa_cur, sb_cur, 0)

    @pl.when(i + 1 < pl.num_programs(0))
    def _():
        start(sa_nxt, sb_nxt, 1 - slot)

    _wait_rows(ybuf.at[slot], sems.at[slot])
    r = r_ref[...]
    y = (r[:, 2:3] * _load_row_tiles(ybuf.at[slot, 0], tm, F32)
         + r[:, 3:4] * _load_row_tiles(ybuf.at[slot, 1], tm, F32))
    xo_ref[...] = _layer_norm(alpha * x_ref[...] + g1p_ref[0] * y, lng_ref[...], lnb_ref[...])


def _combine_ln(ys_tiles, slot_a, slot_b, route, x, g1p, lng, lnb, alpha, tm):
    b, s, d = x.shape
    n = b * s
    sub = d // LANES
    n_tiles = n // tm
    per_seq = s // tm
    cur = pl.BlockSpec((tm,), lambda i: (i,), memory_space=pltpu.SMEM)
    nxt = pl.BlockSpec((tm,), lambda i: (jnp.minimum(i + 1, n_tiles - 1),), memory_space=pltpu.SMEM)
    tile = pl.BlockSpec((tm, d), lambda i: (i, 0))
    vec = pl.BlockSpec((1, d), lambda i: (0, 0))
    out = pl.pallas_call(
        functools.partial(_combine_ln_kernel, alpha=alpha, tm=tm),
        grid=(n_tiles,),
        in_specs=[cur, cur, nxt, nxt, pl.BlockSpec(memory_space=pl.ANY),
                  pl.BlockSpec((tm, LANES), lambda i: (i, 0)), tile,
                  pl.BlockSpec((1, 1, d), lambda i: (i // per_seq, 0, 0)), vec, vec],
        out_specs=tile,
        out_shape=jax.ShapeDtypeStruct((n, d), F32),
        scratch_shapes=[pltpu.VMEM((2, 2, tm * sub, LANES), F32), pltpu.SemaphoreType.DMA((2,))],
        compiler_params=_params("arbitrary"),
        name="combine_ln",
    )(slot_a, slot_b, slot_a, slot_b, ys_tiles, route.reshape(n, LANES), x.reshape(n, d), g1p, lng, lnb)
    return out.reshape(b, s, d)


def _moe_plan(route, tm):
    n = route.shape[0]
    e = N_EXPERTS
    n_tiles = (TOP_K * n + e * (tm - 1)) // tm
    flat_e = route[:, :TOP_K].astype(jnp.int32).reshape(-1)
    onehot = (flat_e[:, None] == jnp.arange(e, dtype=jnp.int32)[None, :]).astype(jnp.int32)
    incl = jnp.cumsum(onehot, axis=0)
    counts = incl[-1]
    padded = (counts + tm - 1) // tm * tm
    ends = jnp.cumsum(padded)
    offs = ends - padded
    slot = jnp.sum(onehot * (offs[None, :] + incl - 1), axis=1)
    src = jnp.zeros((n_tiles * tm,), jnp.int32).at[slot].set(jnp.arange(TOP_K * n, dtype=jnp.int32) // TOP_K)
    n_active = ends[-1] // tm
    tile_id = jnp.minimum(jnp.arange(n_tiles, dtype=jnp.int32), n_active - 1)
    tile_expert = jnp.minimum(jnp.sum((tile_id[:, None] * tm >= ends[None, :]).astype(jnp.int32), axis=1), e - 1)
    return ((slot * ROW_TILE).reshape(n, TOP_K), src * ROW_TILE, tile_expert.astype(jnp.int32),
            n_active.reshape(1).astype(jnp.int32))


def _moe_swiglu_ln(u_tiles, route, wg, wu, wd, x, g1p, lng, lnb, alpha, tm_moe=896, tm_ln=512):
    assert TOP_K == 2
    b, s, d = x.shape
    n = b * s
    f = wg.shape[2]
    nf = 2 if f % (2 * MXU_N) == 0 else 1
    tf = f // nf
    assert tm_moe % nf == 0
    slot, src, tile_expert, n_active = _moe_plan(route.reshape(n, LANES), tm_moe)
    ys = _moe_ffn(u_tiles.reshape(n * (d // LANES), LANES), src, wg, wu, wd, tile_expert, n_active, tm_moe, tf)
    return _combine_ln(ys, slot[:, 0], slot[:, 1], route, x, g1p, lng, lnb, alpha, tm_ln)


def kernel(x, c, positions, ada_w, ada_b, ln_g, ln_b, attn_w_in, attn_w_out, rec_w_in, rec_lb_logits,
           rec_norm_w, rec_w_out, ffn_w_gate, ffn_w_up, ffn_w_down, router_w, moe_w_gate, moe_w_up,
           moe_w_down):
    b, s, d = x.shape
    depth = ada_w.shape[0]
    alpha = (2.0 * depth) ** 0.25
    ts = 512

    mods = _ada_mods(c, ada_w, ada_b).reshape(depth, b, 6, 1, d)
    one_plus = lambda t: 1.0 + t
    pos = positions.reshape(b, s, 1)

    for i in range(depth):
        shift_m, scale_m, gate_m, shift_f, scale_f, gate_f = (mods[i, :, r] for r in range(6))
        j = i // 2
        lng = ln_g[i].reshape(2, 1, d)
        lnb = ln_b[i].reshape(2, 1, d)
        if i % 2 == 0:
            qkv = _qkv_proj(x, one_plus(scale_m), shift_m, pos, attn_w_in[j].astype(BF16), ts)
            o = _moba_attention(qkv, d, n_grp=math.gcd(d // HEAD_DIM, 4),
                                ch=math.gcd(s // MOBA_BLOCK, 4))
            x, u = _proj_ln(o, attn_w_out[j].astype(BF16), x, one_plus(gate_m), lng[0], lnb[0],
                            one_plus(scale_f), shift_f, alpha, ts)
            x = _ffn_dense(u, ffn_w_gate[j].astype(BF16), ffn_w_up[j].astype(BF16),
                           ffn_w_down[j].astype(BF16), x, one_plus(gate_f), lng[1], lnb[1], alpha, ts)
        else:
            qf, key, v, gs, lf = _rec_in(x, one_plus(scale_m), shift_m, rec_w_in[j].astype(BF16),
                                         rec_lb_logits, i, ts)
            o = _hgrn_recurrence(qf, key, v, lf, gs, rec_norm_w[j].reshape(1, d), ts)
            w_hi = router_w[j].astype(BF16)
            w_lo = (router_w[j] - w_hi.astype(F32)).astype(BF16)
            wr = (jnp.zeros((d, 2 * LANES), BF16).at[:, :N_EXPERTS].set(w_hi)
                  .at[:, LANES:LANES + N_EXPERTS].set(w_lo))
            x, u, route = _proj_ln(o, rec_w_out[j].astype(BF16), x, one_plus(gate_m), lng[0], lnb[0],
                                   one_plus(scale_f), shift_f, alpha, ts, wr=wr)
            x = _moe_swiglu_ln(u, route, moe_w_gate[j].astype(BF16), moe_w_up[j].astype(BF16),
                               moe_w_down[j].astype(BF16), x, one_plus(gate_f), lng[1], lnb[1], alpha)
    return x
```

```python
import functools
import math

import jax
import jax.numpy as jnp
from jax import lax
from jax.experimental import pallas as pl
from jax.experimental.pallas import tpu as pltpu

HEAD_DIM = 128
ROPE_DIM = HEAD_DIM // 4
ROPE_THETA = 500000.0
MOBA_BLOCK = 256
MOBA_TOPK = 3
HGRN_EXPAND = 128
HGRN_CHUNK = 64
N_EXPERTS = 8
TOP_K = 2
LN_EPS = 1e-5
RMS_EPS = 1e-6

LANES = 128
MXU_N = 256
VMEM_LIMIT = 56 * 1024 * 1024
NEG_BIG = -1e30

F32 = jnp.float32
BF16 = jnp.bfloat16
HIGHEST = lax.Precision.HIGHEST


def _params(*sem):
    return pltpu.CompilerParams(dimension_semantics=sem, vmem_limit_bytes=VMEM_LIMIT)


def _dot(a, b):
    return jnp.dot(a, b, preferred_element_type=F32)


def _dot_nt(a, b, precision=None):
    return lax.dot_general(a, b, (((1,), (1,)), ((), ())), precision=precision,
                           preferred_element_type=F32)


def _silu(x):
    return x * jax.nn.sigmoid(x)


def _layer_norm(z, g, b):
    mu = jnp.mean(z, axis=-1, keepdims=True)
    d = z - mu
    var = jnp.mean(d * d, axis=-1, keepdims=True)
    return d * lax.rsqrt(var + LN_EPS) * g + b


def _ada_kernel(c_ref, w_ref, b_ref, o_ref):
    a = _silu(c_ref[...])
    o_ref[0] = jnp.dot(a, w_ref[0], precision=HIGHEST, preferred_element_type=F32) + b_ref[0]


def _ada_mods(c, ada_w, ada_b):
    depth, d, m = ada_w.shape
    b = c.shape[0]
    tn = m // 4
    return pl.pallas_call(
        _ada_kernel,
        grid=(depth, m // tn),
        in_specs=[pl.BlockSpec((b, d), lambda l, j: (0, 0)),
                  pl.BlockSpec((1, d, tn), lambda l, j: (l, 0, j)),
                  pl.BlockSpec((1, 1, tn), lambda l, j: (l, 0, j))],
        out_specs=pl.BlockSpec((1, b, tn), lambda l, j: (l, 0, j)),
        out_shape=jax.ShapeDtypeStruct((depth, b, m), F32),
        compiler_params=_params("arbitrary", "arbitrary"),
        name="ada_mods",
    )(c, ada_w, ada_b.reshape(depth, 1, m))


def _qkv_kernel(x_ref, sc_ref, sh_ref, pos_ref, w_ref, o_ref, trig_scr, *, n_rot):
    u = (x_ref[0] * sc_ref[0] + sh_ref[0]).astype(BF16)
    half = ROPE_DIM // 2
    pack = LANES // ROPE_DIM
    rows = x_ref.shape[1] // pack
    lane = lax.broadcasted_iota(jnp.int32, (1, LANES), 1)
    inv = jnp.exp(-math.log(ROPE_THETA) * (lane % half).astype(F32) * (2.0 / ROPE_DIM))
    pos = jnp.zeros((rows, LANES), F32)
    for m in range(pack):
        pos = jnp.where(lane // ROPE_DIM == m, pos_ref[0, pl.ds(m, rows, stride=pack), :].astype(F32), pos)
    ang = pos * inv
    for t, trig in enumerate((jnp.cos(ang), jnp.sin(ang))):
        for m in range(pack):
            moved = trig if m == 0 else pltpu.roll(trig, LANES - ROPE_DIM * m, 1)
            trig_scr[t, pl.ds(m, rows, stride=pack), :] = moved
    cos_t = jnp.where(lane < ROPE_DIM, trig_scr[0], 1.0)
    sin_t = jnp.where(lane < ROPE_DIM, trig_scr[1], 0.0)
    sin_lo = jnp.where(lane < half, -sin_t, 0.0)
    sin_hi = jnp.where((lane >= half) & (lane < ROPE_DIM), sin_t, 0.0)
    n_cols = w_ref.shape[1]
    for c in range(n_cols // MXU_N):
        y = _dot(u, w_ref[:, c * MXU_N:(c + 1) * MXU_N])
        for s in range(MXU_N // HEAD_DIM):
            col = c * MXU_N + s * HEAD_DIM
            t = y[:, s * HEAD_DIM:(s + 1) * HEAD_DIM]
            if col < n_rot:
                t = (t * cos_t + pltpu.roll(t, HEAD_DIM - half, 1) * sin_lo
                     + pltpu.roll(t, half, 1) * sin_hi)
            if col < n_rot // 2:
                t = t * (HEAD_DIM ** -0.5 * math.log2(math.e))
            o_ref[0, :, col:col + HEAD_DIM] = t.astype(BF16)


def _qkv_proj(x, sc, sh, pos, w, ts):
    b, s, d = x.shape
    n = w.shape[1]
    return pl.pallas_call(
        functools.partial(_qkv_kernel, n_rot=2 * d),
        grid=(b, s // ts),
        in_specs=[pl.BlockSpec((1, ts, d), lambda i, j: (i, j, 0)),
                  pl.BlockSpec((1, 1, d), lambda i, j: (i, 0, 0)),
                  pl.BlockSpec((1, 1, d), lambda i, j: (i, 0, 0)),
                  pl.BlockSpec((1, ts, 1), lambda i, j: (i, j, 0)),
                  pl.BlockSpec((d, n), lambda i, j: (0, 0))],
        out_specs=pl.BlockSpec((1, ts, n), lambda i, j: (i, j, 0)),
        out_shape=jax.ShapeDtypeStruct((b, s, n), BF16),
        scratch_shapes=[pltpu.VMEM((2, ts, LANES), F32)],
        compiler_params=_params("arbitrary", "arbitrary"),
        name="qkv_proj",
    )(x, sc, sh, pos, w)


def _moba_kernel(q_ref, k_ref, v_ref, o_ref, kmean_scr, vt_scr, bias_scr, m_scr, l_scr, acc_scr,
                 *, nb, n_grp, ch):
    blk = MOBA_BLOCK
    hd = HEAD_DIM
    cw = ch * blk
    i = pl.program_id(2)

    @pl.when(i == 0)
    def _():
        for hh in range(n_grp):
            means = []
            for n in range(nb):
                kb = k_ref[0, n * blk:(n + 1) * blk, hh * hd:(hh + 1) * hd].astype(F32)
                means.append(jnp.mean(kb, axis=0, keepdims=True))
                vb = v_ref[0, n * blk:(n + 1) * blk, hh * hd:(hh + 1) * hd]
                vt_scr[hh, n // ch, :, (n % ch) * blk:(n % ch + 1) * blk] = vb.astype(F32).T.astype(BF16)
            km = jnp.concatenate(means, axis=0)
            km_hi = km.astype(BF16)
            km_lo = (km - km_hi.astype(F32)).astype(BF16)
            kmean_scr[hh] = jnp.concatenate([km_hi, km_lo], axis=0)

    def sweep(c, n_own):
        own = n_own > 0
        n_blk = n_own if own else ch
        k0 = pl.multiple_of(c * cw, cw)
        heads = [slice(hh * hd, (hh + 1) * hd) for hh in range(n_grp)]
        scores = [_dot_nt(k_ref[0, pl.ds(k0, n_blk * blk), cols], q_ref[0, :, cols])
                  for cols in heads]
        m_news, l_news, probs = [], [], []
        for hh, s in enumerate(scores):
            parts = [s[t * blk:(t + 1) * blk] + bias_scr[hh, pl.ds(c * ch + t, 1), :]
                     for t in range(n_blk - 1 if own else n_blk)]
            if own:
                s_own = s[(n_blk - 1) * blk:]
                kpos = lax.broadcasted_iota(jnp.int32, s_own.shape, 0)
                qpos = lax.broadcasted_iota(jnp.int32, s_own.shape, 1)
                parts.append(jnp.where(kpos <= qpos, s_own, NEG_BIG))
            m_blk = parts[0]
            for t in range(1, n_blk):
                m_blk = jnp.maximum(m_blk, parts[t])
            m_new = jnp.max(m_blk, axis=0, keepdims=True)
            if not own:
                m_new = jnp.maximum(m_new, m_scr[hh])
            ps = [jnp.exp2(part - m_new) for part in parts]
            l_new = ps[0]
            for t in range(1, n_blk):
                l_new = l_new + ps[t]
            m_news.append(m_new)
            l_news.append(jnp.sum(l_new, axis=0, keepdims=True))
            probs.append(jnp.concatenate([p.astype(BF16) for p in ps], axis=0) if n_blk > 1
                         else ps[0].astype(BF16))
        pvs = [_dot(vt_scr[hh, c, :, 0:n_blk * blk], probs[hh]) for hh in range(n_grp)]
        for hh in range(n_grp):
            if own:
                l_scr[hh] = l_news[hh]
                acc_scr[hh] = pvs[hh]
            else:
                alpha = jnp.exp2(m_scr[hh] - m_news[hh])
                l_scr[hh] = alpha * l_scr[hh] + l_news[hh]
                acc_scr[hh] = alpha * acc_scr[hh] + pvs[hh]
            m_scr[hh] = m_news[hh]

    c_own = i // ch
    for hh in range(n_grp):
        q = q_ref[0, :, hh * hd:(hh + 1) * hd]
        gate2 = _dot_nt(kmean_scr[hh], q)
        gate = gate2[:nb] + gate2[nb:]
        blk_id = lax.broadcasted_iota(jnp.int32, gate.shape, 0)
        past = blk_id < i
        g = jnp.where(past, gate, -jnp.inf)
        rank = jnp.zeros(gate.shape, jnp.int32)
        for m in range(nb):
            gm = g[m:m + 1, :]
            beats = jnp.where(gm > g, 1, jnp.where(gm == g, jnp.where(blk_id > m, 1, 0), 0))
            rank = rank + beats
        bias_scr[hh] = jnp.where(past, jnp.where(rank < MOBA_TOPK, 0.0, NEG_BIG), NEG_BIG)

    for r in range(ch):
        @pl.when(i % ch == r)
        def _():
            sweep(c_own, r + 1)

    def body(c, carry):
        sweep(c, 0)
        return carry

    lax.fori_loop(0, c_own, body, 0)
    for hh in range(n_grp):
        o_ref[0, :, hh * hd:(hh + 1) * hd] = (acc_scr[hh] / l_scr[hh]).T.astype(BF16)


def _moba_attention(qkv, d, n_grp, ch):
    b, s, _ = qkv.shape
    h = d // HEAD_DIM
    blk = MOBA_BLOCK
    nb = s // blk
    gw = n_grp * HEAD_DIM
    ng = h // n_grp
    return pl.pallas_call(
        functools.partial(_moba_kernel, nb=nb, n_grp=n_grp, ch=ch),
        grid=(b, ng, nb),
        in_specs=[pl.BlockSpec((1, blk, gw), lambda bi, hi, i: (bi, i, hi)),
                  pl.BlockSpec((1, s, gw), lambda bi, hi, i: (bi, 0, ng + hi)),
                  pl.BlockSpec((1, s, gw), lambda bi, hi, i: (bi, 0, 2 * ng + hi))],
        out_specs=pl.BlockSpec((1, blk, gw), lambda bi, hi, i: (bi, i, hi)),
        out_shape=jax.ShapeDtypeStruct((b, s, d), BF16),
        scratch_shapes=[pltpu.VMEM((n_grp, 2 * nb, HEAD_DIM), BF16),
                        pltpu.VMEM((n_grp, nb // ch, HEAD_DIM, ch * blk), BF16),
                        pltpu.VMEM((n_grp, nb, blk), F32),
                        pltpu.VMEM((n_grp, 1, blk), F32),
                        pltpu.VMEM((n_grp, 1, blk), F32),
                        pltpu.VMEM((n_grp, HEAD_DIM, blk), F32)],
        compiler_params=_params("arbitrary", "arbitrary", "arbitrary"),
        name="moba_attention",
    )(qkv, qkv, qkv)


def _store_row_tiles(ref, val):
    rows, d = val.shape
    sub = d // LANES
    for c in range(sub):
        ref[pl.ds(c, rows, stride=sub), :] = val[:, c * LANES:(c + 1) * LANES]


def _load_row_tiles(ref, rows, dtype):
    sub = ref.shape[0] // rows
    return jnp.concatenate([ref[pl.ds(c, rows, stride=sub), :].astype(dtype) for c in range(sub)], axis=1)


def _route_top2(u, wr_ref):
    u_hi = u.astype(BF16)
    u_lo = (u - u_hi.astype(F32)).astype(BF16)
    both = _dot(u_hi, wr_ref[...])
    logits = both[:, :LANES] + both[:, LANES:] + _dot(u_lo, wr_ref[:, :LANES])
    lane = lax.broadcasted_iota(jnp.int32, logits.shape, 1)
    logits = jnp.where(lane < N_EXPERTS, logits, -jnp.inf)
    m1 = jnp.max(logits, axis=-1, keepdims=True)
    i1 = jnp.min(jnp.where(logits == m1, lane, LANES), axis=-1, keepdims=True)
    rest = jnp.where(lane == i1, -jnp.inf, logits)
    m2 = jnp.max(rest, axis=-1, keepdims=True)
    i2 = jnp.min(jnp.where(rest == m2, lane, LANES), axis=-1, keepdims=True)
    e2 = jnp.exp(m2 - m1)
    w1 = 1.0 / (1.0 + e2)
    w2 = e2 / (1.0 + e2)
    return jnp.where(lane == 0, i1.astype(F32),
                     jnp.where(lane == 1, i2.astype(F32),
                               jnp.where(lane == 2, w1, jnp.where(lane == 3, w2, 0.0))))


def _proj_ln_route_kernel(a_ref, w_ref, x_ref, g1p_ref, lng_ref, lnb_ref, sc_ref, sh_ref, wr_ref,
                          xo_ref, uo_ref, ro_ref, *, alpha):
    y = _dot(a_ref[0], w_ref[...])
    xn = _layer_norm(alpha * x_ref[0] + g1p_ref[0] * y, lng_ref[...], lnb_ref[...])
    xo_ref[0] = xn
    u = xn * sc_ref[0] + sh_ref[0]
    _store_row_tiles(uo_ref.at[0], u)
    ro_ref[0] = _route_top2(u, wr_ref)


def _proj_ln_route(a, w, x, g1p, lng, lnb, sc, sh, wr, alpha, tm):
    b, s, d = x.shape
    sub = d // LANES
    tile = pl.BlockSpec((1, tm, d), lambda i, j: (i, j, 0))
    per_b = pl.BlockSpec((1, 1, d), lambda i, j: (i, 0, 0))
    vec = pl.BlockSpec((1, d), lambda i, j: (0, 0))
    return pl.pallas_call(
        functools.partial(_proj_ln_route_kernel, alpha=alpha),
        grid=(b, s // tm),
        in_specs=[tile, pl.BlockSpec((d, d), lambda i, j: (0, 0)), tile, per_b, vec, vec, per_b, per_b,
                  pl.BlockSpec((d, 2 * LANES), lambda i, j: (0, 0))],
        out_specs=[tile, pl.BlockSpec((1, tm * sub, LANES), lambda i, j: (i, j, 0)),
                   pl.BlockSpec((1, tm, LANES), lambda i, j: (i, j, 0))],
        out_shape=[jax.ShapeDtypeStruct((b, s, d), F32), jax.ShapeDtypeStruct((b, s * sub, LANES), F32),
                   jax.ShapeDtypeStruct((b, s, LANES), F32)],
        compiler_params=_params("arbitrary", "arbitrary"),
        name="proj_ln_route",
    )(a, w, x, g1p, lng, lnb, sc, sh, wr)


def _proj_ffn_kernel(a_ref, wo_ref, x_ref, g1m_ref, lng_ref, lnb_ref, sc_ref, sh_ref, wg_ref, wu_ref, wd_ref,
                     g1f_ref, xo_ref, h_scr, *, alpha, tf):
    x1 = _layer_norm(alpha * x_ref[0] + g1m_ref[0] * _dot(a_ref[0], wo_ref[...]), lng_ref[0], lnb_ref[0])
    u = (x1 * sc_ref[0] + sh_ref[0]).astype(BF16)
    f = wg_ref.shape[1]
    for j in range(f // tf):
        cols = slice(j * tf, (j + 1) * tf)
        hj = _silu(_dot(u, wg_ref[:, cols])) * _dot(u, wu_ref[:, cols])
        h_scr[:, cols] = hj.astype(BF16)
    y = _dot(h_scr[...], wd_ref[...])
    xo_ref[0] = _layer_norm(alpha * x1 + g1f_ref[0] * y, lng_ref[1], lnb_ref[1])


def _proj_ffn_dense(a, wo, x, g1m, lng, lnb, sc, sh, wg, wu, wd, g1f, alpha, tm):
    b, s, d = x.shape
    f = wg.shape[1]
    tile = pl.BlockSpec((1, tm, d), lambda i, j: (i, j, 0))
    per_b = pl.BlockSpec((1, 1, d), lambda i, j: (i, 0, 0))
    ln_pair = pl.BlockSpec((2, 1, d), lambda i, j: (0, 0, 0))
    resident = dict(pipeline_mode=pl.Buffered(1))
    return pl.pallas_call(
        functools.partial(_proj_ffn_kernel, alpha=alpha, tf=MXU_N),
        grid=(b, s // tm),
        in_specs=[tile, pl.BlockSpec((d, d), lambda i, j: (0, 0), **resident), tile, per_b, ln_pair, ln_pair,
                  per_b, per_b,
                  pl.BlockSpec((d, f), lambda i, j: (0, 0), **resident),
                  pl.BlockSpec((d, f), lambda i, j: (0, 0), **resident),
                  pl.BlockSpec((f, d), lambda i, j: (0, 0), **resident),
                  per_b],
        out_specs=tile,
        out_shape=jax.ShapeDtypeStruct((b, s, d), F32),
        scratch_shapes=[pltpu.VMEM((tm, f), BF16)],
        compiler_params=_params("arbitrary", "arbitrary"),
        name="proj_ffn_dense",
    )(a, wo, x, g1m, lng, lnb, sc, sh, wg, wu, wd, g1f)


def _rec_in_kernel(x_ref, sc_ref, sh_ref, w_ref, lbl_ref, qf_ref, key_ref, v_ref, gs_ref, lf_ref,
                   *, layer_idx):
    u = (x_ref[0] * sc_ref[0] + sh_ref[0]).astype(BF16)
    d = x_ref.shape[2]
    lbl = lbl_ref[...]
    e = jnp.exp(lbl - jnp.max(lbl, axis=0, keepdims=True))
    sm = e / jnp.sum(e, axis=0, keepdims=True)
    lb = jnp.zeros((1, d), F32)
    for r in range(1, layer_idx + 1):
        lb = lb + sm[r:r + 1, :]
    for c in range(d // MXU_N):
        cols = slice(c * MXU_N, (c + 1) * MXU_N)
        q = _dot(u, w_ref[:, c * MXU_N:(c + 1) * MXU_N])
        qf_ref[0, :, cols] = _silu(q).astype(BF16)
        f = _dot(u, w_ref[:, d + c * MXU_N:d + (c + 1) * MXU_N])
        lbc = lb[:, cols]
        f_gate = lbc + (1.0 - lbc) * jax.nn.sigmoid(f)
        lf_ref[0, :, cols] = jnp.log(f_gate)
        key_ref[0, :, cols] = (1.0 - f_gate).astype(BF16)
        v = _dot(u, w_ref[:, 2 * d + c * MXU_N:2 * d + (c + 1) * MXU_N])
        v_ref[0, :, cols] = v.astype(BF16)
        g = _dot(u, w_ref[:, 3 * d + c * MXU_N:3 * d + (c + 1) * MXU_N])
        gs_ref[0, :, cols] = _silu(g).astype(BF16)


def _rec_in(x, sc, sh, w, lb_logits, layer_idx, ts):
    b, s, d = x.shape
    depth = lb_logits.shape[0]
    tile = pl.BlockSpec((1, ts, d), lambda i, j: (i, j, 0))
    per_b = pl.BlockSpec((1, 1, d), lambda i, j: (i, 0, 0))
    bf = jax.ShapeDtypeStruct((b, s, d), BF16)
    return pl.pallas_call(
        functools.partial(_rec_in_kernel, layer_idx=layer_idx),
        grid=(b, s // ts),
        in_specs=[tile, per_b, per_b,
                  pl.BlockSpec((d, 4 * d), lambda i, j: (0, 0)),
                  pl.BlockSpec((depth, d), lambda i, j: (0, 0))],
        out_specs=[tile] * 5,
        out_shape=[bf, bf, bf, bf, jax.ShapeDtypeStruct((b, s, d), F32)],
        compiler_params=_params("arbitrary", "arbitrary"),
        name="rec_in",
    )(x, sc, sh, w, lb_logits)


def _hgrn_kernel(qf_ref, key_ref, v_ref, lf_ref, gs_ref, nw_ref, o_ref, st_scr, *, n_heads):
    cs = HGRN_CHUNK
    dk = HGRN_EXPAND

    @pl.when(pl.program_id(1) == 0)
    def _():
        st_scr[...] = jnp.zeros(st_scr.shape, F32)

    row = lax.broadcasted_iota(jnp.int32, (cs, cs), 0)
    col = lax.broadcasted_iota(jnp.int32, (cs, cs), 1)
    causal = row >= col
    tri = jnp.where(causal, 1.0, 0.0).astype(BF16)
    n_chunks = qf_ref.shape[1] // cs

    def chunk(c, carry):
        rows = pl.ds(pl.multiple_of(c * cs, cs), cs)
        lf = lf_ref[0, rows, :]
        hi = lf.astype(BF16)
        r1 = lf - hi.astype(F32)
        mid = r1.astype(BF16)
        lo = (r1 - mid.astype(F32)).astype(BF16)
        g = _dot(tri, hi) + _dot(tri, mid) + _dot(tri, lo)
        g_last = g[cs - 1:cs, :]
        qf = qf_ref[0, rows, :].astype(F32)
        key = key_ref[0, rows, :].astype(F32)
        v = v_ref[0, rows, :]
        q_dec = (qf * jnp.exp(g)).astype(BF16)
        k_dec = (key * jnp.exp(-g)).astype(BF16)
        k_state = (key * jnp.exp(g_last - g)).astype(BF16)
        decay = jnp.exp(g_last)
        heads = [slice(h * dk, (h + 1) * dk) for h in range(n_heads)]
        a = [jnp.where(causal, _dot_nt(q_dec[:, c], k_dec[:, c]), 0.0).astype(BF16) for c in heads]
        st = [st_scr[h] for h in range(n_heads)]
        o = [_dot(a[h], v[:, c]) + _dot_nt(q_dec[:, c], st[h].astype(BF16))
             for h, c in enumerate(heads)]
        for h, c in enumerate(heads):
            v_t = v[:, c].astype(F32).T.astype(BF16)
            st_scr[h] = st[h] * decay[:, c] + _dot(v_t, k_state[:, c])
        on = jnp.concatenate(
            [oh * lax.rsqrt(jnp.mean(oh * oh, axis=-1, keepdims=True) + RMS_EPS) for oh in o], axis=1)
        o_ref[0, rows, :] = (on * nw_ref[...] * gs_ref[0, rows, :].astype(F32)).astype(BF16)
        return carry

    lax.fori_loop(0, n_chunks, chunk, 0)


def _hgrn_recurrence(qf, key, v, lf, gs, norm_w, ts):
    b, s, d = qf.shape
    h = d // HGRN_EXPAND
    tile = pl.BlockSpec((1, ts, d), lambda i, j: (i, j, 0))
    return pl.pallas_call(
        functools.partial(_hgrn_kernel, n_heads=h),
        grid=(b, s // ts),
        in_specs=[tile, tile, tile, tile, tile, pl.BlockSpec((1, d), lambda i, j: (0, 0))],
        out_specs=tile,
        out_shape=jax.ShapeDtypeStruct((b, s, d), BF16),
        scratch_shapes=[pltpu.VMEM((h, HGRN_EXPAND, HGRN_EXPAND), F32)],
        compiler_params=_params("arbitrary", "arbitrary"),
        name="hgrn_recurrence",
    )(qf, key, v, lf, gs, norm_w)


GATHER_UNROLL = 8


ROW_TILE = 8


def _start_row(idx_at, src_hbm, dst, sem, r):
    first = pl.multiple_of(idx_at(r), ROW_TILE)
    pltpu.make_async_copy(src_hbm.at[pl.ds(first, ROW_TILE), :],
                          dst.at[pl.ds(r * ROW_TILE, ROW_TILE), :], sem).start()


def _start_row_gather(idx_at, src_hbm, dst, sem, n_rows):
    def one(r, carry):
        _start_row(idx_at, src_hbm, dst, sem, r)
        return carry

    lax.fori_loop(0, n_rows, one, 0, unroll=GATHER_UNROLL)


def _wait_rows(buf, sem):
    pltpu.make_async_copy(buf, buf, sem).wait()


def _moe_ffn_kernel(te_ref, na_ref, src_cur_ref, src_nxt_ref, u_hbm, wg_ref, wu_ref, wd_ref, o_ref,
                    xbuf, xb_scr, h_scr, acc_scr, sems, *, tm, nf):
    i = pl.program_id(0)
    j = pl.program_id(1)
    n_tiles = pl.num_programs(0)
    rows_per_step = tm // nf
    n_active = na_ref[0]
    active = i < n_active
    slot = i % 2

    @pl.when((j == 0) & (i == 0))
    def _():
        _start_row_gather(lambda r: src_cur_ref[0, 0, r], u_hbm, xbuf.at[0], sems.at[0], tm)

    @pl.when((j == 0) & (i <= n_active))
    def _():
        _wait_rows(xbuf.at[slot], sems.at[slot])

    @pl.when(active & (j == 0))
    def _():
        xb_scr[...] = _load_row_tiles(xbuf.at[slot], tm, BF16)

    @pl.when(active)
    def _():
        for k in range(rows_per_step):
            _start_row(lambda r: src_nxt_ref[0, 0, r], u_hbm, xbuf.at[1 - slot], sems.at[1 - slot],
                       j * rows_per_step + k)
        x = xb_scr[...]
        for c in range(wg_ref.shape[2] // MXU_N):
            cols = slice(c * MXU_N, (c + 1) * MXU_N)
            hc = _silu(_dot(x, wg_ref[0, :, cols])) * _dot(x, wu_ref[0, :, cols])
            h_scr[:, cols] = hc.astype(BF16)
        y = _dot(h_scr[...], wd_ref[0])
        if nf == 1:
            _store_row_tiles(o_ref, y)
        else:
            @pl.when(j == 0)
            def _():
                acc_scr[...] = y

            @pl.when((j > 0) & (j < nf - 1))
            def _():
                acc_scr[...] += y

            @pl.when(j == nf - 1)
            def _():
                _store_row_tiles(o_ref, acc_scr[...] + y)

    @pl.when(jnp.logical_not(active) & (j == nf - 1))
    def _():
        o_ref[...] = jnp.zeros(o_ref.shape, F32)

    @pl.when(active & (i == n_tiles - 1) & (j == nf - 1))
    def _():
        _wait_rows(xbuf.at[1 - slot], sems.at[1 - slot])


def _moe_ffn(u_tiles, src, wg, wu, wd, tile_expert, n_active, tm, tf):
    d = wg.shape[1]
    sub = d // LANES
    assert sub == ROW_TILE
    n_tiles = src.shape[0] // tm
    f = wg.shape[2]
    nf = f // tf

    def frozen(j, i, na):
        return jnp.where(i < na[0], j, nf - 1)

    grid_spec = pltpu.PrefetchScalarGridSpec(
        num_scalar_prefetch=2,
        grid=(n_tiles, nf),
        in_specs=[pl.BlockSpec((1, 1, tm), lambda i, j, te, na: (i, 0, 0), memory_space=pltpu.SMEM),
                  pl.BlockSpec((1, 1, tm), lambda i, j, te, na: (jnp.minimum(i + 1, n_tiles - 1), 0, 0),
                               memory_space=pltpu.SMEM),
                  pl.BlockSpec(memory_space=pl.ANY),
                  pl.BlockSpec((1, d, tf), lambda i, j, te, na: (te[i], 0, frozen(j, i, na))),
                  pl.BlockSpec((1, d, tf), lambda i, j, te, na: (te[i], 0, frozen(j, i, na))),
                  pl.BlockSpec((1, tf, d), lambda i, j, te, na: (te[i], frozen(j, i, na), 0))],
        out_specs=pl.BlockSpec((tm * sub, LANES), lambda i, j, te, na: (i, 0)),
        scratch_shapes=[pltpu.VMEM((2, tm * sub, LANES), F32),
                        pltpu.VMEM((tm, d), BF16),
                        pltpu.VMEM((tm, tf), BF16),
                        pltpu.VMEM((tm, d), F32),
                        pltpu.SemaphoreType.DMA((2,))],
    )
    return pl.pallas_call(
        functools.partial(_moe_ffn_kernel, tm=tm, nf=nf),
        grid_spec=grid_spec,
        out_shape=jax.ShapeDtypeStruct((n_tiles * tm * sub, LANES), F32),
        compiler_params=_params("arbitrary", "arbitrary"),
        name="moe_ffn",
    )(tile_expert, n_active, src.reshape(n_tiles, 1, tm), src.reshape(n_tiles, 1, tm), u_tiles, wg, wu, wd)


def _combine_ln_kernel(sa_cur, sb_cur, sa_nxt, sb_nxt, ys_hbm, r_ref, x_ref, g1p_ref, lng_ref, lnb_ref,
                       xo_ref, ybuf, sems, *, alpha, tm):
    i = pl.program_id(0)
    slot = i % 2

    def start(sa, sb, s):
        _start_row_gather(lambda r: sa[r], ys_hbm, ybuf.at[s, 0], sems.at[s], tm)
        _start_row_gather(lambda r: sb[r], ys_hbm, ybuf.at[s, 1], sems.at[s], tm)

    @pl.when(i == 0)
    def _():
        start(sa_cur, sb_cur, 0)

    @pl.when(i + 1 < pl.num_programs(0))
    def _():
        start(sa_nxt, sb_nxt, 1 - slot)

    _wait_rows(ybuf.at[slot], sems.at[slot])
    r = r_ref[...]
    y = (r[:, 2:3] * _load_row_tiles(ybuf.at[slot, 0], tm, F32)
         + r[:, 3:4] * _load_row_tiles(ybuf.at[slot, 1], tm, F32))
    xo_ref[...] = _layer_norm(alpha * x_ref[...] + g1p_ref[0] * y, lng_ref[...], lnb_ref[...])


def _combine_ln(ys_tiles, slot_a, slot_b, route, x, g1p, lng, lnb, alpha, tm):
    b, s, d = x.shape
    n = b * s
    sub = d // LANES
    n_tiles = n // tm
    per_seq = s // tm
    cur = pl.BlockSpec((tm,), lambda i: (i,), memory_space=pltpu.SMEM)
    nxt = pl.BlockSpec((tm,), lambda i: (jnp.minimum(i + 1, n_tiles - 1),), memory_space=pltpu.SMEM)
    tile = pl.BlockSpec((tm, d), lambda i: (i, 0))
    vec = pl.BlockSpec((1, d), lambda i: (0, 0))
    out = pl.pallas_call(
        functools.partial(_combine_ln_kernel, alpha=alpha, tm=tm),
        grid=(n_tiles,),
        in_specs=[cur, cur, nxt, nxt, pl.BlockSpec(memory_space=pl.ANY),
                  pl.BlockSpec((tm, LANES), lambda i: (i, 0)), tile,
                  pl.BlockSpec((1, 1, d), lambda i: (i // per_seq, 0, 0)), vec, vec],
        out_specs=tile,
        out_shape=jax.ShapeDtypeStruct((n, d), F32),
        scratch_shapes=[pltpu.VMEM((2, 2, tm * sub, LANES), F32), pltpu.SemaphoreType.DMA((2,))],
        compiler_params=_params("arbitrary"),
        name="combine_ln",
    )(slot_a, slot_b, slot_a, slot_b, ys_tiles, route.reshape(n, LANES), x.reshape(n, d), g1p, lng, lnb)
    return out.reshape(b, s, d)


def _moe_plan(route, tm):
    n = route.shape[0]
    e = N_EXPERTS
    n_tiles = (TOP_K * n + e * (tm - 1)) // tm
    flat_e = route[:, :TOP_K].astype(jnp.int32).reshape(-1)
    onehot = (flat_e[:, None] == jnp.arange(e, dtype=jnp.int32)[None, :]).astype(jnp.int32)
    incl = jnp.cumsum(onehot, axis=0)
    counts = incl[-1]
    padded = (counts + tm - 1) // tm * tm
    ends = jnp.cumsum(padded)
    offs = ends - padded
    slot = jnp.sum(onehot * (offs[None, :] + incl - 1), axis=1)
    n_active = ends[-1] // tm
    tile_id = jnp.minimum(jnp.arange(n_tiles, dtype=jnp.int32), n_active - 1)
    tile_expert = jnp.minimum(jnp.sum((tile_id[:, None] * tm >= ends[None, :]).astype(jnp.int32), axis=1), e - 1)
    by_expert = jnp.argsort(flat_e, stable=True).astype(jnp.int32) // TOP_K
    by_expert = jnp.pad(by_expert, (0, n_tiles * tm - TOP_K * n))
    row_expert = jnp.repeat(tile_expert, tm)
    shift = offs - (jnp.cumsum(counts) - counts)
    src = jnp.zeros((n_tiles * tm,), jnp.int32)
    for k in range(e):
        src = jnp.where(row_expert == k, jnp.roll(by_expert, shift[k]), src)
    return ((slot * ROW_TILE).reshape(n, TOP_K), src * ROW_TILE, tile_expert.astype(jnp.int32),
            n_active.reshape(1).astype(jnp.int32))


def _moe_swiglu_ln(u_tiles, route, wg, wu, wd, x, g1p, lng, lnb, alpha, tm_moe=896, tm_ln=512):
    assert TOP_K == 2
    b, s, d = x.shape
    n = b * s
    f = wg.shape[2]
    nf = 2 if f % (2 * MXU_N) == 0 else 1
    tf = f // nf
    assert tm_moe % nf == 0
    slot, src, tile_expert, n_active = _moe_plan(route.reshape(n, LANES), tm_moe)
    ys = _moe_ffn(u_tiles.reshape(n * (d // LANES), LANES), src, wg, wu, wd, tile_expert, n_active, tm_moe, tf)
    return _combine_ln(ys, slot[:, 0], slot[:, 1], route, x, g1p, lng, lnb, alpha, tm_ln)


def kernel(x, c, positions, ada_w, ada_b, ln_g, ln_b, attn_w_in, attn_w_out, rec_w_in, rec_lb_logits,
           rec_norm_w, rec_w_out, ffn_w_gate, ffn_w_up, ffn_w_down, router_w, moe_w_gate, moe_w_up,
           moe_w_down):
    b, s, d = x.shape
    depth = ada_w.shape[0]
    alpha = (2.0 * depth) ** 0.25
    ts = 512

    mods = _ada_mods(c, ada_w, ada_b).reshape(depth, b, 6, 1, d)
    one_plus = lambda t: 1.0 + t
    pos = positions.reshape(b, s, 1)

    for i in range(depth):
        shift_m, scale_m, gate_m, shift_f, scale_f, gate_f = (mods[i, :, r] for r in range(6))
        j = i // 2
        lng = ln_g[i].reshape(2, 1, d)
        lnb = ln_b[i].reshape(2, 1, d)
        if i % 2 == 0:
            qkv = _qkv_proj(x, one_plus(scale_m), shift_m, pos, attn_w_in[j].astype(BF16), ts)
            o = _moba_attention(qkv, d, n_grp=math.gcd(d // HEAD_DIM, 4),
                                ch=math.gcd(s // MOBA_BLOCK, 4))
            x = _proj_ffn_dense(o, attn_w_out[j].astype(BF16), x, one_plus(gate_m), lng, lnb,
                                one_plus(scale_f), shift_f, ffn_w_gate[j].astype(BF16),
                                ffn_w_up[j].astype(BF16), ffn_w_down[j].astype(BF16), one_plus(gate_f),
                                alpha, ts)
        else:
            qf, key, v, gs, lf = _rec_in(x, one_plus(scale_m), shift_m, rec_w_in[j].astype(BF16),
                                         rec_lb_logits, i, ts)
            o = _hgrn_recurrence(qf, key, v, lf, gs, rec_norm_w[j].reshape(1, d), ts)
            w_hi = router_w[j].astype(BF16)
            w_lo = (router_w[j] - w_hi.astype(F32)).astype(BF16)
            wr = (jnp.zeros((d, 2 * LANES), BF16).at[:, :N_EXPERTS].set(w_hi)
                  .at[:, LANES:LANES + N_EXPERTS].set(w_lo))
            x, u, route = _proj_ln_route(o, rec_w_out[j].astype(BF16), x, one_plus(gate_m), lng[0], lnb[0],
                                         one_plus(scale_f), shift_f, wr, alpha, ts)
            x = _moe_swiglu_ln(u, route, moe_w_gate[j].astype(BF16), moe_w_up[j].astype(BF16),
                               moe_w_down[j].astype(BF16), x, one_plus(gate_f), lng[1], lnb[1], alpha)
    return x
```

```python
import functools
import math

import jax
import jax.numpy as jnp
from jax import lax
from jax.experimental import pallas as pl
from jax.experimental.pallas import tpu as pltpu

HEAD_DIM = 128
ROPE_DIM = HEAD_DIM // 4
ROPE_THETA = 500000.0
MOBA_BLOCK = 256
MOBA_TOPK = 3
HGRN_EXPAND = 128
HGRN_CHUNK = 64
N_EXPERTS = 8
TOP_K = 2
LN_EPS = 1e-5
RMS_EPS = 1e-6

LANES = 128
MXU_N = 256
VMEM_LIMIT = 56 * 1024 * 1024
NEG_BIG = -1e30

F32 = jnp.float32
BF16 = jnp.bfloat16
HIGHEST = lax.Precision.HIGHEST


def _params(*sem):
    return pltpu.CompilerParams(dimension_semantics=sem, vmem_limit_bytes=VMEM_LIMIT)


def _dot(a, b):
    return jnp.dot(a, b, preferred_element_type=F32)


def _dot_nt(a, b, precision=None):
    return lax.dot_general(a, b, (((1,), (1,)), ((), ())), precision=precision,
                           preferred_element_type=F32)


def _silu(x):
    return x * jax.nn.sigmoid(x)


def _layer_norm(z, g, b):
    mu = jnp.mean(z, axis=-1, keepdims=True)
    d = z - mu
    var = jnp.mean(d * d, axis=-1, keepdims=True)
    return d * lax.rsqrt(var + LN_EPS) * g + b


def _ada_kernel(c_ref, w_ref, b_ref, o_ref):
    a = _silu(c_ref[...])
    o_ref[0] = jnp.dot(a, w_ref[0], precision=HIGHEST, preferred_element_type=F32) + b_ref[0]


def _ada_mods(c, ada_w, ada_b):
    depth, d, m = ada_w.shape
    b = c.shape[0]
    tn = m // 4
    return pl.pallas_call(
        _ada_kernel,
        grid=(depth, m // tn),
        in_specs=[pl.BlockSpec((b, d), lambda l, j: (0, 0)),
                  pl.BlockSpec((1, d, tn), lambda l, j: (l, 0, j)),
                  pl.BlockSpec((1, 1, tn), lambda l, j: (l, 0, j))],
        out_specs=pl.BlockSpec((1, b, tn), lambda l, j: (l, 0, j)),
        out_shape=jax.ShapeDtypeStruct((depth, b, m), F32),
        compiler_params=_params("arbitrary", "arbitrary"),
        name="ada_mods",
    )(c, ada_w, ada_b.reshape(depth, 1, m))


def _qkv_kernel(x_ref, sc_ref, sh_ref, pos_ref, w_ref, o_ref, trig_scr, *, n_rot):
    u = (x_ref[0] * sc_ref[0] + sh_ref[0]).astype(BF16)
    half = ROPE_DIM // 2
    pack = LANES // ROPE_DIM
    rows = x_ref.shape[1] // pack
    lane = lax.broadcasted_iota(jnp.int32, (1, LANES), 1)
    inv = jnp.exp(-math.log(ROPE_THETA) * (lane % half).astype(F32) * (2.0 / ROPE_DIM))
    pos = jnp.zeros((rows, LANES), F32)
    for m in range(pack):
        pos = jnp.where(lane // ROPE_DIM == m, pos_ref[0, pl.ds(m, rows, stride=pack), :].astype(F32), pos)
    ang = pos * inv
    for t, trig in enumerate((jnp.cos(ang), jnp.sin(ang))):
        for m in range(pack):
            moved = trig if m == 0 else pltpu.roll(trig, LANES - ROPE_DIM * m, 1)
            trig_scr[t, pl.ds(m, rows, stride=pack), :] = moved
    cos_t = jnp.where(lane < ROPE_DIM, trig_scr[0], 1.0)
    sin_t = jnp.where(lane < ROPE_DIM, trig_scr[1], 0.0)
    sin_lo = jnp.where(lane < half, -sin_t, 0.0)
    sin_hi = jnp.where((lane >= half) & (lane < ROPE_DIM), sin_t, 0.0)
    n_cols = w_ref.shape[1]
    for c in range(n_cols // MXU_N):
        y = _dot(u, w_ref[:, c * MXU_N:(c + 1) * MXU_N])
        for s in range(MXU_N // HEAD_DIM):
            col = c * MXU_N + s * HEAD_DIM
            t = y[:, s * HEAD_DIM:(s + 1) * HEAD_DIM]
            if col < n_rot:
                t = (t * cos_t + pltpu.roll(t, HEAD_DIM - half, 1) * sin_lo
                     + pltpu.roll(t, half, 1) * sin_hi)
            if col < n_rot // 2:
                t = t * (HEAD_DIM ** -0.5 * math.log2(math.e))
            o_ref[0, :, col:col + HEAD_DIM] = t.astype(BF16)


def _qkv_proj(x, sc, sh, pos, w, ts):
    b, s, d = x.shape
    n = w.shape[1]
    return pl.pallas_call(
        functools.partial(_qkv_kernel, n_rot=2 * d),
        grid=(b, s // ts),
        in_specs=[pl.BlockSpec((1, ts, d), lambda i, j: (i, j, 0)),
                  pl.BlockSpec((1, 1, d), lambda i, j: (i, 0, 0)),
                  pl.BlockSpec((1, 1, d), lambda i, j: (i, 0, 0)),
                  pl.BlockSpec((1, ts, 1), lambda i, j: (i, j, 0)),
                  pl.BlockSpec((d, n), lambda i, j: (0, 0))],
        out_specs=pl.BlockSpec((1, ts, n), lambda i, j: (i, j, 0)),
        out_shape=jax.ShapeDtypeStruct((b, s, n), BF16),
        scratch_shapes=[pltpu.VMEM((2, ts, LANES), F32)],
        compiler_params=_params("arbitrary", "arbitrary"),
        name="qkv_proj",
    )(x, sc, sh, pos, w)


def _moba_kernel(q_ref, k_ref, v_ref, o_ref, kmean_scr, vt_scr, bias_scr, m_scr, l_scr, acc_scr,
                 *, nb, n_grp, ch):
    blk = MOBA_BLOCK
    hd = HEAD_DIM
    cw = ch * blk
    i = pl.program_id(2)

    @pl.when(i == 0)
    def _():
        for hh in range(n_grp):
            means = []
            for n in range(nb):
                kb = k_ref[0, n * blk:(n + 1) * blk, hh * hd:(hh + 1) * hd].astype(F32)
                means.append(jnp.mean(kb, axis=0, keepdims=True))
                vb = v_ref[0, n * blk:(n + 1) * blk, hh * hd:(hh + 1) * hd]
                vt_scr[hh, n // ch, :, (n % ch) * blk:(n % ch + 1) * blk] = vb.astype(F32).T.astype(BF16)
            km = jnp.concatenate(means, axis=0)
            km_hi = km.astype(BF16)
            km_lo = (km - km_hi.astype(F32)).astype(BF16)
            kmean_scr[hh] = jnp.concatenate([km_hi, km_lo], axis=0)

    def sweep(c, n_own):
        own = n_own > 0
        n_blk = n_own if own else ch
        k0 = pl.multiple_of(c * cw, cw)
        heads = [slice(hh * hd, (hh + 1) * hd) for hh in range(n_grp)]
        scores = [_dot_nt(k_ref[0, pl.ds(k0, n_blk * blk), cols], q_ref[0, :, cols])
                  for cols in heads]
        m_news, l_news, probs = [], [], []
        for hh, s in enumerate(scores):
            parts = [s[t * blk:(t + 1) * blk] + bias_scr[hh, pl.ds(c * ch + t, 1), :]
                     for t in range(n_blk - 1 if own else n_blk)]
            if own:
                s_own = s[(n_blk - 1) * blk:]
                kpos = lax.broadcasted_iota(jnp.int32, s_own.shape, 0)
                qpos = lax.broadcasted_iota(jnp.int32, s_own.shape, 1)
                parts.append(jnp.where(kpos <= qpos, s_own, NEG_BIG))
            m_blk = parts[0]
            for t in range(1, n_blk):
                m_blk = jnp.maximum(m_blk, parts[t])
            m_new = jnp.max(m_blk, axis=0, keepdims=True)
            if not own:
                m_new = jnp.maximum(m_new, m_scr[hh])
            ps = [jnp.exp2(part - m_new) for part in parts]
            l_new = ps[0]
            for t in range(1, n_blk):
                l_new = l_new + ps[t]
            m_news.append(m_new)
            l_news.append(jnp.sum(l_new, axis=0, keepdims=True))
            probs.append(jnp.concatenate([p.astype(BF16) for p in ps], axis=0) if n_blk > 1
                         else ps[0].astype(BF16))
        pvs = [_dot(vt_scr[hh, c, :, 0:n_blk * blk], probs[hh]) for hh in range(n_grp)]
        for hh in range(n_grp):
            if own:
                l_scr[hh] = l_news[hh]
                acc_scr[hh] = pvs[hh]
            else:
                alpha = jnp.exp2(m_scr[hh] - m_news[hh])
                l_scr[hh] = alpha * l_scr[hh] + l_news[hh]
                acc_scr[hh] = alpha * acc_scr[hh] + pvs[hh]
            m_scr[hh] = m_news[hh]

    c_own = i // ch
    for hh in range(n_grp):
        q = q_ref[0, :, hh * hd:(hh + 1) * hd]
        gate2 = _dot_nt(kmean_scr[hh], q)
        gate = gate2[:nb] + gate2[nb:]
        blk_id = lax.broadcasted_iota(jnp.int32, gate.shape, 0)
        past = blk_id < i
        g = jnp.where(past, gate, -jnp.inf)
        rank = jnp.zeros(gate.shape, jnp.int32)
        for m in range(nb):
            gm = g[m:m + 1, :]
            beats = jnp.where(gm > g, 1, jnp.where(gm == g, jnp.where(blk_id > m, 1, 0), 0))
            rank = rank + beats
        bias_scr[hh] = jnp.where(past, jnp.where(rank < MOBA_TOPK, 0.0, NEG_BIG), NEG_BIG)

    for r in range(ch):
        @pl.when(i % ch == r)
        def _():
            sweep(c_own, r + 1)

    def body(c, carry):
        sweep(c, 0)
        return carry

    lax.fori_loop(0, c_own, body, 0)
    for hh in range(n_grp):
        o_ref[0, :, hh * hd:(hh + 1) * hd] = (acc_scr[hh] / l_scr[hh]).T.astype(BF16)


def _moba_attention(qkv, d, n_grp, ch):
    b, s, _ = qkv.shape
    h = d // HEAD_DIM
    blk = MOBA_BLOCK
    nb = s // blk
    gw = n_grp * HEAD_DIM
    ng = h // n_grp
    return pl.pallas_call(
        functools.partial(_moba_kernel, nb=nb, n_grp=n_grp, ch=ch),
        grid=(b, ng, nb),
        in_specs=[pl.BlockSpec((1, blk, gw), lambda bi, hi, i: (bi, i, hi)),
                  pl.BlockSpec((1, s, gw), lambda bi, hi, i: (bi, 0, ng + hi)),
                  pl.BlockSpec((1, s, gw), lambda bi, hi, i: (bi, 0, 2 * ng + hi))],
        out_specs=pl.BlockSpec((1, blk, gw), lambda bi, hi, i: (bi, i, hi)),
        out_shape=jax.ShapeDtypeStruct((b, s, d), BF16),
        scratch_shapes=[pltpu.VMEM((n_grp, 2 * nb, HEAD_DIM), BF16),
                        pltpu.VMEM((n_grp, nb // ch, HEAD_DIM, ch * blk), BF16),
                        pltpu.VMEM((n_grp, nb, blk), F32),
                        pltpu.VMEM((n_grp, 1, blk), F32),
                        pltpu.VMEM((n_grp, 1, blk), F32),
                        pltpu.VMEM((n_grp, HEAD_DIM, blk), F32)],
        compiler_params=_params("arbitrary", "arbitrary", "arbitrary"),
        name="moba_attention",
    )(qkv, qkv, qkv)


def _store_row_tiles(ref, val):
    rows, d = val.shape
    sub = d // LANES
    for c in range(sub):
        ref[pl.ds(c, rows, stride=sub), :] = val[:, c * LANES:(c + 1) * LANES]


def _load_row_tiles(ref, rows, dtype):
    sub = ref.shape[0] // rows
    return jnp.concatenate([ref[pl.ds(c, rows, stride=sub), :].astype(dtype) for c in range(sub)], axis=1)


def _route_top2(u, wr_ref):
    u_hi = u.astype(BF16)
    u_lo = (u - u_hi.astype(F32)).astype(BF16)
    both = _dot(u_hi, wr_ref[...])
    logits = both[:, :LANES] + both[:, LANES:] + _dot(u_lo, wr_ref[:, :LANES])
    lane = lax.broadcasted_iota(jnp.int32, logits.shape, 1)
    logits = jnp.where(lane < N_EXPERTS, logits, -jnp.inf)
    m1 = jnp.max(logits, axis=-1, keepdims=True)
    i1 = jnp.min(jnp.where(logits == m1, lane, LANES), axis=-1, keepdims=True)
    rest = jnp.where(lane == i1, -jnp.inf, logits)
    m2 = jnp.max(rest, axis=-1, keepdims=True)
    i2 = jnp.min(jnp.where(rest == m2, lane, LANES), axis=-1, keepdims=True)
    e2 = jnp.exp(m2 - m1)
    w1 = 1.0 / (1.0 + e2)
    w2 = e2 / (1.0 + e2)
    return jnp.where(lane == 0, i1.astype(F32),
                     jnp.where(lane == 1, i2.astype(F32),
                               jnp.where(lane == 2, w1, jnp.where(lane == 3, w2, 0.0))))


def _proj_ln_route_kernel(a_ref, w_ref, x_ref, g1p_ref, lng_ref, lnb_ref, sc_ref, sh_ref, wr_ref,
                          xo_ref, uo_ref, ro_ref, *, alpha):
    y = _dot(a_ref[0], w_ref[...])
    xn = _layer_norm(alpha * x_ref[0] + g1p_ref[0] * y, lng_ref[...], lnb_ref[...])
    xo_ref[0] = xn
    u = xn * sc_ref[0] + sh_ref[0]
    _store_row_tiles(uo_ref.at[0], u)
    ro_ref[0] = _route_top2(u, wr_ref)


def _proj_ln_route(a, w, x, g1p, lng, lnb, sc, sh, wr, alpha, tm):
    b, s, d = x.shape
    sub = d // LANES
    tile = pl.BlockSpec((1, tm, d), lambda i, j: (i, j, 0))
    per_b = pl.BlockSpec((1, 1, d), lambda i, j: (i, 0, 0))
    vec = pl.BlockSpec((1, d), lambda i, j: (0, 0))
    return pl.pallas_call(
        functools.partial(_proj_ln_route_kernel, alpha=alpha),
        grid=(b, s // tm),
        in_specs=[tile, pl.BlockSpec((d, d), lambda i, j: (0, 0)), tile, per_b, vec, vec, per_b, per_b,
                  pl.BlockSpec((d, 2 * LANES), lambda i, j: (0, 0))],
        out_specs=[tile, pl.BlockSpec((1, tm * sub, LANES), lambda i, j: (i, j, 0)),
                   pl.BlockSpec((1, tm, LANES), lambda i, j: (i, j, 0))],
        out_shape=[jax.ShapeDtypeStruct((b, s, d), F32), jax.ShapeDtypeStruct((b, s * sub, LANES), F32),
                   jax.ShapeDtypeStruct((b, s, LANES), F32)],
        compiler_params=_params("arbitrary", "arbitrary"),
        name="proj_ln_route",
    )(a, w, x, g1p, lng, lnb, sc, sh, wr)


def _proj_ffn_kernel(a_ref, wo_ref, x_ref, g1m_ref, lng_ref, lnb_ref, sc_ref, sh_ref, wg_ref, wu_ref, wd_ref,
                     g1f_ref, xo_ref, h_scr, *, alpha, tf):
    x1 = _layer_norm(alpha * x_ref[0] + g1m_ref[0] * _dot(a_ref[0], wo_ref[...]), lng_ref[0], lnb_ref[0])
    u = (x1 * sc_ref[0] + sh_ref[0]).astype(BF16)
    f = wg_ref.shape[1]
    for j in range(f // tf):
        cols = slice(j * tf, (j + 1) * tf)
        hj = _silu(_dot(u, wg_ref[:, cols])) * _dot(u, wu_ref[:, cols])
        h_scr[:, cols] = hj.astype(BF16)
    y = _dot(h_scr[...], wd_ref[...])
    xo_ref[0] = _layer_norm(alpha * x1 + g1f_ref[0] * y, lng_ref[1], lnb_ref[1])


def _proj_ffn_dense(a, wo, x, g1m, lng, lnb, sc, sh, wg, wu, wd, g1f, alpha, tm):
    b, s, d = x.shape
    f = wg.shape[1]
    tile = pl.BlockSpec((1, tm, d), lambda i, j: (i, j, 0))
    per_b = pl.BlockSpec((1, 1, d), lambda i, j: (i, 0, 0))
    ln_pair = pl.BlockSpec((2, 1, d), lambda i, j: (0, 0, 0))
    resident = dict(pipeline_mode=pl.Buffered(1))
    return pl.pallas_call(
        functools.partial(_proj_ffn_kernel, alpha=alpha, tf=MXU_N),
        grid=(b, s // tm),
        in_specs=[tile, pl.BlockSpec((d, d), lambda i, j: (0, 0), **resident), tile, per_b, ln_pair, ln_pair,
                  per_b, per_b,
                  pl.BlockSpec((d, f), lambda i, j: (0, 0), **resident),
                  pl.BlockSpec((d, f), lambda i, j: (0, 0), **resident),
                  pl.BlockSpec((f, d), lambda i, j: (0, 0), **resident),
                  per_b],
        out_specs=tile,
        out_shape=jax.ShapeDtypeStruct((b, s, d), F32),
        scratch_shapes=[pltpu.VMEM((tm, f), BF16)],
        compiler_params=_params("arbitrary", "arbitrary"),
        name="proj_ffn_dense",
    )(a, wo, x, g1m, lng, lnb, sc, sh, wg, wu, wd, g1f)


def _rec_in_kernel(x_ref, sc_ref, sh_ref, w_ref, lbl_ref, qf_ref, key_ref, v_ref, gs_ref, lf_ref,
                   *, layer_idx):
    u = (x_ref[0] * sc_ref[0] + sh_ref[0]).astype(BF16)
    d = x_ref.shape[2]
    lbl = lbl_ref[...]
    e = jnp.exp(lbl - jnp.max(lbl, axis=0, keepdims=True))
    sm = e / jnp.sum(e, axis=0, keepdims=True)
    lb = jnp.zeros((1, d), F32)
    for r in range(1, layer_idx + 1):
        lb = lb + sm[r:r + 1, :]
    for c in range(d // MXU_N):
        cols = slice(c * MXU_N, (c + 1) * MXU_N)
        q = _dot(u, w_ref[:, c * MXU_N:(c + 1) * MXU_N])
        qf_ref[0, :, cols] = _silu(q).astype(BF16)
        f = _dot(u, w_ref[:, d + c * MXU_N:d + (c + 1) * MXU_N])
        lbc = lb[:, cols]
        f_gate = lbc + (1.0 - lbc) * jax.nn.sigmoid(f)
        lf_ref[0, :, cols] = jnp.log(f_gate)
        key_ref[0, :, cols] = (1.0 - f_gate).astype(BF16)
        v = _dot(u, w_ref[:, 2 * d + c * MXU_N:2 * d + (c + 1) * MXU_N])
        v_ref[0, :, cols] = v.astype(BF16)
        g = _dot(u, w_ref[:, 3 * d + c * MXU_N:3 * d + (c + 1) * MXU_N])
        gs_ref[0, :, cols] = _silu(g).astype(BF16)


def _rec_in(x, sc, sh, w, lb_logits, layer_idx, ts):
    b, s, d = x.shape
    depth = lb_logits.shape[0]
    tile = pl.BlockSpec((1, ts, d), lambda i, j: (i, j, 0))
    per_b = pl.BlockSpec((1, 1, d), lambda i, j: (i, 0, 0))
    bf = jax.ShapeDtypeStruct((b, s, d), BF16)
    return pl.pallas_call(
        functools.partial(_rec_in_kernel, layer_idx=layer_idx),
        grid=(b, s // ts),
        in_specs=[tile, per_b, per_b,
                  pl.BlockSpec((d, 4 * d), lambda i, j: (0, 0)),
                  pl.BlockSpec((depth, d), lambda i, j: (0, 0))],
        out_specs=[tile] * 5,
        out_shape=[bf, bf, bf, bf, jax.ShapeDtypeStruct((b, s, d), F32)],
        compiler_params=_params("arbitrary", "arbitrary"),
        name="rec_in",
    )(x, sc, sh, w, lb_logits)


def _hgrn_kernel(qf_ref, key_ref, v_ref, lf_ref, gs_ref, nw_ref, o_ref, st_scr, *, n_heads):
    cs = HGRN_CHUNK
    dk = HGRN_EXPAND

    @pl.when(pl.program_id(1) == 0)
    def _():
        st_scr[...] = jnp.zeros(st_scr.shape, F32)

    row = lax.broadcasted_iota(jnp.int32, (cs, cs), 0)
    col = lax.broadcasted_iota(jnp.int32, (cs, cs), 1)
    causal = row >= col
    tri = jnp.where(causal, 1.0, 0.0).astype(BF16)
    n_chunks = qf_ref.shape[1] // cs

    def chunk(c, carry):
        rows = pl.ds(pl.multiple_of(c * cs, cs), cs)
        lf = lf_ref[0, rows, :]
        hi = lf.astype(BF16)
        r1 = lf - hi.astype(F32)
        mid = r1.astype(BF16)
        lo = (r1 - mid.astype(F32)).astype(BF16)
        g = _dot(tri, hi) + _dot(tri, mid) + _dot(tri, lo)
        g_last = g[cs - 1:cs, :]
        qf = qf_ref[0, rows, :].astype(F32)
        key = key_ref[0, rows, :].astype(F32)
        v = v_ref[0, rows, :]
        q_dec = (qf * jnp.exp(g)).astype(BF16)
        k_dec = (key * jnp.exp(-g)).astype(BF16)
        k_state = (key * jnp.exp(g_last - g)).astype(BF16)
        decay = jnp.exp(g_last)
        heads = [slice(h * dk, (h + 1) * dk) for h in range(n_heads)]
        a = [jnp.where(causal, _dot_nt(q_dec[:, c], k_dec[:, c]), 0.0).astype(BF16) for c in heads]
        st = [st_scr[h] for h in range(n_heads)]
        o = [_dot(a[h], v[:, c]) + _dot_nt(q_dec[:, c], st[h].astype(BF16))
             for h, c in enumerate(heads)]
        for h, c in enumerate(heads):
            v_t = v[:, c].astype(F32).T.astype(BF16)
            st_scr[h] = st[h] * decay[:, c] + _dot(v_t, k_state[:, c])
        on = jnp.concatenate(
            [oh * lax.rsqrt(jnp.mean(oh * oh, axis=-1, keepdims=True) + RMS_EPS) for oh in o], axis=1)
        o_ref[0, rows, :] = (on * nw_ref[...] * gs_ref[0, rows, :].astype(F32)).astype(BF16)
        return carry

    lax.fori_loop(0, n_chunks, chunk, 0, unroll=2)


def _hgrn_recurrence(qf, key, v, lf, gs, norm_w, ts):
    b, s, d = qf.shape
    h = d // HGRN_EXPAND
    tile = pl.BlockSpec((1, ts, d), lambda i, j: (i, j, 0))
    return pl.pallas_call(
        functools.partial(_hgrn_kernel, n_heads=h),
        grid=(b, s // ts),
        in_specs=[tile, tile, tile, tile, tile, pl.BlockSpec((1, d), lambda i, j: (0, 0))],
        out_specs=tile,
        out_shape=jax.ShapeDtypeStruct((b, s, d), BF16),
        scratch_shapes=[pltpu.VMEM((h, HGRN_EXPAND, HGRN_EXPAND), F32)],
        compiler_params=_params("arbitrary", "arbitrary"),
        name="hgrn_recurrence",
    )(qf, key, v, lf, gs, norm_w)


GATHER_UNROLL = 8


ROW_TILE = 8


def _start_row(idx_at, src_hbm, dst, sem, r):
    first = pl.multiple_of(idx_at(r), ROW_TILE)
    pltpu.make_async_copy(src_hbm.at[pl.ds(first, ROW_TILE), :],
                          dst.at[pl.ds(r * ROW_TILE, ROW_TILE), :], sem).start()


def _start_row_gather(idx_at, src_hbm, dst, sem, n_rows):
    def one(r, carry):
        _start_row(idx_at, src_hbm, dst, sem, r)
        return carry

    lax.fori_loop(0, n_rows, one, 0, unroll=GATHER_UNROLL)


def _wait_rows(buf, sem):
    pltpu.make_async_copy(buf, buf, sem).wait()


MOE_AHEAD = 2
MOE_XBUFS = MOE_AHEAD + 1


def _moe_ffn_kernel(te_ref, na_ref, src_t0_ref, src_t1_ref, src_ahead_ref, u_hbm, wg_ref, wu_ref, wd_ref,
                    o_ref, xbuf, xb_scr, h_scr, acc_scr, sems, *, tm, nf, n_tiles):
    i = pl.program_id(0)
    j = pl.program_id(1)
    rows_per_step = tm // nf
    n_active = na_ref[0]
    active = i < n_active
    slot = i % MOE_XBUFS
    ahead = (i + MOE_AHEAD) % MOE_XBUFS

    assert MOE_AHEAD == 2

    @pl.when((j == 0) & (i == 0))
    def _():
        _start_row_gather(lambda r: src_t0_ref[0, 0, r], u_hbm, xbuf.at[0], sems.at[0], tm)
        _start_row_gather(lambda r: src_t1_ref[0, 0, r], u_hbm, xbuf.at[1], sems.at[1], tm)

    @pl.when((j == 0) & (i <= n_active + 1))
    def _():
        _wait_rows(xbuf.at[slot], sems.at[slot])

    @pl.when(active & (j == 0))
    def _():
        xb_scr[...] = _load_row_tiles(xbuf.at[slot], tm, BF16)

    @pl.when(active)
    def _():
        for k in range(rows_per_step):
            _start_row(lambda r: src_ahead_ref[0, 0, r], u_hbm, xbuf.at[ahead], sems.at[ahead],
                       j * rows_per_step + k)
        x = xb_scr[...]
        for c in range(wg_ref.shape[2] // MXU_N):
            cols = slice(c * MXU_N, (c + 1) * MXU_N)
            hc = _silu(_dot(x, wg_ref[0, :, cols])) * _dot(x, wu_ref[0, :, cols])
            h_scr[:, cols] = hc.astype(BF16)
        y = _dot(h_scr[...], wd_ref[0])
        if nf == 1:
            _store_row_tiles(o_ref, y)
        else:
            @pl.when(j == 0)
            def _():
                acc_scr[...] = y

            @pl.when((j > 0) & (j < nf - 1))
            def _():
                acc_scr[...] += y

            @pl.when(j == nf - 1)
            def _():
                _store_row_tiles(o_ref, acc_scr[...] + y)

    @pl.when(jnp.logical_not(active) & (j == nf - 1) & (i < n_tiles))
    def _():
        o_ref[...] = jnp.zeros(o_ref.shape, F32)


def _moe_ffn(u_tiles, src, wg, wu, wd, tile_expert, n_active, tm, tf):
    d = wg.shape[1]
    sub = d // LANES
    assert sub == ROW_TILE
    n_tiles = src.shape[0] // tm
    assert tile_expert.shape[0] == n_tiles + MOE_AHEAD
    f = wg.shape[2]
    nf = f // tf

    def frozen(j, i, na):
        return jnp.where(i < na[0], j, nf - 1)

    def idx_block(tile_of):
        return pl.BlockSpec((1, 1, tm), lambda i, j, te, na: (jnp.minimum(tile_of(i), n_tiles - 1), 0, 0),
                            memory_space=pltpu.SMEM)

    grid_spec = pltpu.PrefetchScalarGridSpec(
        num_scalar_prefetch=2,
        grid=(n_tiles + MOE_AHEAD, nf),
        in_specs=[idx_block(lambda i: 0), idx_block(lambda i: 1), idx_block(lambda i: i + MOE_AHEAD),
                  pl.BlockSpec(memory_space=pl.ANY),
                  pl.BlockSpec((1, d, tf), lambda i, j, te, na: (te[i], 0, frozen(j, i, na))),
                  pl.BlockSpec((1, d, tf), lambda i, j, te, na: (te[i], 0, frozen(j, i, na))),
                  pl.BlockSpec((1, tf, d), lambda i, j, te, na: (te[i], frozen(j, i, na), 0))],
        out_specs=pl.BlockSpec((tm * sub, LANES), lambda i, j, te, na: (jnp.minimum(i, n_tiles - 1), 0)),
        scratch_shapes=[pltpu.VMEM((MOE_XBUFS, tm * sub, LANES), F32),
                        pltpu.VMEM((tm, d), BF16),
                        pltpu.VMEM((tm, tf), BF16),
                        pltpu.VMEM((tm, d), F32),
                        pltpu.SemaphoreType.DMA((MOE_XBUFS,))],
    )
    src3 = src.reshape(n_tiles, 1, tm)
    return pl.pallas_call(
        functools.partial(_moe_ffn_kernel, tm=tm, nf=nf, n_tiles=n_tiles),
        grid_spec=grid_spec,
        out_shape=jax.ShapeDtypeStruct((n_tiles * tm * sub, LANES), F32),
        compiler_params=_params("arbitrary", "arbitrary"),
        name="moe_ffn",
    )(tile_expert, n_active, src3, src3, src3, u_tiles, wg, wu, wd)


def _combine_ln_kernel(sa_cur, sb_cur, sa_nxt, sb_nxt, ys_hbm, r_ref, x_ref, g1p_ref, lng_ref, lnb_ref,
                       xo_ref, ybuf, sems, *, alpha, tm):
    i = pl.program_id(0)
    slot = i % 2

    def start(sa, sb, s):
        _start_row_gather(lambda r: sa[r], ys_hbm, ybuf.at[s, 0], sems.at[s], tm)
        _start_row_gather(lambda r: sb[r], ys_hbm, ybuf.at[s, 1], sems.at[s], tm)

    @pl.when(i == 0)
    def _():
        start(sa_cur, sb_cur, 0)

    @pl.when(i + 1 < pl.num_programs(0))
    def _():
        start(sa_nxt, sb_nxt, 1 - slot)

    _wait_rows(ybuf.at[slot], sems.at[slot])
    r = r_ref[...]
    y = (r[:, 2:3] * _load_row_tiles(ybuf.at[slot, 0], tm, F32)
         + r[:, 3:4] * _load_row_tiles(ybuf.at[slot, 1], tm, F32))
    xo_ref[...] = _layer_norm(alpha * x_ref[...] + g1p_ref[0] * y, lng_ref[...], lnb_ref[...])


def _combine_ln(ys_tiles, slot_a, slot_b, route, x, g1p, lng, lnb, alpha, tm):
    b, s, d = x.shape
    n = b * s
    sub = d // LANES
    n_tiles = n // tm
    per_seq = s // tm
    cur = pl.BlockSpec((tm,), lambda i: (i,), memory_space=pltpu.SMEM)
    nxt = pl.BlockSpec((tm,), lambda i: (jnp.minimum(i + 1, n_tiles - 1),), memory_space=pltpu.SMEM)
    tile = pl.BlockSpec((tm, d), lambda i: (i, 0))
    vec = pl.BlockSpec((1, d), lambda i: (0, 0))
    out = pl.pallas_call(
        functools.partial(_combine_ln_kernel, alpha=alpha, tm=tm),
        grid=(n_tiles,),
        in_specs=[cur, cur, nxt, nxt, pl.BlockSpec(memory_space=pl.ANY),
                  pl.BlockSpec((tm, LANES), lambda i: (i, 0)), tile,
                  pl.BlockSpec((1, 1, d), lambda i: (i // per_seq, 0, 0)), vec, vec],
        out_specs=tile,
        out_shape=jax.ShapeDtypeStruct((n, d), F32),
        scratch_shapes=[pltpu.VMEM((2, 2, tm * sub, LANES), F32), pltpu.SemaphoreType.DMA((2,))],
        compiler_params=_params("arbitrary"),
        name="combine_ln",
    )(slot_a, slot_b, slot_a, slot_b, ys_tiles, route.reshape(n, LANES), x.reshape(n, d), g1p, lng, lnb)
    return out.reshape(b, s, d)


def _moe_plan(route, tm):
    n = route.shape[0]
    e = N_EXPERTS
    n_tiles = (TOP_K * n + e * (tm - 1)) // tm
    flat_e = route[:, :TOP_K].astype(jnp.int32).reshape(-1)
    onehot = (flat_e[:, None] == jnp.arange(e, dtype=jnp.int32)[None, :]).astype(jnp.int32)
    incl = jnp.cumsum(onehot, axis=0)
    counts = incl[-1]
    padded = (counts + tm - 1) // tm * tm
    ends = jnp.cumsum(padded)
    offs = ends - padded
    slot = jnp.sum(onehot * (offs[None, :] + incl - 1), axis=1)
    n_active = ends[-1] // tm
    tile_id = jnp.minimum(jnp.arange(n_tiles + MOE_AHEAD, dtype=jnp.int32), n_active - 1)
    tile_expert = jnp.minimum(jnp.sum((tile_id[:, None] * tm >= ends[None, :]).astype(jnp.int32), axis=1), e - 1)
    by_expert = jnp.argsort(flat_e, stable=True).astype(jnp.int32) // TOP_K
    by_expert = jnp.pad(by_expert, (0, n_tiles * tm - TOP_K * n))
    row_expert = jnp.repeat(tile_expert[:n_tiles], tm)
    shift = offs - (jnp.cumsum(counts) - counts)
    src = jnp.zeros((n_tiles * tm,), jnp.int32)
    for k in range(e):
        src = jnp.where(row_expert == k, jnp.roll(by_expert, shift[k]), src)
    return ((slot * ROW_TILE).reshape(n, TOP_K), src * ROW_TILE, tile_expert.astype(jnp.int32),
            n_active.reshape(1).astype(jnp.int32))


def _moe_swiglu_ln(u_tiles, route, wg, wu, wd, x, g1p, lng, lnb, alpha, tm_moe=896, tm_ln=512):
    assert TOP_K == 2
    b, s, d = x.shape
    n = b * s
    f = wg.shape[2]
    nf = 2 if f % (2 * MXU_N) == 0 else 1
    tf = f // nf
    assert tm_moe % nf == 0
    slot, src, tile_expert, n_active = _moe_plan(route.reshape(n, LANES), tm_moe)
    ys = _moe_ffn(u_tiles.reshape(n * (d // LANES), LANES), src, wg, wu, wd, tile_expert, n_active, tm_moe, tf)
    return _combine_ln(ys, slot[:, 0], slot[:, 1], route, x, g1p, lng, lnb, alpha, tm_ln)


def kernel(x, c, positions, ada_w, ada_b, ln_g, ln_b, attn_w_in, attn_w_out, rec_w_in, rec_lb_logits,
           rec_norm_w, rec_w_out, ffn_w_gate, ffn_w_up, ffn_w_down, router_w, moe_w_gate, moe_w_up,
           moe_w_down):
    b, s, d = x.shape
    depth = ada_w.shape[0]
    alpha = (2.0 * depth) ** 0.25
    ts = 512

    mods = _ada_mods(c, ada_w, ada_b).reshape(depth, b, 6, 1, d)
    one_plus = lambda t: 1.0 + t
    pos = positions.reshape(b, s, 1)

    for i in range(depth):
        shift_m, scale_m, gate_m, shift_f, scale_f, gate_f = (mods[i, :, r] for r in range(6))
        j = i // 2
        lng = ln_g[i].reshape(2, 1, d)
        lnb = ln_b[i].reshape(2, 1, d)
        if i % 2 == 0:
            qkv = _qkv_proj(x, one_plus(scale_m), shift_m, pos, attn_w_in[j].astype(BF16), ts)
            o = _moba_attention(qkv, d, n_grp=math.gcd(d // HEAD_DIM, 4),
                                ch=math.gcd(s // MOBA_BLOCK, 4))
            x = _proj_ffn_dense(o, attn_w_out[j].astype(BF16), x, one_plus(gate_m), lng, lnb,
                                one_plus(scale_f), shift_f, ffn_w_gate[j].astype(BF16),
                                ffn_w_up[j].astype(BF16), ffn_w_down[j].astype(BF16), one_plus(gate_f),
                                alpha, ts)
        else:
            qf, key, v, gs, lf = _rec_in(x, one_plus(scale_m), shift_m, rec_w_in[j].astype(BF16),
                                         rec_lb_logits, i, ts)
            o = _hgrn_recurrence(qf, key, v, lf, gs, rec_norm_w[j].reshape(1, d), ts)
            w_hi = router_w[j].astype(BF16)
            w_lo = (router_w[j] - w_hi.astype(F32)).astype(BF16)
            wr = (jnp.zeros((d, 2 * LANES), BF16).at[:, :N_EXPERTS].set(w_hi)
                  .at[:, LANES:LANES + N_EXPERTS].set(w_lo))
            x, u, route = _proj_ln_route(o, rec_w_out[j].astype(BF16), x, one_plus(gate_m), lng[0], lnb[0],
                                         one_plus(scale_f), shift_f, wr, alpha, ts)
            x = _moe_swiglu_ln(u, route, moe_w_gate[j].astype(BF16), moe_w_up[j].astype(BF16),
                               moe_w_down[j].astype(BF16), x, one_plus(gate_f), lng[1], lnb[1], alpha)
    return x
```

```python
import functools
import math

import jax
import jax.numpy as jnp
from jax import lax
from jax.experimental import pallas as pl
from jax.experimental.pallas import tpu as pltpu

HEAD_DIM = 128
ROPE_DIM = HEAD_DIM // 4
ROPE_THETA = 500000.0
MOBA_BLOCK = 256
MOBA_TOPK = 3
HGRN_EXPAND = 128
HGRN_CHUNK = 64
N_EXPERTS = 8
TOP_K = 2
LN_EPS = 1e-5
RMS_EPS = 1e-6

LANES = 128
MXU_N = 256
VMEM_LIMIT = 56 * 1024 * 1024
NEG_BIG = -1e30

F32 = jnp.float32
BF16 = jnp.bfloat16
HIGHEST = lax.Precision.HIGHEST


def _params(*sem):
    return pltpu.CompilerParams(dimension_semantics=sem, vmem_limit_bytes=VMEM_LIMIT)


def _dot(a, b):
    return jnp.dot(a, b, preferred_element_type=F32)


def _dot_nt(a, b, precision=None):
    return lax.dot_general(a, b, (((1,), (1,)), ((), ())), precision=precision,
                           preferred_element_type=F32)


def _silu(x):
    return x * jax.nn.sigmoid(x)


def _layer_norm(z, g, b):
    mu = jnp.mean(z, axis=-1, keepdims=True)
    d = z - mu
    var = jnp.mean(d * d, axis=-1, keepdims=True)
    return d * lax.rsqrt(var + LN_EPS) * g + b


def _ada_kernel(c_ref, w_ref, b_ref, o_ref):
    a = _silu(c_ref[...])
    o_ref[0] = jnp.dot(a, w_ref[0], precision=HIGHEST, preferred_element_type=F32) + b_ref[0]


def _ada_mods(c, ada_w, ada_b):
    depth, d, m = ada_w.shape
    b = c.shape[0]
    tn = m // 4
    return pl.pallas_call(
        _ada_kernel,
        grid=(depth, m // tn),
        in_specs=[pl.BlockSpec((b, d), lambda l, j: (0, 0)),
                  pl.BlockSpec((1, d, tn), lambda l, j: (l, 0, j)),
                  pl.BlockSpec((1, 1, tn), lambda l, j: (l, 0, j))],
        out_specs=pl.BlockSpec((1, b, tn), lambda l, j: (l, 0, j)),
        out_shape=jax.ShapeDtypeStruct((depth, b, m), F32),
        compiler_params=_params("arbitrary", "arbitrary"),
        name="ada_mods",
    )(c, ada_w, ada_b.reshape(depth, 1, m))


def _qkv_kernel(x_ref, sc_ref, sh_ref, pos_ref, w_ref, o_ref, trig_scr, *, n_rot):
    u = (x_ref[0] * sc_ref[0] + sh_ref[0]).astype(BF16)
    half = ROPE_DIM // 2
    pack = LANES // ROPE_DIM
    rows = x_ref.shape[1] // pack
    lane = lax.broadcasted_iota(jnp.int32, (1, LANES), 1)
    inv = jnp.exp(-math.log(ROPE_THETA) * (lane % half).astype(F32) * (2.0 / ROPE_DIM))
    pos = jnp.zeros((rows, LANES), F32)
    for m in range(pack):
        pos = jnp.where(lane // ROPE_DIM == m, pos_ref[0, pl.ds(m, rows, stride=pack), :].astype(F32), pos)
    ang = pos * inv
    for t, trig in enumerate((jnp.cos(ang), jnp.sin(ang))):
        for m in range(pack):
            moved = trig if m == 0 else pltpu.roll(trig, LANES - ROPE_DIM * m, 1)
            trig_scr[t, pl.ds(m, rows, stride=pack), :] = moved
    cos_t = jnp.where(lane < ROPE_DIM, trig_scr[0], 1.0)
    sin_t = jnp.where(lane < ROPE_DIM, trig_scr[1], 0.0)
    sin_lo = jnp.where(lane < half, -sin_t, 0.0)
    sin_hi = jnp.where((lane >= half) & (lane < ROPE_DIM), sin_t, 0.0)
    n_cols = w_ref.shape[1]
    for c in range(n_cols // MXU_N):
        y = _dot(u, w_ref[:, c * MXU_N:(c + 1) * MXU_N])
        for s in range(MXU_N // HEAD_DIM):
            col = c * MXU_N + s * HEAD_DIM
            t = y[:, s * HEAD_DIM:(s + 1) * HEAD_DIM]
            if col < n_rot:
                t = (t * cos_t + pltpu.roll(t, HEAD_DIM - half, 1) * sin_lo
                     + pltpu.roll(t, half, 1) * sin_hi)
            if col < n_rot // 2:
                t = t * (HEAD_DIM ** -0.5 * math.log2(math.e))
            o_ref[0, :, col:col + HEAD_DIM] = t.astype(BF16)


def _qkv_proj(x, sc, sh, pos, w, ts):
    b, s, d = x.shape
    n = w.shape[1]
    return pl.pallas_call(
        functools.partial(_qkv_kernel, n_rot=2 * d),
        grid=(b, s // ts),
        in_specs=[pl.BlockSpec((1, ts, d), lambda i, j: (i, j, 0)),
                  pl.BlockSpec((1, 1, d), lambda i, j: (i, 0, 0)),
                  pl.BlockSpec((1, 1, d), lambda i, j: (i, 0, 0)),
                  pl.BlockSpec((1, ts, 1), lambda i, j: (i, j, 0)),
                  pl.BlockSpec((d, n), lambda i, j: (0, 0))],
        out_specs=pl.BlockSpec((1, ts, n), lambda i, j: (i, j, 0)),
        out_shape=jax.ShapeDtypeStruct((b, s, n), BF16),
        scratch_shapes=[pltpu.VMEM((2, ts, LANES), F32)],
        compiler_params=_params("arbitrary", "arbitrary"),
        name="qkv_proj",
    )(x, sc, sh, pos, w)


def _moba_kernel(q_ref, k_ref, v_ref, o_ref, kmean_scr, vt_scr, bias_scr, m_scr, l_scr, acc_scr,
                 *, nb, n_grp, ch):
    blk = MOBA_BLOCK
    hd = HEAD_DIM
    cw = ch * blk
    i = pl.program_id(2)

    @pl.when(i == 0)
    def _():
        for hh in range(n_grp):
            means = []
            for n in range(nb):
                kb = k_ref[0, n * blk:(n + 1) * blk, hh * hd:(hh + 1) * hd].astype(F32)
                means.append(jnp.mean(kb, axis=0, keepdims=True))
                vb = v_ref[0, n * blk:(n + 1) * blk, hh * hd:(hh + 1) * hd]
                vt_scr[hh, n // ch, :, (n % ch) * blk:(n % ch + 1) * blk] = vb.astype(F32).T.astype(BF16)
            km = jnp.concatenate(means, axis=0)
            km_hi = km.astype(BF16)
            km_lo = (km - km_hi.astype(F32)).astype(BF16)
            kmean_scr[hh] = jnp.concatenate([km_hi, km_lo], axis=0)

    def sweep(c, n_own):
        own = n_own > 0
        n_blk = n_own if own else ch
        k0 = pl.multiple_of(c * cw, cw)
        heads = [slice(hh * hd, (hh + 1) * hd) for hh in range(n_grp)]
        scores = [_dot_nt(k_ref[0, pl.ds(k0, n_blk * blk), cols], q_ref[0, :, cols])
                  for cols in heads]
        m_news, l_news, probs = [], [], []
        for hh, s in enumerate(scores):
            parts = [s[t * blk:(t + 1) * blk] + bias_scr[hh, pl.ds(c * ch + t, 1), :]
                     for t in range(n_blk - 1 if own else n_blk)]
            if own:
                s_own = s[(n_blk - 1) * blk:]
                kpos = lax.broadcasted_iota(jnp.int32, s_own.shape, 0)
                qpos = lax.broadcasted_iota(jnp.int32, s_own.shape, 1)
                parts.append(jnp.where(kpos <= qpos, s_own, NEG_BIG))
            m_blk = parts[0]
            for t in range(1, n_blk):
                m_blk = jnp.maximum(m_blk, parts[t])
            m_new = jnp.max(m_blk, axis=0, keepdims=True)
            if not own:
                m_new = jnp.maximum(m_new, m_scr[hh])
            ps = [jnp.exp2(part - m_new) for part in parts]
            l_new = ps[0]
            for t in range(1, n_blk):
                l_new = l_new + ps[t]
            m_news.append(m_new)
            l_news.append(jnp.sum(l_new, axis=0, keepdims=True))
            probs.append(jnp.concatenate([p.astype(BF16) for p in ps], axis=0) if n_blk > 1
                         else ps[0].astype(BF16))
        pvs = [_dot(vt_scr[hh, c, :, 0:n_blk * blk], probs[hh]) for hh in range(n_grp)]
        for hh in range(n_grp):
            if own:
                l_scr[hh] = l_news[hh]
                acc_scr[hh] = pvs[hh]
            else:
                alpha = jnp.exp2(m_scr[hh] - m_news[hh])
                l_scr[hh] = alpha * l_scr[hh] + l_news[hh]
                acc_scr[hh] = alpha * acc_scr[hh] + pvs[hh]
            m_scr[hh] = m_news[hh]

    c_own = i // ch
    for hh in range(n_grp):
        q = q_ref[0, :, hh * hd:(hh + 1) * hd]
        gate2 = _dot_nt(kmean_scr[hh], q)
        gate = gate2[:nb] + gate2[nb:]
        blk_id = lax.broadcasted_iota(jnp.int32, gate.shape, 0)
        past = blk_id < i
        g = jnp.where(past, gate, -jnp.inf)
        rank = jnp.zeros(gate.shape, jnp.int32)
        for m in range(nb):
            gm = g[m:m + 1, :]
            beats = jnp.where(gm > g, 1, jnp.where(gm == g, jnp.where(blk_id > m, 1, 0), 0))
            rank = rank + beats
        bias_scr[hh] = jnp.where(past, jnp.where(rank < MOBA_TOPK, 0.0, NEG_BIG), NEG_BIG)

    for r in range(ch):
        @pl.when(i % ch == r)
        def _():
            sweep(c_own, r + 1)

    def body(c, carry):
        sweep(c, 0)
        return carry

    lax.fori_loop(0, c_own, body, 0)
    for hh in range(n_grp):
        o_ref[0, :, hh * hd:(hh + 1) * hd] = (acc_scr[hh] / l_scr[hh]).T.astype(BF16)


def _moba_attention(qkv, d, n_grp, ch):
    b, s, _ = qkv.shape
    h = d // HEAD_DIM
    blk = MOBA_BLOCK
    nb = s // blk
    gw = n_grp * HEAD_DIM
    ng = h // n_grp
    return pl.pallas_call(
        functools.partial(_moba_kernel, nb=nb, n_grp=n_grp, ch=ch),
        grid=(b, ng, nb),
        in_specs=[pl.BlockSpec((1, blk, gw), lambda bi, hi, i: (bi, i, hi)),
                  pl.BlockSpec((1, s, gw), lambda bi, hi, i: (bi, 0, ng + hi)),
                  pl.BlockSpec((1, s, gw), lambda bi, hi, i: (bi, 0, 2 * ng + hi))],
        out_specs=pl.BlockSpec((1, blk, gw), lambda bi, hi, i: (bi, i, hi)),
        out_shape=jax.ShapeDtypeStruct((b, s, d), BF16),
        scratch_shapes=[pltpu.VMEM((n_grp, 2 * nb, HEAD_DIM), BF16),
                        pltpu.VMEM((n_grp, nb // ch, HEAD_DIM, ch * blk), BF16),
                        pltpu.VMEM((n_grp, nb, blk), F32),
                        pltpu.VMEM((n_grp, 1, blk), F32),
                        pltpu.VMEM((n_grp, 1, blk), F32),
                        pltpu.VMEM((n_grp, HEAD_DIM, blk), F32)],
        compiler_params=_params("arbitrary", "arbitrary", "arbitrary"),
        name="moba_attention",
    )(qkv, qkv, qkv)


def _store_row_tiles(ref, val):
    rows, d = val.shape
    sub = d // LANES
    for c in range(sub):
        ref[pl.ds(c, rows, stride=sub), :] = val[:, c * LANES:(c + 1) * LANES]


def _load_row_tiles(ref, rows, dtype):
    sub = ref.shape[0] // rows
    return jnp.concatenate([ref[pl.ds(c, rows, stride=sub), :].astype(dtype) for c in range(sub)], axis=1)


def _route_top2(u, wr_ref):
    u_hi = u.astype(BF16)
    u_lo = (u - u_hi.astype(F32)).astype(BF16)
    both = _dot(u_hi, wr_ref[...])
    logits = both[:, :LANES] + both[:, LANES:] + _dot(u_lo, wr_ref[:, :LANES])
    lane = lax.broadcasted_iota(jnp.int32, logits.shape, 1)
    logits = jnp.where(lane < N_EXPERTS, logits, -jnp.inf)
    m1 = jnp.max(logits, axis=-1, keepdims=True)
    i1 = jnp.min(jnp.where(logits == m1, lane, LANES), axis=-1, keepdims=True)
    rest = jnp.where(lane == i1, -jnp.inf, logits)
    m2 = jnp.max(rest, axis=-1, keepdims=True)
    i2 = jnp.min(jnp.where(rest == m2, lane, LANES), axis=-1, keepdims=True)
    e2 = jnp.exp(m2 - m1)
    w1 = 1.0 / (1.0 + e2)
    w2 = e2 / (1.0 + e2)
    return jnp.where(lane == 0, i1.astype(F32),
                     jnp.where(lane == 1, i2.astype(F32),
                               jnp.where(lane == 2, w1, jnp.where(lane == 3, w2, 0.0))))


def _proj_ln_route_kernel(a_ref, w_ref, x_ref, g1p_ref, lng_ref, lnb_ref, sc_ref, sh_ref, wr_ref,
                          xo_ref, uo_ref, ro_ref, *, alpha):
    y = _dot(a_ref[0], w_ref[...])
    xn = _layer_norm(alpha * x_ref[0] + g1p_ref[0] * y, lng_ref[...], lnb_ref[...])
    xo_ref[0] = xn
    u = xn * sc_ref[0] + sh_ref[0]
    _store_row_tiles(uo_ref.at[0], u)
    ro_ref[0] = _route_top2(u, wr_ref)


def _proj_ln_route(a, w, x, g1p, lng, lnb, sc, sh, wr, alpha, tm):
    b, s, d = x.shape
    sub = d // LANES
    tile = pl.BlockSpec((1, tm, d), lambda i, j: (i, j, 0))
    per_b = pl.BlockSpec((1, 1, d), lambda i, j: (i, 0, 0))
    vec = pl.BlockSpec((1, d), lambda i, j: (0, 0))
    return pl.pallas_call(
        functools.partial(_proj_ln_route_kernel, alpha=alpha),
        grid=(b, s // tm),
        in_specs=[tile, pl.BlockSpec((d, d), lambda i, j: (0, 0)), tile, per_b, vec, vec, per_b, per_b,
                  pl.BlockSpec((d, 2 * LANES), lambda i, j: (0, 0))],
        out_specs=[tile, pl.BlockSpec((1, tm * sub, LANES), lambda i, j: (i, j, 0)),
                   pl.BlockSpec((1, tm, LANES), lambda i, j: (i, j, 0))],
        out_shape=[jax.ShapeDtypeStruct((b, s, d), F32), jax.ShapeDtypeStruct((b, s * sub, LANES), F32),
                   jax.ShapeDtypeStruct((b, s, LANES), F32)],
        compiler_params=_params("arbitrary", "arbitrary"),
        name="proj_ln_route",
    )(a, w, x, g1p, lng, lnb, sc, sh, wr)


def _proj_ffn_kernel(a_ref, wo_ref, x_ref, g1m_ref, lng_ref, lnb_ref, sc_ref, sh_ref, wg_ref, wu_ref, wd_ref,
                     g1f_ref, xo_ref, h_scr, *, alpha, tf):
    x1 = _layer_norm(alpha * x_ref[0] + g1m_ref[0] * _dot(a_ref[0], wo_ref[...]), lng_ref[0], lnb_ref[0])
    u = (x1 * sc_ref[0] + sh_ref[0]).astype(BF16)
    f = wg_ref.shape[1]
    for j in range(f // tf):
        cols = slice(j * tf, (j + 1) * tf)
        hj = _silu(_dot(u, wg_ref[:, cols])) * _dot(u, wu_ref[:, cols])
        h_scr[:, cols] = hj.astype(BF16)
    y = _dot(h_scr[...], wd_ref[...])
    xo_ref[0] = _layer_norm(alpha * x1 + g1f_ref[0] * y, lng_ref[1], lnb_ref[1])


def _proj_ffn_dense(a, wo, x, g1m, lng, lnb, sc, sh, wg, wu, wd, g1f, alpha, tm):
    b, s, d = x.shape
    f = wg.shape[1]
    tile = pl.BlockSpec((1, tm, d), lambda i, j: (i, j, 0))
    per_b = pl.BlockSpec((1, 1, d), lambda i, j: (i, 0, 0))
    ln_pair = pl.BlockSpec((2, 1, d), lambda i, j: (0, 0, 0))
    resident = dict(pipeline_mode=pl.Buffered(1))
    return pl.pallas_call(
        functools.partial(_proj_ffn_kernel, alpha=alpha, tf=MXU_N),
        grid=(b, s // tm),
        in_specs=[tile, pl.BlockSpec((d, d), lambda i, j: (0, 0), **resident), tile, per_b, ln_pair, ln_pair,
                  per_b, per_b,
                  pl.BlockSpec((d, f), lambda i, j: (0, 0), **resident),
                  pl.BlockSpec((d, f), lambda i, j: (0, 0), **resident),
                  pl.BlockSpec((f, d), lambda i, j: (0, 0), **resident),
                  per_b],
        out_specs=tile,
        out_shape=jax.ShapeDtypeStruct((b, s, d), F32),
        scratch_shapes=[pltpu.VMEM((tm, f), BF16)],
        compiler_params=_params("arbitrary", "arbitrary"),
        name="proj_ffn_dense",
    )(a, wo, x, g1m, lng, lnb, sc, sh, wg, wu, wd, g1f)


def _rec_in_kernel(x_ref, sc_ref, sh_ref, w_ref, lbl_ref, qf_ref, key_ref, v_ref, gs_ref, lf_ref,
                   *, layer_idx):
    u = (x_ref[0] * sc_ref[0] + sh_ref[0]).astype(BF16)
    d = x_ref.shape[2]
    lbl = lbl_ref[...]
    e = jnp.exp(lbl - jnp.max(lbl, axis=0, keepdims=True))
    sm = e / jnp.sum(e, axis=0, keepdims=True)
    lb = jnp.zeros((1, d), F32)
    for r in range(1, layer_idx + 1):
        lb = lb + sm[r:r + 1, :]
    for c in range(d // MXU_N):
        cols = slice(c * MXU_N, (c + 1) * MXU_N)
        q = _dot(u, w_ref[:, c * MXU_N:(c + 1) * MXU_N])
        qf_ref[0, :, cols] = _silu(q).astype(BF16)
        f = _dot(u, w_ref[:, d + c * MXU_N:d + (c + 1) * MXU_N])
        lbc = lb[:, cols]
        f_gate = lbc + (1.0 - lbc) * jax.nn.sigmoid(f)
        lf_ref[0, :, cols] = jnp.log(f_gate)
        key_ref[0, :, cols] = (1.0 - f_gate).astype(BF16)
        v = _dot(u, w_ref[:, 2 * d + c * MXU_N:2 * d + (c + 1) * MXU_N])
        v_ref[0, :, cols] = v.astype(BF16)
        g = _dot(u, w_ref[:, 3 * d + c * MXU_N:3 * d + (c + 1) * MXU_N])
        gs_ref[0, :, cols] = _silu(g).astype(BF16)


def _rec_in(x, sc, sh, w, lb_logits, layer_idx, ts):
    b, s, d = x.shape
    depth = lb_logits.shape[0]
    tile = pl.BlockSpec((1, ts, d), lambda i, j: (i, j, 0))
    per_b = pl.BlockSpec((1, 1, d), lambda i, j: (i, 0, 0))
    bf = jax.ShapeDtypeStruct((b, s, d), BF16)
    return pl.pallas_call(
        functools.partial(_rec_in_kernel, layer_idx=layer_idx),
        grid=(b, s // ts),
        in_specs=[tile, per_b, per_b,
                  pl.BlockSpec((d, 4 * d), lambda i, j: (0, 0)),
                  pl.BlockSpec((depth, d), lambda i, j: (0, 0))],
        out_specs=[tile] * 5,
        out_shape=[bf, bf, bf, bf, jax.ShapeDtypeStruct((b, s, d), F32)],
        compiler_params=_params("arbitrary", "arbitrary"),
        name="rec_in",
    )(x, sc, sh, w, lb_logits)


def _hgrn_kernel(qf_ref, key_ref, v_ref, lf_ref, gs_ref, nw_ref, o_ref, st_scr, *, n_heads):
    cs = HGRN_CHUNK
    dk = HGRN_EXPAND

    @pl.when(pl.program_id(1) == 0)
    def _():
        st_scr[...] = jnp.zeros(st_scr.shape, F32)

    row = lax.broadcasted_iota(jnp.int32, (cs, cs), 0)
    col = lax.broadcasted_iota(jnp.int32, (cs, cs), 1)
    causal = row >= col
    tri = jnp.where(causal, 1.0, 0.0).astype(BF16)
    n_chunks = qf_ref.shape[1] // cs

    def chunk(c, carry):
        rows = pl.ds(pl.multiple_of(c * cs, cs), cs)
        lf = lf_ref[0, rows, :]
        hi = lf.astype(BF16)
        r1 = lf - hi.astype(F32)
        mid = r1.astype(BF16)
        lo = (r1 - mid.astype(F32)).astype(BF16)
        g = _dot(tri, hi) + _dot(tri, mid) + _dot(tri, lo)
        g_last = g[cs - 1:cs, :]
        qf = qf_ref[0, rows, :].astype(F32)
        key = key_ref[0, rows, :].astype(F32)
        v = v_ref[0, rows, :]
        q_dec = (qf * jnp.exp(g)).astype(BF16)
        k_dec = (key * jnp.exp(-g)).astype(BF16)
        k_state = (key * jnp.exp(g_last - g)).astype(BF16)
        decay = jnp.exp(g_last)
        heads = [slice(h * dk, (h + 1) * dk) for h in range(n_heads)]
        a = [jnp.where(causal, _dot_nt(q_dec[:, c], k_dec[:, c]), 0.0).astype(BF16) for c in heads]
        st = [st_scr[h] for h in range(n_heads)]
        o = [_dot(a[h], v[:, c]) + _dot_nt(q_dec[:, c], st[h].astype(BF16))
             for h, c in enumerate(heads)]
        for h, c in enumerate(heads):
            v_t = v[:, c].astype(F32).T.astype(BF16)
            st_scr[h] = st[h] * decay[:, c] + _dot(v_t, k_state[:, c])
        on = jnp.concatenate(
            [oh * lax.rsqrt(jnp.mean(oh * oh, axis=-1, keepdims=True) + RMS_EPS) for oh in o], axis=1)
        o_ref[0, rows, :] = (on * nw_ref[...] * gs_ref[0, rows, :].astype(F32)).astype(BF16)
        return carry

    lax.fori_loop(0, n_chunks, chunk, 0, unroll=4)


def _hgrn_recurrence(qf, key, v, lf, gs, norm_w, ts):
    b, s, d = qf.shape
    h = d // HGRN_EXPAND
    tile = pl.BlockSpec((1, ts, d), lambda i, j: (i, j, 0))
    return pl.pallas_call(
        functools.partial(_hgrn_kernel, n_heads=h),
        grid=(b, s // ts),
        in_specs=[tile, tile, tile, tile, tile, pl.BlockSpec((1, d), lambda i, j: (0, 0))],
        out_specs=tile,
        out_shape=jax.ShapeDtypeStruct((b, s, d), BF16),
        scratch_shapes=[pltpu.VMEM((h, HGRN_EXPAND, HGRN_EXPAND), F32)],
        compiler_params=_params("arbitrary", "arbitrary"),
        name="hgrn_recurrence",
    )(qf, key, v, lf, gs, norm_w)


GATHER_UNROLL = 8


ROW_TILE = 8


def _start_row(idx_at, src_hbm, dst, sem, r):
    first = pl.multiple_of(idx_at(r), ROW_TILE)
    pltpu.make_async_copy(src_hbm.at[pl.ds(first, ROW_TILE), :],
                          dst.at[pl.ds(r * ROW_TILE, ROW_TILE), :], sem).start()


def _start_row_gather(idx_at, src_hbm, dst, sem, n_rows):
    def one(r, carry):
        _start_row(idx_at, src_hbm, dst, sem, r)
        return carry

    lax.fori_loop(0, n_rows, one, 0, unroll=GATHER_UNROLL)


def _wait_rows(buf, sem):
    pltpu.make_async_copy(buf, buf, sem).wait()


MOE_AHEAD = 2
MOE_XBUFS = MOE_AHEAD + 1


def _moe_ffn_kernel(te_ref, na_ref, src_t0_ref, src_t1_ref, src_ahead_ref, u_hbm, wg_ref, wu_ref, wd_ref,
                    o_ref, xbuf, xb_scr, h_scr, acc_scr, sems, *, tm, nf, n_tiles):
    i = pl.program_id(0)
    j = pl.program_id(1)
    rows_per_step = tm // nf
    n_active = na_ref[0]
    active = i < n_active
    slot = i % MOE_XBUFS
    ahead = (i + MOE_AHEAD) % MOE_XBUFS

    assert MOE_AHEAD == 2

    @pl.when((j == 0) & (i == 0))
    def _():
        _start_row_gather(lambda r: src_t0_ref[0, 0, r], u_hbm, xbuf.at[0], sems.at[0], tm)
        _start_row_gather(lambda r: src_t1_ref[0, 0, r], u_hbm, xbuf.at[1], sems.at[1], tm)

    @pl.when((j == 0) & (i <= n_active + 1))
    def _():
        _wait_rows(xbuf.at[slot], sems.at[slot])

    @pl.when(active & (j == 0))
    def _():
        xb_scr[...] = _load_row_tiles(xbuf.at[slot], tm, BF16)

    @pl.when(active)
    def _():
        for k in range(rows_per_step):
            _start_row(lambda r: src_ahead_ref[0, 0, r], u_hbm, xbuf.at[ahead], sems.at[ahead],
                       j * rows_per_step + k)
        x = xb_scr[...]
        for c in range(wg_ref.shape[2] // MXU_N):
            cols = slice(c * MXU_N, (c + 1) * MXU_N)
            hc = _silu(_dot(x, wg_ref[0, :, cols])) * _dot(x, wu_ref[0, :, cols])
            h_scr[:, cols] = hc.astype(BF16)
        y = _dot(h_scr[...], wd_ref[0])
        if nf == 1:
            _store_row_tiles(o_ref, y)
        else:
            @pl.when(j == 0)
            def _():
                acc_scr[...] = y

            @pl.when((j > 0) & (j < nf - 1))
            def _():
                acc_scr[...] += y

            @pl.when(j == nf - 1)
            def _():
                _store_row_tiles(o_ref, acc_scr[...] + y)

    @pl.when(jnp.logical_not(active) & (j == nf - 1) & (i < n_tiles))
    def _():
        o_ref[...] = jnp.zeros(o_ref.shape, F32)


def _moe_ffn(u_tiles, src, wg, wu, wd, tile_expert, n_active, tm, tf):
    d = wg.shape[1]
    sub = d // LANES
    assert sub == ROW_TILE
    n_tiles = src.shape[0] // tm
    assert tile_expert.shape[0] == n_tiles + MOE_AHEAD
    f = wg.shape[2]
    nf = f // tf

    def frozen(j, i, na):
        return jnp.where(i < na[0], j, nf - 1)

    def idx_block(tile_of):
        return pl.BlockSpec((1, 1, tm), lambda i, j, te, na: (jnp.minimum(tile_of(i), n_tiles - 1), 0, 0),
                            memory_space=pltpu.SMEM)

    grid_spec = pltpu.PrefetchScalarGridSpec(
        num_scalar_prefetch=2,
        grid=(n_tiles + MOE_AHEAD, nf),
        in_specs=[idx_block(lambda i: 0), idx_block(lambda i: 1), idx_block(lambda i: i + MOE_AHEAD),
                  pl.BlockSpec(memory_space=pl.ANY),
                  pl.BlockSpec((1, d, tf), lambda i, j, te, na: (te[i], 0, frozen(j, i, na))),
                  pl.BlockSpec((1, d, tf), lambda i, j, te, na: (te[i], 0, frozen(j, i, na))),
                  pl.BlockSpec((1, tf, d), lambda i, j, te, na: (te[i], frozen(j, i, na), 0))],
        out_specs=pl.BlockSpec((tm * sub, LANES), lambda i, j, te, na: (jnp.minimum(i, n_tiles - 1), 0)),
        scratch_shapes=[pltpu.VMEM((MOE_XBUFS, tm * sub, LANES), F32),
                        pltpu.VMEM((tm, d), BF16),
                        pltpu.VMEM((tm, tf), BF16),
                        pltpu.VMEM((tm, d), F32),
                        pltpu.SemaphoreType.DMA((MOE_XBUFS,))],
    )
    src3 = src.reshape(n_tiles, 1, tm)
    return pl.pallas_call(
        functools.partial(_moe_ffn_kernel, tm=tm, nf=nf, n_tiles=n_tiles),
        grid_spec=grid_spec,
        out_shape=jax.ShapeDtypeStruct((n_tiles * tm * sub, LANES), F32),
        compiler_params=_params("arbitrary", "arbitrary"),
        name="moe_ffn",
    )(tile_expert, n_active, src3, src3, src3, u_tiles, wg, wu, wd)


def _combine_ln_kernel(sa_cur, sb_cur, sa_nxt, sb_nxt, ys_hbm, r_ref, x_ref, g1p_ref, lng_ref, lnb_ref,
                       xo_ref, ybuf, sems, *, alpha, tm):
    i = pl.program_id(0)
    slot = i % 2

    def start(sa, sb, s):
        _start_row_gather(lambda r: sa[r], ys_hbm, ybuf.at[s, 0], sems.at[s], tm)
        _start_row_gather(lambda r: sb[r], ys_hbm, ybuf.at[s, 1], sems.at[s], tm)

    @pl.when(i == 0)
    def _():
        start(sa_cur, sb_cur, 0)

    @pl.when(i + 1 < pl.num_programs(0))
    def _():
        start(sa_nxt, sb_nxt, 1 - slot)

    _wait_rows(ybuf.at[slot], sems.at[slot])
    r = r_ref[...]
    y = (r[:, 2:3] * _load_row_tiles(ybuf.at[slot, 0], tm, F32)
         + r[:, 3:4] * _load_row_tiles(ybuf.at[slot, 1], tm, F32))
    xo_ref[...] = _layer_norm(alpha * x_ref[...] + g1p_ref[0] * y, lng_ref[...], lnb_ref[...])


def _combine_ln(ys_tiles, slot_a, slot_b, route, x, g1p, lng, lnb, alpha, tm):
    b, s, d = x.shape
    n = b * s
    sub = d // LANES
    n_tiles = n // tm
    per_seq = s // tm
    cur = pl.BlockSpec((tm,), lambda i: (i,), memory_space=pltpu.SMEM)
    nxt = pl.BlockSpec((tm,), lambda i: (jnp.minimum(i + 1, n_tiles - 1),), memory_space=pltpu.SMEM)
    tile = pl.BlockSpec((tm, d), lambda i: (i, 0))
    vec = pl.BlockSpec((1, d), lambda i: (0, 0))
    out = pl.pallas_call(
        functools.partial(_combine_ln_kernel, alpha=alpha, tm=tm),
        grid=(n_tiles,),
        in_specs=[cur, cur, nxt, nxt, pl.BlockSpec(memory_space=pl.ANY),
                  pl.BlockSpec((tm, LANES), lambda i: (i, 0)), tile,
                  pl.BlockSpec((1, 1, d), lambda i: (i // per_seq, 0, 0)), vec, vec],
        out_specs=tile,
        out_shape=jax.ShapeDtypeStruct((n, d), F32),
        scratch_shapes=[pltpu.VMEM((2, 2, tm * sub, LANES), F32), pltpu.SemaphoreType.DMA((2,))],
        compiler_params=_params("arbitrary"),
        name="combine_ln",
    )(slot_a, slot_b, slot_a, slot_b, ys_tiles, route.reshape(n, LANES), x.reshape(n, d), g1p, lng, lnb)
    return out.reshape(b, s, d)


def _moe_plan(route, tm):
    n = route.shape[0]
    e = N_EXPERTS
    n_tiles = (TOP_K * n + e * (tm - 1)) // tm
    flat_e = route[:, :TOP_K].astype(jnp.int32).reshape(-1)
    onehot = (flat_e[:, None] == jnp.arange(e, dtype=jnp.int32)[None, :]).astype(jnp.int32)
    incl = jnp.cumsum(onehot, axis=0)
    counts = incl[-1]
    padded = (counts + tm - 1) // tm * tm
    ends = jnp.cumsum(padded)
    offs = ends - padded
    slot = jnp.sum(onehot * (offs[None, :] + incl - 1), axis=1)
    n_active = ends[-1] // tm
    tile_id = jnp.minimum(jnp.arange(n_tiles + MOE_AHEAD, dtype=jnp.int32), n_active - 1)
    tile_expert = jnp.minimum(jnp.sum((tile_id[:, None] * tm >= ends[None, :]).astype(jnp.int32), axis=1), e - 1)
    by_expert = jnp.argsort(flat_e, stable=True).astype(jnp.int32) // TOP_K
    by_expert = jnp.pad(by_expert, (0, n_tiles * tm - TOP_K * n))
    row_expert = jnp.repeat(tile_expert[:n_tiles], tm)
    shift = offs - (jnp.cumsum(counts) - counts)
    src = jnp.zeros((n_tiles * tm,), jnp.int32)
    for k in range(e):
        src = jnp.where(row_expert == k, jnp.roll(by_expert, shift[k]), src)
    return ((slot * ROW_TILE).reshape(n, TOP_K), src * ROW_TILE, tile_expert.astype(jnp.int32),
            n_active.reshape(1).astype(jnp.int32))


def _moe_swiglu_ln(u_tiles, route, wg, wu, wd, x, g1p, lng, lnb, alpha, tm_moe=896, tm_ln=512):
    assert TOP_K == 2
    b, s, d = x.shape
    n = b * s
    f = wg.shape[2]
    nf = 2 if f % (2 * MXU_N) == 0 else 1
    tf = f // nf
    assert tm_moe % nf == 0
    slot, src, tile_expert, n_active = _moe_plan(route.reshape(n, LANES), tm_moe)
    ys = _moe_ffn(u_tiles.reshape(n * (d // LANES), LANES), src, wg, wu, wd, tile_expert, n_active, tm_moe, tf)
    return _combine_ln(ys, slot[:, 0], slot[:, 1], route, x, g1p, lng, lnb, alpha, tm_ln)


def kernel(x, c, positions, ada_w, ada_b, ln_g, ln_b, attn_w_in, attn_w_out, rec_w_in, rec_lb_logits,
           rec_norm_w, rec_w_out, ffn_w_gate, ffn_w_up, ffn_w_down, router_w, moe_w_gate, moe_w_up,
           moe_w_down):
    b, s, d = x.shape
    depth = ada_w.shape[0]
    alpha = (2.0 * depth) ** 0.25
    ts = 512

    mods = _ada_mods(c, ada_w, ada_b).reshape(depth, b, 6, 1, d)
    one_plus = lambda t: 1.0 + t
    pos = positions.reshape(b, s, 1)

    for i in range(depth):
        shift_m, scale_m, gate_m, shift_f, scale_f, gate_f = (mods[i, :, r] for r in range(6))
        j = i // 2
        lng = ln_g[i].reshape(2, 1, d)
        lnb = ln_b[i].reshape(2, 1, d)
        if i % 2 == 0:
            qkv = _qkv_proj(x, one_plus(scale_m), shift_m, pos, attn_w_in[j].astype(BF16), ts)
            o = _moba_attention(qkv, d, n_grp=math.gcd(d // HEAD_DIM, 8),
                                ch=math.gcd(s // MOBA_BLOCK, 4))
            x = _proj_ffn_dense(o, attn_w_out[j].astype(BF16), x, one_plus(gate_m), lng, lnb,
                                one_plus(scale_f), shift_f, ffn_w_gate[j].astype(BF16),
                                ffn_w_up[j].astype(BF16), ffn_w_down[j].astype(BF16), one_plus(gate_f),
                                alpha, ts)
        else:
            qf, key, v, gs, lf = _rec_in(x, one_plus(scale_m), shift_m, rec_w_in[j].astype(BF16),
                                         rec_lb_logits, i, ts)
            o = _hgrn_recurrence(qf, key, v, lf, gs, rec_norm_w[j].reshape(1, d), ts)
            w_hi = router_w[j].astype(BF16)
            w_lo = (router_w[j] - w_hi.astype(F32)).astype(BF16)
            wr = (jnp.zeros((d, 2 * LANES), BF16).at[:, :N_EXPERTS].set(w_hi)
                  .at[:, LANES:LANES + N_EXPERTS].set(w_lo))
            x, u, route = _proj_ln_route(o, rec_w_out[j].astype(BF16), x, one_plus(gate_m), lng[0], lnb[0],
                                         one_plus(scale_f), shift_f, wr, alpha, ts)
            x = _moe_swiglu_ln(u, route, moe_w_gate[j].astype(BF16), moe_w_up[j].astype(BF16),
                               moe_w_down[j].astype(BF16), x, one_plus(gate_f), lng[1], lnb[1], alpha)
    return x
```

```python
import functools
import math
from typing import NamedTuple

import jax
import jax.numpy as jnp
from jax import lax
from jax.experimental import pallas as pl
from jax.experimental.pallas import tpu as pltpu

HEAD_DIM = 128
ROPE_DIM = HEAD_DIM // 4
ROPE_THETA = 500000.0
MOBA_BLOCK = 256
MOBA_TOPK = 3
HGRN_EXPAND = 128
HGRN_CHUNK = 64
N_EXPERTS = 8
TOP_K = 2
LN_EPS = 1e-5
RMS_EPS = 1e-6

LANES = 128
MXU_N = 256
VMEM_LIMIT = 56 * 1024 * 1024
NEG_BIG = -1e30

F32 = jnp.float32
BF16 = jnp.bfloat16
HIGHEST = lax.Precision.HIGHEST


def _params(*sem):
    return pltpu.CompilerParams(dimension_semantics=sem, vmem_limit_bytes=VMEM_LIMIT)


def _dot(a, b):
    return jnp.dot(a, b, preferred_element_type=F32)


def _dot_nt(a, b, precision=None):
    return lax.dot_general(a, b, (((1,), (1,)), ((), ())), precision=precision,
                           preferred_element_type=F32)


def _silu(x):
    return x * jax.nn.sigmoid(x)


def _layer_norm(z, g, b):
    mu = jnp.mean(z, axis=-1, keepdims=True)
    d = z - mu
    var = jnp.mean(d * d, axis=-1, keepdims=True)
    return d * lax.rsqrt(var + LN_EPS) * g + b


def _ada_kernel(c_ref, w_ref, b_ref, o_ref):
    a = _silu(c_ref[...])
    o_ref[0] = jnp.dot(a, w_ref[0], precision=HIGHEST, preferred_element_type=F32) + b_ref[0]


def _ada_mods(c, ada_w, ada_b):
    depth, d, m = ada_w.shape
    b = c.shape[0]
    tn = m // 4
    return pl.pallas_call(
        _ada_kernel,
        grid=(depth, m // tn),
        in_specs=[pl.BlockSpec((b, d), lambda l, j: (0, 0)),
                  pl.BlockSpec((1, d, tn), lambda l, j: (l, 0, j)),
                  pl.BlockSpec((1, 1, tn), lambda l, j: (l, 0, j))],
        out_specs=pl.BlockSpec((1, b, tn), lambda l, j: (l, 0, j)),
        out_shape=jax.ShapeDtypeStruct((depth, b, m), F32),
        compiler_params=_params("arbitrary", "arbitrary"),
        name="ada_mods",
    )(c, ada_w, ada_b.reshape(depth, 1, m))


def _qkv_kernel(x_ref, sc_ref, sh_ref, pos_ref, w_ref, o_ref, trig_scr, *, n_rot):
    u = (x_ref[0] * sc_ref[0] + sh_ref[0]).astype(BF16)
    half = ROPE_DIM // 2
    pack = LANES // ROPE_DIM
    rows = x_ref.shape[1] // pack
    lane = lax.broadcasted_iota(jnp.int32, (1, LANES), 1)
    inv = jnp.exp(-math.log(ROPE_THETA) * (lane % half).astype(F32) * (2.0 / ROPE_DIM))
    pos = jnp.zeros((rows, LANES), F32)
    for m in range(pack):
        pos = jnp.where(lane // ROPE_DIM == m, pos_ref[0, pl.ds(m, rows, stride=pack), :].astype(F32), pos)
    ang = pos * inv
    for t, trig in enumerate((jnp.cos(ang), jnp.sin(ang))):
        for m in range(pack):
            moved = trig if m == 0 else pltpu.roll(trig, LANES - ROPE_DIM * m, 1)
            trig_scr[t, pl.ds(m, rows, stride=pack), :] = moved
    cos_t = jnp.where(lane < ROPE_DIM, trig_scr[0], 1.0)
    sin_t = jnp.where(lane < ROPE_DIM, trig_scr[1], 0.0)
    sin_lo = jnp.where(lane < half, -sin_t, 0.0)
    sin_hi = jnp.where((lane >= half) & (lane < ROPE_DIM), sin_t, 0.0)
    n_cols = w_ref.shape[1]
    for c in range(n_cols // MXU_N):
        y = _dot(u, w_ref[:, c * MXU_N:(c + 1) * MXU_N])
        for s in range(MXU_N // HEAD_DIM):
            col = c * MXU_N + s * HEAD_DIM
            t = y[:, s * HEAD_DIM:(s + 1) * HEAD_DIM]
            if col < n_rot:
                t = (t * cos_t + pltpu.roll(t, HEAD_DIM - half, 1) * sin_lo
                     + pltpu.roll(t, half, 1) * sin_hi)
            if col < n_rot // 2:
                t = t * (HEAD_DIM ** -0.5 * math.log2(math.e))
            o_ref[0, :, col:col + HEAD_DIM] = t.astype(BF16)


def _qkv_proj(x, sc, sh, pos, w, ts):
    b, s, d = x.shape
    n = w.shape[1]
    return pl.pallas_call(
        functools.partial(_qkv_kernel, n_rot=2 * d),
        grid=(b, s // ts),
        in_specs=[pl.BlockSpec((1, ts, d), lambda i, j: (i, j, 0)),
                  pl.BlockSpec((1, 1, d), lambda i, j: (i, 0, 0)),
                  pl.BlockSpec((1, 1, d), lambda i, j: (i, 0, 0)),
                  pl.BlockSpec((1, ts, 1), lambda i, j: (i, j, 0)),
                  pl.BlockSpec((d, n), lambda i, j: (0, 0))],
        out_specs=pl.BlockSpec((1, ts, n), lambda i, j: (i, j, 0)),
        out_shape=jax.ShapeDtypeStruct((b, s, n), BF16),
        scratch_shapes=[pltpu.VMEM((2, ts, LANES), F32)],
        compiler_params=_params("arbitrary", "arbitrary"),
        name="qkv_proj",
    )(x, sc, sh, pos, w)


def _moba_kernel(q_ref, k_ref, v_ref, o_ref, kmean_scr, vt_scr, bias_scr, m_scr, l_scr, acc_scr,
                 *, nb, n_grp, ch):
    blk = MOBA_BLOCK
    hd = HEAD_DIM
    cw = ch * blk
    i = pl.program_id(2)

    @pl.when(i == 0)
    def _():
        for hh in range(n_grp):
            means = []
            for n in range(nb):
                kb = k_ref[0, n * blk:(n + 1) * blk, hh * hd:(hh + 1) * hd].astype(F32)
                means.append(jnp.mean(kb, axis=0, keepdims=True))
                vb = v_ref[0, n * blk:(n + 1) * blk, hh * hd:(hh + 1) * hd]
                vt_scr[hh, n // ch, :, (n % ch) * blk:(n % ch + 1) * blk] = vb.astype(F32).T.astype(BF16)
            km = jnp.concatenate(means, axis=0)
            km_hi = km.astype(BF16)
            km_lo = (km - km_hi.astype(F32)).astype(BF16)
            kmean_scr[hh] = jnp.concatenate([km_hi, km_lo], axis=0)

    def sweep(c, n_own):
        own = n_own > 0
        n_blk = n_own if own else ch
        k0 = pl.multiple_of(c * cw, cw)
        heads = [slice(hh * hd, (hh + 1) * hd) for hh in range(n_grp)]
        scores = [_dot_nt(k_ref[0, pl.ds(k0, n_blk * blk), cols], q_ref[0, :, cols])
                  for cols in heads]
        m_news, l_news, probs = [], [], []
        for hh, s in enumerate(scores):
            parts = [s[t * blk:(t + 1) * blk] + bias_scr[hh, pl.ds(c * ch + t, 1), :]
                     for t in range(n_blk - 1 if own else n_blk)]
            if own:
                s_own = s[(n_blk - 1) * blk:]
                kpos = lax.broadcasted_iota(jnp.int32, s_own.shape, 0)
                qpos = lax.broadcasted_iota(jnp.int32, s_own.shape, 1)
                parts.append(jnp.where(kpos <= qpos, s_own, NEG_BIG))
            m_blk = parts[0]
            for t in range(1, n_blk):
                m_blk = jnp.maximum(m_blk, parts[t])
            m_new = jnp.max(m_blk, axis=0, keepdims=True)
            if not own:
                m_new = jnp.maximum(m_new, m_scr[hh])
            ps = [jnp.exp2(part - m_new) for part in parts]
            l_new = ps[0]
            for t in range(1, n_blk):
                l_new = l_new + ps[t]
            m_news.append(m_new)
            l_news.append(jnp.sum(l_new, axis=0, keepdims=True))
            probs.append(jnp.concatenate([p.astype(BF16) for p in ps], axis=0) if n_blk > 1
                         else ps[0].astype(BF16))
        pvs = [_dot(vt_scr[hh, c, :, 0:n_blk * blk], probs[hh]) for hh in range(n_grp)]
        for hh in range(n_grp):
            if own:
                l_scr[hh] = l_news[hh]
                acc_scr[hh] = pvs[hh]
            else:
                alpha = jnp.exp2(m_scr[hh] - m_news[hh])
                l_scr[hh] = alpha * l_scr[hh] + l_news[hh]
                acc_scr[hh] = alpha * acc_scr[hh] + pvs[hh]
            m_scr[hh] = m_news[hh]

    c_own = i // ch
    for hh in range(n_grp):
        q = q_ref[0, :, hh * hd:(hh + 1) * hd]
        gate2 = _dot_nt(kmean_scr[hh], q)
        gate = gate2[:nb] + gate2[nb:]
        blk_id = lax.broadcasted_iota(jnp.int32, gate.shape, 0)
        past = blk_id < i
        g = jnp.where(past, gate, -jnp.inf)
        rank = jnp.zeros(gate.shape, jnp.int32)
        for m in range(nb):
            gm = g[m:m + 1, :]
            beats = jnp.where(gm > g, 1, jnp.where(gm == g, jnp.where(blk_id > m, 1, 0), 0))
            rank = rank + beats
        bias_scr[hh] = jnp.where(past, jnp.where(rank < MOBA_TOPK, 0.0, NEG_BIG), NEG_BIG)

    for r in range(ch):
        @pl.when(i % ch == r)
        def _():
            sweep(c_own, r + 1)

    def body(c, carry):
        sweep(c, 0)
        return carry

    lax.fori_loop(0, c_own, body, 0)
    for hh in range(n_grp):
        o_ref[0, :, hh * hd:(hh + 1) * hd] = (acc_scr[hh] / l_scr[hh]).T.astype(BF16)


def _moba_attention(qkv, d, n_grp, ch):
    b, s, _ = qkv.shape
    h = d // HEAD_DIM
    blk = MOBA_BLOCK
    nb = s // blk
    gw = n_grp * HEAD_DIM
    ng = h // n_grp
    return pl.pallas_call(
        functools.partial(_moba_kernel, nb=nb, n_grp=n_grp, ch=ch),
        grid=(b, ng, nb),
        in_specs=[pl.BlockSpec((1, blk, gw), lambda bi, hi, i: (bi, i, hi)),
                  pl.BlockSpec((1, s, gw), lambda bi, hi, i: (bi, 0, ng + hi)),
                  pl.BlockSpec((1, s, gw), lambda bi, hi, i: (bi, 0, 2 * ng + hi))],
        out_specs=pl.BlockSpec((1, blk, gw), lambda bi, hi, i: (bi, i, hi)),
        out_shape=jax.ShapeDtypeStruct((b, s, d), BF16),
        scratch_shapes=[pltpu.VMEM((n_grp, 2 * nb, HEAD_DIM), BF16),
                        pltpu.VMEM((n_grp, nb // ch, HEAD_DIM, ch * blk), BF16),
                        pltpu.VMEM((n_grp, nb, blk), F32),
                        pltpu.VMEM((n_grp, 1, blk), F32),
                        pltpu.VMEM((n_grp, 1, blk), F32),
                        pltpu.VMEM((n_grp, HEAD_DIM, blk), F32)],
        compiler_params=_params("arbitrary", "arbitrary", "arbitrary"),
        name="moba_attention",
    )(qkv, qkv, qkv)


def _store_row_tiles(ref, val):
    rows, d = val.shape
    sub = d // LANES
    for c in range(sub):
        ref[pl.ds(c, rows, stride=sub), :] = val[:, c * LANES:(c + 1) * LANES]


def _load_row_tiles(ref, rows, dtype):
    sub = ref.shape[0] // rows
    return jnp.concatenate([ref[pl.ds(c, rows, stride=sub), :].astype(dtype) for c in range(sub)], axis=1)


def _route_top2(u, wr_ref):
    u_hi = u.astype(BF16)
    u_lo = (u - u_hi.astype(F32)).astype(BF16)
    both = _dot(u_hi, wr_ref[...])
    logits = both[:, :LANES] + both[:, LANES:] + _dot(u_lo, wr_ref[:, :LANES])
    lane = lax.broadcasted_iota(jnp.int32, logits.shape, 1)
    logits = jnp.where(lane < N_EXPERTS, logits, -jnp.inf)
    m1 = jnp.max(logits, axis=-1, keepdims=True)
    i1 = jnp.min(jnp.where(logits == m1, lane, LANES), axis=-1, keepdims=True)
    rest = jnp.where(lane == i1, -jnp.inf, logits)
    m2 = jnp.max(rest, axis=-1, keepdims=True)
    i2 = jnp.min(jnp.where(rest == m2, lane, LANES), axis=-1, keepdims=True)
    e2 = jnp.exp(m2 - m1)
    w1 = 1.0 / (1.0 + e2)
    w2 = e2 / (1.0 + e2)
    return jnp.where(lane == 0, i1.astype(F32),
                     jnp.where(lane == 1, i2.astype(F32),
                               jnp.where(lane == 2, w1, jnp.where(lane == 3, w2, 0.0))))


def _proj_ln_route_kernel(a_ref, w_ref, x_ref, g1p_ref, lng_ref, lnb_ref, sc_ref, sh_ref, wr_ref,
                          xo_ref, uo_ref, ro_ref, *, alpha):
    y = _dot(a_ref[0], w_ref[...])
    xn = _layer_norm(alpha * x_ref[0] + g1p_ref[0] * y, lng_ref[...], lnb_ref[...])
    xo_ref[0] = xn
    u = xn * sc_ref[0] + sh_ref[0]
    _store_row_tiles(uo_ref.at[0], u)
    ro_ref[0] = _route_top2(u, wr_ref)


def _proj_ln_route(a, w, x, g1p, lng, lnb, sc, sh, wr, alpha, tm):
    b, s, d = x.shape
    sub = d // LANES
    tile = pl.BlockSpec((1, tm, d), lambda i, j: (i, j, 0))
    per_b = pl.BlockSpec((1, 1, d), lambda i, j: (i, 0, 0))
    vec = pl.BlockSpec((1, d), lambda i, j: (0, 0))
    return pl.pallas_call(
        functools.partial(_proj_ln_route_kernel, alpha=alpha),
        grid=(b, s // tm),
        in_specs=[tile, pl.BlockSpec((d, d), lambda i, j: (0, 0)), tile, per_b, vec, vec, per_b, per_b,
                  pl.BlockSpec((d, 2 * LANES), lambda i, j: (0, 0))],
        out_specs=[tile, pl.BlockSpec((1, tm * sub, LANES), lambda i, j: (i, j, 0)),
                   pl.BlockSpec((1, tm, LANES), lambda i, j: (i, j, 0))],
        out_shape=[jax.ShapeDtypeStruct((b, s, d), F32), jax.ShapeDtypeStruct((b, s * sub, LANES), F32),
                   jax.ShapeDtypeStruct((b, s, LANES), F32)],
        compiler_params=_params("arbitrary", "arbitrary"),
        name="proj_ln_route",
    )(a, w, x, g1p, lng, lnb, sc, sh, wr)


def _proj_ffn_kernel(a_ref, wo_ref, x_ref, g1m_ref, lng_ref, lnb_ref, sc_ref, sh_ref, wg_ref, wu_ref, wd_ref,
                     g1f_ref, xo_ref, h_scr, *, alpha, tf):
    x1 = _layer_norm(alpha * x_ref[0] + g1m_ref[0] * _dot(a_ref[0], wo_ref[...]), lng_ref[0], lnb_ref[0])
    u = (x1 * sc_ref[0] + sh_ref[0]).astype(BF16)
    f = wg_ref.shape[1]
    for j in range(f // tf):
        cols = slice(j * tf, (j + 1) * tf)
        hj = _silu(_dot(u, wg_ref[:, cols])) * _dot(u, wu_ref[:, cols])
        h_scr[:, cols] = hj.astype(BF16)
    y = _dot(h_scr[...], wd_ref[...])
    xo_ref[0] = _layer_norm(alpha * x1 + g1f_ref[0] * y, lng_ref[1], lnb_ref[1])


def _proj_ffn_dense(a, wo, x, g1m, lng, lnb, sc, sh, wg, wu, wd, g1f, alpha, tm):
    b, s, d = x.shape
    f = wg.shape[1]
    tile = pl.BlockSpec((1, tm, d), lambda i, j: (i, j, 0))
    per_b = pl.BlockSpec((1, 1, d), lambda i, j: (i, 0, 0))
    ln_pair = pl.BlockSpec((2, 1, d), lambda i, j: (0, 0, 0))
    resident = dict(pipeline_mode=pl.Buffered(1))
    return pl.pallas_call(
        functools.partial(_proj_ffn_kernel, alpha=alpha, tf=MXU_N),
        grid=(b, s // tm),
        in_specs=[tile, pl.BlockSpec((d, d), lambda i, j: (0, 0), **resident), tile, per_b, ln_pair, ln_pair,
                  per_b, per_b,
                  pl.BlockSpec((d, f), lambda i, j: (0, 0), **resident),
                  pl.BlockSpec((d, f), lambda i, j: (0, 0), **resident),
                  pl.BlockSpec((f, d), lambda i, j: (0, 0), **resident),
                  per_b],
        out_specs=tile,
        out_shape=jax.ShapeDtypeStruct((b, s, d), F32),
        scratch_shapes=[pltpu.VMEM((tm, f), BF16)],
        compiler_params=_params("arbitrary", "arbitrary"),
        name="proj_ffn_dense",
    )(a, wo, x, g1m, lng, lnb, sc, sh, wg, wu, wd, g1f)


def _rec_in_kernel(x_ref, sc_ref, sh_ref, w_ref, lbl_ref, qf_ref, key_ref, v_ref, gs_ref, lf_ref,
                   *, layer_idx):
    u = (x_ref[0] * sc_ref[0] + sh_ref[0]).astype(BF16)
    d = x_ref.shape[2]
    lbl = lbl_ref[...]
    e = jnp.exp(lbl - jnp.max(lbl, axis=0, keepdims=True))
    sm = e / jnp.sum(e, axis=0, keepdims=True)
    lb = jnp.zeros((1, d), F32)
    for r in range(1, layer_idx + 1):
        lb = lb + sm[r:r + 1, :]
    for c in range(d // MXU_N):
        cols = slice(c * MXU_N, (c + 1) * MXU_N)
        q = _dot(u, w_ref[:, c * MXU_N:(c + 1) * MXU_N])
        qf_ref[0, :, cols] = _silu(q).astype(BF16)
        f = _dot(u, w_ref[:, d + c * MXU_N:d + (c + 1) * MXU_N])
        lbc = lb[:, cols]
        f_gate = lbc + (1.0 - lbc) * jax.nn.sigmoid(f)
        lf_ref[0, :, cols] = jnp.log(f_gate)
        key_ref[0, :, cols] = (1.0 - f_gate).astype(BF16)
        v = _dot(u, w_ref[:, 2 * d + c * MXU_N:2 * d + (c + 1) * MXU_N])
        v_ref[0, :, cols] = v.astype(BF16)
        g = _dot(u, w_ref[:, 3 * d + c * MXU_N:3 * d + (c + 1) * MXU_N])
        gs_ref[0, :, cols] = _silu(g).astype(BF16)


def _rec_in(x, sc, sh, w, lb_logits, layer_idx, ts):
    b, s, d = x.shape
    depth = lb_logits.shape[0]
    tile = pl.BlockSpec((1, ts, d), lambda i, j: (i, j, 0))
    per_b = pl.BlockSpec((1, 1, d), lambda i, j: (i, 0, 0))
    bf = jax.ShapeDtypeStruct((b, s, d), BF16)
    return pl.pallas_call(
        functools.partial(_rec_in_kernel, layer_idx=layer_idx),
        grid=(b, s // ts),
        in_specs=[tile, per_b, per_b,
                  pl.BlockSpec((d, 4 * d), lambda i, j: (0, 0)),
                  pl.BlockSpec((depth, d), lambda i, j: (0, 0))],
        out_specs=[tile] * 5,
        out_shape=[bf, bf, bf, bf, jax.ShapeDtypeStruct((b, s, d), F32)],
        compiler_params=_params("arbitrary", "arbitrary"),
        name="rec_in",
    )(x, sc, sh, w, lb_logits)


def _hgrn_kernel(qf_ref, key_ref, v_ref, lf_ref, gs_ref, nw_ref, o_ref, st_scr, *, n_heads):
    cs = HGRN_CHUNK
    dk = HGRN_EXPAND

    @pl.when(pl.program_id(1) == 0)
    def _():
        st_scr[...] = jnp.zeros(st_scr.shape, F32)

    row = lax.broadcasted_iota(jnp.int32, (cs, cs), 0)
    col = lax.broadcasted_iota(jnp.int32, (cs, cs), 1)
    causal = row >= col
    tri = jnp.where(causal, 1.0, 0.0).astype(BF16)
    n_chunks = qf_ref.shape[1] // cs

    def chunk(c, carry):
        rows = pl.ds(pl.multiple_of(c * cs, cs), cs)
        lf = lf_ref[0, rows, :]
        hi = lf.astype(BF16)
        r1 = lf - hi.astype(F32)
        mid = r1.astype(BF16)
        lo = (r1 - mid.astype(F32)).astype(BF16)
        g = _dot(tri, hi) + _dot(tri, mid) + _dot(tri, lo)
        g_last = g[cs - 1:cs, :]
        qf = qf_ref[0, rows, :].astype(F32)
        key = key_ref[0, rows, :].astype(F32)
        v = v_ref[0, rows, :]
        q_dec = (qf * jnp.exp(g)).astype(BF16)
        k_dec = (key * jnp.exp(-g)).astype(BF16)
        k_state = (key * jnp.exp(g_last - g)).astype(BF16)
        decay = jnp.exp(g_last)
        heads = [slice(h * dk, (h + 1) * dk) for h in range(n_heads)]
        a = [jnp.where(causal, _dot_nt(q_dec[:, c], k_dec[:, c]), 0.0).astype(BF16) for c in heads]
        st = [st_scr[h] for h in range(n_heads)]
        o = [_dot(a[h], v[:, c]) + _dot_nt(q_dec[:, c], st[h].astype(BF16))
             for h, c in enumerate(heads)]
        for h, c in enumerate(heads):
            v_t = v[:, c].astype(F32).T.astype(BF16)
            st_scr[h] = st[h] * decay[:, c] + _dot(v_t, k_state[:, c])
        on = jnp.concatenate(
            [oh * lax.rsqrt(jnp.mean(oh * oh, axis=-1, keepdims=True) + RMS_EPS) for oh in o], axis=1)
        o_ref[0, rows, :] = (on * nw_ref[...] * gs_ref[0, rows, :].astype(F32)).astype(BF16)
        return carry

    lax.fori_loop(0, n_chunks, chunk, 0, unroll=4)


def _hgrn_recurrence(qf, key, v, lf, gs, norm_w, ts):
    b, s, d = qf.shape
    h = d // HGRN_EXPAND
    tile = pl.BlockSpec((1, ts, d), lambda i, j: (i, j, 0))
    return pl.pallas_call(
        functools.partial(_hgrn_kernel, n_heads=h),
        grid=(b, s // ts),
        in_specs=[tile, tile, tile, tile, tile, pl.BlockSpec((1, d), lambda i, j: (0, 0))],
        out_specs=tile,
        out_shape=jax.ShapeDtypeStruct((b, s, d), BF16),
        scratch_shapes=[pltpu.VMEM((h, HGRN_EXPAND, HGRN_EXPAND), F32)],
        compiler_params=_params("arbitrary", "arbitrary"),
        name="hgrn_recurrence",
    )(qf, key, v, lf, gs, norm_w)


GATHER_UNROLL = 8


ROW_TILE = 8


def _start_row(idx_at, src_hbm, dst, sem, r):
    first = pl.multiple_of(idx_at(r), ROW_TILE)
    pltpu.make_async_copy(src_hbm.at[pl.ds(first, ROW_TILE), :],
                          dst.at[pl.ds(r * ROW_TILE, ROW_TILE), :], sem).start()


def _start_row_gather(idx_at, src_hbm, dst, sem, n_rows):
    def one(r, carry):
        _start_row(idx_at, src_hbm, dst, sem, r)
        return carry

    lax.fori_loop(0, n_rows, one, 0, unroll=GATHER_UNROLL)


def _wait_rows(buf, sem):
    pltpu.make_async_copy(buf, buf, sem).wait()


MOE_AHEAD = 2
MOE_XBUFS = MOE_AHEAD + 1


def _moe_ffn_kernel(te_ref, na_ref, src_t0_ref, src_t1_ref, src_ahead_ref, u_hbm, wg_ref, wu_ref, wd_ref,
                    o_ref, xbuf, xb_scr, h_scr, acc_scr, sems, *, tm, nf, n_tiles):
    i = pl.program_id(0)
    j = pl.program_id(1)
    rows_per_step = tm // nf
    n_active = na_ref[0]
    active = i < n_active
    slot = i % MOE_XBUFS
    ahead = (i + MOE_AHEAD) % MOE_XBUFS

    assert MOE_AHEAD == 2

    @pl.when((j == 0) & (i == 0))
    def _():
        _start_row_gather(lambda r: src_t0_ref[0, 0, r], u_hbm, xbuf.at[0], sems.at[0], tm)
        _start_row_gather(lambda r: src_t1_ref[0, 0, r], u_hbm, xbuf.at[1], sems.at[1], tm)

    @pl.when((j == 0) & (i <= n_active + 1))
    def _():
        _wait_rows(xbuf.at[slot], sems.at[slot])

    @pl.when(active & (j == 0))
    def _():
        xb_scr[...] = _load_row_tiles(xbuf.at[slot], tm, BF16)

    @pl.when(active)
    def _():
        for k in range(rows_per_step):
            _start_row(lambda r: src_ahead_ref[0, 0, r], u_hbm, xbuf.at[ahead], sems.at[ahead],
                       j * rows_per_step + k)
        x = xb_scr[...]
        for c in range(wg_ref.shape[3] // MXU_N):
            cols = slice(c * MXU_N, (c + 1) * MXU_N)
            hc = _silu(_dot(x, wg_ref[0, 0, :, cols])) * _dot(x, wu_ref[0, 0, :, cols])
            h_scr[:, cols] = hc.astype(BF16)
        y = _dot(h_scr[...], wd_ref[0])
        if nf == 1:
            _store_row_tiles(o_ref, y)
        else:
            @pl.when(j == 0)
            def _():
                acc_scr[...] = y

            if nf > 2:
                @pl.when((j > 0) & (j < nf - 1))
                def _():
                    acc_scr[...] += y

            @pl.when(j == nf - 1)
            def _():
                _store_row_tiles(o_ref, acc_scr[...] + y)

    @pl.when(jnp.logical_not(active) & (j == nf - 1) & (i < n_tiles))
    def _():
        o_ref[...] = jnp.zeros(o_ref.shape, F32)


def _moe_ffn(u_tiles, src, wg, wu, wd, tile_expert, n_active, tm, tf):
    d = wd.shape[2]
    sub = d // LANES
    assert sub == ROW_TILE
    n_tiles = src.shape[0] // tm
    assert tile_expert.shape[0] == n_tiles + MOE_AHEAD
    nf = wg.shape[1]
    assert wg.shape[3] == tf and wd.shape[1] == nf * tf

    def frozen(j, i, na):
        return jnp.where(i < na[0], j, nf - 1)

    def idx_block(tile_of):
        return pl.BlockSpec((1, 1, tm), lambda i, j, te, na: (jnp.minimum(tile_of(i), n_tiles - 1), 0, 0),
                            memory_space=pltpu.SMEM)

    grid_spec = pltpu.PrefetchScalarGridSpec(
        num_scalar_prefetch=2,
        grid=(n_tiles + MOE_AHEAD, nf),
        in_specs=[idx_block(lambda i: 0), idx_block(lambda i: 1), idx_block(lambda i: i + MOE_AHEAD),
                  pl.BlockSpec(memory_space=pl.ANY),
                  pl.BlockSpec((1, 1, d, tf), lambda i, j, te, na: (te[i], frozen(j, i, na), 0, 0)),
                  pl.BlockSpec((1, 1, d, tf), lambda i, j, te, na: (te[i], frozen(j, i, na), 0, 0)),
                  pl.BlockSpec((1, tf, d), lambda i, j, te, na: (te[i], frozen(j, i, na), 0))],
        out_specs=pl.BlockSpec((tm * sub, LANES), lambda i, j, te, na: (jnp.minimum(i, n_tiles - 1), 0)),
        scratch_shapes=[pltpu.VMEM((MOE_XBUFS, tm * sub, LANES), F32),
                        pltpu.VMEM((tm, d), BF16),
                        pltpu.VMEM((tm, tf), BF16),
                        pltpu.VMEM((tm, d), F32),
                        pltpu.SemaphoreType.DMA((MOE_XBUFS,))],
    )
    src3 = src.reshape(n_tiles, 1, tm)
    return pl.pallas_call(
        functools.partial(_moe_ffn_kernel, tm=tm, nf=nf, n_tiles=n_tiles),
        grid_spec=grid_spec,
        out_shape=jax.ShapeDtypeStruct((n_tiles * tm * sub, LANES), F32),
        compiler_params=_params("arbitrary", "arbitrary"),
        name="moe_ffn",
    )(tile_expert, n_active, src3, src3, src3, u_tiles, wg, wu, wd)


def _combine_ln_kernel(sa_cur, sb_cur, sa_nxt, sb_nxt, ys_hbm, r_ref, x_ref, g1p_ref, lng_ref, lnb_ref,
                       xo_ref, ybuf, sems, *, alpha, tm):
    i = pl.program_id(0)
    slot = i % 2

    def start(sa, sb, s):
        _start_row_gather(lambda r: sa[r], ys_hbm, ybuf.at[s, 0], sems.at[s], tm)
        _start_row_gather(lambda r: sb[r], ys_hbm, ybuf.at[s, 1], sems.at[s], tm)

    @pl.when(i == 0)
    def _():
        start(sa_cur, sb_cur, 0)

    @pl.when(i + 1 < pl.num_programs(0))
    def _():
        start(sa_nxt, sb_nxt, 1 - slot)

    _wait_rows(ybuf.at[slot], sems.at[slot])
    r = r_ref[...]
    y = (r[:, 2:3] * _load_row_tiles(ybuf.at[slot, 0], tm, F32)
         + r[:, 3:4] * _load_row_tiles(ybuf.at[slot, 1], tm, F32))
    xo_ref[...] = _layer_norm(alpha * x_ref[...] + g1p_ref[0] * y, lng_ref[...], lnb_ref[...])


def _combine_ln(ys_tiles, slot_a, slot_b, route, x, g1p, lng, lnb, alpha, tm):
    b, s, d = x.shape
    n = b * s
    sub = d // LANES
    n_tiles = n // tm
    per_seq = s // tm
    cur = pl.BlockSpec((tm,), lambda i: (i,), memory_space=pltpu.SMEM)
    nxt = pl.BlockSpec((tm,), lambda i: (jnp.minimum(i + 1, n_tiles - 1),), memory_space=pltpu.SMEM)
    tile = pl.BlockSpec((tm, d), lambda i: (i, 0))
    vec = pl.BlockSpec((1, d), lambda i: (0, 0))
    out = pl.pallas_call(
        functools.partial(_combine_ln_kernel, alpha=alpha, tm=tm),
        grid=(n_tiles,),
        in_specs=[cur, cur, nxt, nxt, pl.BlockSpec(memory_space=pl.ANY),
                  pl.BlockSpec((tm, LANES), lambda i: (i, 0)), tile,
                  pl.BlockSpec((1, 1, d), lambda i: (i // per_seq, 0, 0)), vec, vec],
        out_specs=tile,
        out_shape=jax.ShapeDtypeStruct((n, d), F32),
        scratch_shapes=[pltpu.VMEM((2, 2, tm * sub, LANES), F32), pltpu.SemaphoreType.DMA((2,))],
        compiler_params=_params("arbitrary"),
        name="combine_ln",
    )(slot_a, slot_b, slot_a, slot_b, ys_tiles, route.reshape(n, LANES), x.reshape(n, d), g1p, lng, lnb)
    return out.reshape(b, s, d)


def _moe_plan(route, tm):
    n = route.shape[0]
    e = N_EXPERTS
    n_tiles = (TOP_K * n + e * (tm - 1)) // tm
    flat_e = route[:, :TOP_K].astype(jnp.int32).reshape(-1)
    onehot = (flat_e[:, None] == jnp.arange(e, dtype=jnp.int32)[None, :]).astype(jnp.int32)
    incl = jnp.cumsum(onehot, axis=0)
    counts = incl[-1]
    padded = (counts + tm - 1) // tm * tm
    ends = jnp.cumsum(padded)
    offs = ends - padded
    slot = jnp.sum(onehot * (offs[None, :] + incl - 1), axis=1)
    n_active = ends[-1] // tm
    tile_id = jnp.minimum(jnp.arange(n_tiles + MOE_AHEAD, dtype=jnp.int32), n_active - 1)
    tile_expert = jnp.minimum(jnp.sum((tile_id[:, None] * tm >= ends[None, :]).astype(jnp.int32), axis=1), e - 1)
    by_expert = jnp.argsort(flat_e, stable=True).astype(jnp.int32) // TOP_K
    by_expert = jnp.pad(by_expert, (0, n_tiles * tm - TOP_K * n))
    row_expert = jnp.repeat(tile_expert[:n_tiles], tm)
    shift = offs - (jnp.cumsum(counts) - counts)
    src = jnp.zeros((n_tiles * tm,), jnp.int32)
    for k in range(e):
        src = jnp.where(row_expert == k, jnp.roll(by_expert, shift[k]), src)
    return ((slot * ROW_TILE).reshape(n, TOP_K), src * ROW_TILE, tile_expert.astype(jnp.int32),
            n_active.reshape(1).astype(jnp.int32))


def _moe_swiglu_ln(u_tiles, route, wg, wu, wd, x, g1p, lng, lnb, alpha, tiles):
    assert TOP_K == 2
    b, s, d = x.shape
    n = b * s
    nf = tiles.moe_slabs
    tf = wg.shape[2] // nf
    assert tiles.moe_rows % nf == 0
    slot, src, tile_expert, n_active = _moe_plan(route.reshape(n, LANES), tiles.moe_rows)
    slabs = lambda w: w.reshape(w.shape[0], d, nf, tf).transpose(0, 2, 1, 3)
    ys = _moe_ffn(u_tiles.reshape(n * (d // LANES), LANES), src, slabs(wg), slabs(wu), wd, tile_expert,
                  n_active, tiles.moe_rows, tf)
    return _combine_ln(ys, slot[:, 0], slot[:, 1], route, x, g1p, lng, lnb, alpha, tiles.token_rows)


class _Tiles(NamedTuple):
    token_rows: int
    attn_heads: int
    attn_chunk: int
    moe_rows: int
    moe_slabs: int


def _tiles(s, d, d_ff_expert):
    return _Tiles(token_rows=512,
                  attn_heads=math.gcd(d // HEAD_DIM, 8),
                  attn_chunk=math.gcd(s // MOBA_BLOCK, 4),
                  moe_rows=896,
                  moe_slabs=2 if d_ff_expert % (2 * MXU_N) == 0 else 1)


def kernel(x, c, positions, ada_w, ada_b, ln_g, ln_b, attn_w_in, attn_w_out, rec_w_in, rec_lb_logits,
           rec_norm_w, rec_w_out, ffn_w_gate, ffn_w_up, ffn_w_down, router_w, moe_w_gate, moe_w_up,
           moe_w_down):
    b, s, d = x.shape
    depth = ada_w.shape[0]
    alpha = (2.0 * depth) ** 0.25
    tiles = _tiles(s, d, moe_w_gate.shape[-1])
    ts = tiles.token_rows

    mods = _ada_mods(c, ada_w, ada_b).reshape(depth, b, 6, 1, d)
    one_plus = lambda t: 1.0 + t
    pos = positions.reshape(b, s, 1)

    for i in range(depth):
        shift_m, scale_m, gate_m, shift_f, scale_f, gate_f = (mods[i, :, r] for r in range(6))
        j = i // 2
        lng = ln_g[i].reshape(2, 1, d)
        lnb = ln_b[i].reshape(2, 1, d)
        if i % 2 == 0:
            qkv = _qkv_proj(x, one_plus(scale_m), shift_m, pos, attn_w_in[j].astype(BF16), ts)
            o = _moba_attention(qkv, d, n_grp=tiles.attn_heads, ch=tiles.attn_chunk)
            x = _proj_ffn_dense(o, attn_w_out[j].astype(BF16), x, one_plus(gate_m), lng, lnb,
                                one_plus(scale_f), shift_f, ffn_w_gate[j].astype(BF16),
                                ffn_w_up[j].astype(BF16), ffn_w_down[j].astype(BF16), one_plus(gate_f),
                                alpha, ts)
        else:
            qf, key, v, gs, lf = _rec_in(x, one_plus(scale_m), shift_m, rec_w_in[j].astype(BF16),
                                         rec_lb_logits, i, ts)
            o = _hgrn_recurrence(qf, key, v, lf, gs, rec_norm_w[j].reshape(1, d), ts)
            w_hi = router_w[j].astype(BF16)
            w_lo = (router_w[j] - w_hi.astype(F32)).astype(BF16)
            wr = (jnp.zeros((d, 2 * LANES), BF16).at[:, :N_EXPERTS].set(w_hi)
                  .at[:, LANES:LANES + N_EXPERTS].set(w_lo))
            x, u, route = _proj_ln_route(o, rec_w_out[j].astype(BF16), x, one_plus(gate_m), lng[0], lnb[0],
                                         one_plus(scale_f), shift_f, wr, alpha, ts)
            x = _moe_swiglu_ln(u, route, moe_w_gate[j].astype(BF16), moe_w_up[j].astype(BF16),
                               moe_w_down[j].astype(BF16), x, one_plus(gate_f), lng[1], lnb[1], alpha, tiles)
    return x
```

```python
import functools
import math
from typing import NamedTuple

import jax
import jax.numpy as jnp
from jax import lax
from jax.experimental import pallas as pl
from jax.experimental.pallas import tpu as pltpu

HEAD_DIM = 128
ROPE_DIM = HEAD_DIM // 4
ROPE_THETA = 500000.0
MOBA_BLOCK = 256
MOBA_TOPK = 3
HGRN_EXPAND = 128
HGRN_CHUNK = 64
N_EXPERTS = 8
TOP_K = 2
LN_EPS = 1e-5
RMS_EPS = 1e-6

LANES = 128
MXU_N = 256
VMEM_LIMIT = 56 * 1024 * 1024
NEG_BIG = -1e30

F32 = jnp.float32
BF16 = jnp.bfloat16
HIGHEST = lax.Precision.HIGHEST


def _params(*sem):
    return pltpu.CompilerParams(dimension_semantics=sem, vmem_limit_bytes=VMEM_LIMIT)


def _dot(a, b):
    return jnp.dot(a, b, preferred_element_type=F32)


def _dot_nt(a, b, precision=None):
    return lax.dot_general(a, b, (((1,), (1,)), ((), ())), precision=precision,
                           preferred_element_type=F32)


def _silu(x):
    return x * jax.nn.sigmoid(x)


def _layer_norm(z, g, b):
    mu = jnp.mean(z, axis=-1, keepdims=True)
    d = z - mu
    var = jnp.mean(d * d, axis=-1, keepdims=True)
    return d * lax.rsqrt(var + LN_EPS) * g + b


def _ada_kernel(c_ref, w_ref, b_ref, o_ref):
    a = _silu(c_ref[...])
    o_ref[0] = jnp.dot(a, w_ref[0], precision=HIGHEST, preferred_element_type=F32) + b_ref[0]


def _ada_mods(c, ada_w, ada_b):
    depth, d, m = ada_w.shape
    b = c.shape[0]
    tn = m // 4
    return pl.pallas_call(
        _ada_kernel,
        grid=(depth, m // tn),
        in_specs=[pl.BlockSpec((b, d), lambda l, j: (0, 0)),
                  pl.BlockSpec((1, d, tn), lambda l, j: (l, 0, j)),
                  pl.BlockSpec((1, 1, tn), lambda l, j: (l, 0, j))],
        out_specs=pl.BlockSpec((1, b, tn), lambda l, j: (l, 0, j)),
        out_shape=jax.ShapeDtypeStruct((depth, b, m), F32),
        compiler_params=_params("arbitrary", "arbitrary"),
        name="ada_mods",
    )(c, ada_w, ada_b.reshape(depth, 1, m))


def _qkv_kernel(x_ref, sc_ref, sh_ref, pos_ref, w_ref, o_ref, trig_scr, *, n_rot):
    u = (x_ref[0] * sc_ref[0] + sh_ref[0]).astype(BF16)
    half = ROPE_DIM // 2
    pack = LANES // ROPE_DIM
    rows = x_ref.shape[1] // pack
    lane = lax.broadcasted_iota(jnp.int32, (1, LANES), 1)
    inv = jnp.exp(-math.log(ROPE_THETA) * (lane % half).astype(F32) * (2.0 / ROPE_DIM))
    pos = jnp.zeros((rows, LANES), F32)
    for m in range(pack):
        pos = jnp.where(lane // ROPE_DIM == m, pos_ref[0, pl.ds(m, rows, stride=pack), :].astype(F32), pos)
    ang = pos * inv
    for t, trig in enumerate((jnp.cos(ang), jnp.sin(ang))):
        for m in range(pack):
            moved = trig if m == 0 else pltpu.roll(trig, LANES - ROPE_DIM * m, 1)
            trig_scr[t, pl.ds(m, rows, stride=pack), :] = moved
    cos_t = jnp.where(lane < ROPE_DIM, trig_scr[0], 1.0)
    sin_t = jnp.where(lane < ROPE_DIM, trig_scr[1], 0.0)
    sin_lo = jnp.where(lane < half, -sin_t, 0.0)
    sin_hi = jnp.where((lane >= half) & (lane < ROPE_DIM), sin_t, 0.0)
    n_cols = w_ref.shape[1]
    for c in range(n_cols // MXU_N):
        y = _dot(u, w_ref[:, c * MXU_N:(c + 1) * MXU_N])
        for s in range(MXU_N // HEAD_DIM):
            col = c * MXU_N + s * HEAD_DIM
            t = y[:, s * HEAD_DIM:(s + 1) * HEAD_DIM]
            if col < n_rot:
                t = (t * cos_t + pltpu.roll(t, HEAD_DIM - half, 1) * sin_lo
                     + pltpu.roll(t, half, 1) * sin_hi)
            if col < n_rot // 2:
                t = t * (HEAD_DIM ** -0.5 * math.log2(math.e))
            o_ref[0, :, col:col + HEAD_DIM] = t.astype(BF16)


def _qkv_proj(x, sc, sh, pos, w, ts):
    b, s, d = x.shape
    n = w.shape[1]
    return pl.pallas_call(
        functools.partial(_qkv_kernel, n_rot=2 * d),
        grid=(b, s // ts),
        in_specs=[pl.BlockSpec((1, ts, d), lambda i, j: (i, j, 0)),
                  pl.BlockSpec((1, 1, d), lambda i, j: (i, 0, 0)),
                  pl.BlockSpec((1, 1, d), lambda i, j: (i, 0, 0)),
                  pl.BlockSpec((1, ts, 1), lambda i, j: (i, j, 0)),
                  pl.BlockSpec((d, n), lambda i, j: (0, 0))],
        out_specs=pl.BlockSpec((1, ts, n), lambda i, j: (i, j, 0)),
        out_shape=jax.ShapeDtypeStruct((b, s, n), BF16),
        scratch_shapes=[pltpu.VMEM((2, ts, LANES), F32)],
        compiler_params=_params("arbitrary", "arbitrary"),
        name="qkv_proj",
    )(x, sc, sh, pos, w)


def _moba_kernel(q_ref, k_ref, v_ref, o_ref, kmean_scr, vt_scr, bias_scr, m_scr, l_scr, acc_scr,
                 *, nb, n_grp, ch):
    blk = MOBA_BLOCK
    hd = HEAD_DIM
    cw = ch * blk
    i = pl.program_id(2)

    @pl.when(i == 0)
    def _():
        for hh in range(n_grp):
            means = []
            for n in range(nb):
                kb = k_ref[0, n * blk:(n + 1) * blk, hh * hd:(hh + 1) * hd].astype(F32)
                means.append(jnp.mean(kb, axis=0, keepdims=True))
                vb = v_ref[0, n * blk:(n + 1) * blk, hh * hd:(hh + 1) * hd]
                vt_scr[hh, n // ch, :, (n % ch) * blk:(n % ch + 1) * blk] = vb.astype(F32).T.astype(BF16)
            km = jnp.concatenate(means, axis=0)
            km_hi = km.astype(BF16)
            km_lo = (km - km_hi.astype(F32)).astype(BF16)
            kmean_scr[hh] = jnp.concatenate([km_hi, km_lo], axis=0)

    def sweep(c, n_own):
        own = n_own > 0
        n_blk = n_own if own else ch
        k0 = pl.multiple_of(c * cw, cw)
        heads = [slice(hh * hd, (hh + 1) * hd) for hh in range(n_grp)]
        scores = [_dot_nt(k_ref[0, pl.ds(k0, n_blk * blk), cols], q_ref[0, :, cols])
                  for cols in heads]
        m_news, l_news, probs = [], [], []
        for hh, s in enumerate(scores):
            parts = [s[t * blk:(t + 1) * blk] + bias_scr[hh, pl.ds(c * ch + t, 1), :]
                     for t in range(n_blk - 1 if own else n_blk)]
            if own:
                s_own = s[(n_blk - 1) * blk:]
                kpos = lax.broadcasted_iota(jnp.int32, s_own.shape, 0)
                qpos = lax.broadcasted_iota(jnp.int32, s_own.shape, 1)
                parts.append(jnp.where(kpos <= qpos, s_own, NEG_BIG))
            m_blk = parts[0]
            for t in range(1, n_blk):
                m_blk = jnp.maximum(m_blk, parts[t])
            m_new = jnp.max(m_blk, axis=0, keepdims=True)
            if not own:
                m_new = jnp.maximum(m_new, m_scr[hh])
            ps = [jnp.exp2(part - m_new) for part in parts]
            l_new = ps[0]
            for t in range(1, n_blk):
                l_new = l_new + ps[t]
            m_news.append(m_new)
            l_news.append(jnp.sum(l_new, axis=0, keepdims=True))
            probs.append(jnp.concatenate([p.astype(BF16) for p in ps], axis=0) if n_blk > 1
                         else ps[0].astype(BF16))
        pvs = [_dot(vt_scr[hh, c, :, 0:n_blk * blk], probs[hh]) for hh in range(n_grp)]
        for hh in range(n_grp):
            if own:
                l_scr[hh] = l_news[hh]
                acc_scr[hh] = pvs[hh]
            else:
                alpha = jnp.exp2(m_scr[hh] - m_news[hh])
                l_scr[hh] = alpha * l_scr[hh] + l_news[hh]
                acc_scr[hh] = alpha * acc_scr[hh] + pvs[hh]
            m_scr[hh] = m_news[hh]

    c_own = i // ch
    for hh in range(n_grp):
        q = q_ref[0, :, hh * hd:(hh + 1) * hd]
        gate2 = _dot_nt(kmean_scr[hh], q)
        gate = gate2[:nb] + gate2[nb:]
        blk_id = lax.broadcasted_iota(jnp.int32, gate.shape, 0)
        past = blk_id < i
        g = jnp.where(past, gate, -jnp.inf)
        rank = jnp.zeros(gate.shape, jnp.int32)
        for m in range(nb):
            gm = g[m:m + 1, :]
            beats = jnp.where(gm > g, 1, jnp.where(gm == g, jnp.where(blk_id > m, 1, 0), 0))
            rank = rank + beats
        bias_scr[hh] = jnp.where(past, jnp.where(rank < MOBA_TOPK, 0.0, NEG_BIG), NEG_BIG)

    for r in range(ch):
        @pl.when(i % ch == r)
        def _():
            sweep(c_own, r + 1)

    def body(c, carry):
        sweep(c, 0)
        return carry

    lax.fori_loop(0, c_own, body, 0)
    for hh in range(n_grp):
        o_ref[0, :, hh * hd:(hh + 1) * hd] = (acc_scr[hh] / l_scr[hh]).T.astype(BF16)


def _moba_attention(qkv, d, n_grp, ch):
    b, s, _ = qkv.shape
    h = d // HEAD_DIM
    blk = MOBA_BLOCK
    nb = s // blk
    gw = n_grp * HEAD_DIM
    ng = h // n_grp
    return pl.pallas_call(
        functools.partial(_moba_kernel, nb=nb, n_grp=n_grp, ch=ch),
        grid=(b, ng, nb),
        in_specs=[pl.BlockSpec((1, blk, gw), lambda bi, hi, i: (bi, i, hi)),
                  pl.BlockSpec((1, s, gw), lambda bi, hi, i: (bi, 0, ng + hi)),
                  pl.BlockSpec((1, s, gw), lambda bi, hi, i: (bi, 0, 2 * ng + hi))],
        out_specs=pl.BlockSpec((1, blk, gw), lambda bi, hi, i: (bi, i, hi)),
        out_shape=jax.ShapeDtypeStruct((b, s, d), BF16),
        scratch_shapes=[pltpu.VMEM((n_grp, 2 * nb, HEAD_DIM), BF16),
                        pltpu.VMEM((n_grp, nb // ch, HEAD_DIM, ch * blk), BF16),
                        pltpu.VMEM((n_grp, nb, blk), F32),
                        pltpu.VMEM((n_grp, 1, blk), F32),
                        pltpu.VMEM((n_grp, 1, blk), F32),
                        pltpu.VMEM((n_grp, HEAD_DIM, blk), F32)],
        compiler_params=_params("arbitrary", "arbitrary", "arbitrary"),
        name="moba_attention",
    )(qkv, qkv, qkv)


def _store_row_tiles(ref, val):
    rows, d = val.shape
    sub = d // LANES
    for c in range(sub):
        ref[pl.ds(c, rows, stride=sub), :] = val[:, c * LANES:(c + 1) * LANES]


def _load_row_tiles(ref, rows, dtype):
    sub = ref.shape[0] // rows
    return jnp.concatenate([ref[pl.ds(c, rows, stride=sub), :].astype(dtype) for c in range(sub)], axis=1)


def _route_top2(u, wr_ref):
    u_hi = u.astype(BF16)
    u_lo = (u - u_hi.astype(F32)).astype(BF16)
    both = _dot(u_hi, wr_ref[...])
    logits = both[:, :LANES] + both[:, LANES:] + _dot(u_lo, wr_ref[:, :LANES])
    lane = lax.broadcasted_iota(jnp.int32, logits.shape, 1)
    logits = jnp.where(lane < N_EXPERTS, logits, -jnp.inf)
    m1 = jnp.max(logits, axis=-1, keepdims=True)
    i1 = jnp.min(jnp.where(logits == m1, lane, LANES), axis=-1, keepdims=True)
    rest = jnp.where(lane == i1, -jnp.inf, logits)
    m2 = jnp.max(rest, axis=-1, keepdims=True)
    i2 = jnp.min(jnp.where(rest == m2, lane, LANES), axis=-1, keepdims=True)
    e2 = jnp.exp(m2 - m1)
    w1 = 1.0 / (1.0 + e2)
    w2 = e2 / (1.0 + e2)
    return jnp.where(lane == 0, i1.astype(F32),
                     jnp.where(lane == 1, i2.astype(F32),
                               jnp.where(lane == 2, w1, jnp.where(lane == 3, w2, 0.0))))


def _proj_ln_route_kernel(a_ref, w_ref, x_ref, g1p_ref, lng_ref, lnb_ref, sc_ref, sh_ref, wr_ref,
                          xo_ref, uo_ref, ro_ref, *, alpha):
    y = _dot(a_ref[0], w_ref[...])
    xn = _layer_norm(alpha * x_ref[0] + g1p_ref[0] * y, lng_ref[...], lnb_ref[...])
    xo_ref[0] = xn
    u = xn * sc_ref[0] + sh_ref[0]
    _store_row_tiles(uo_ref.at[0], u)
    ro_ref[0] = _route_top2(u, wr_ref)


def _proj_ln_route(a, w, x, g1p, lng, lnb, sc, sh, wr, alpha, tm):
    b, s, d = x.shape
    sub = d // LANES
    tile = pl.BlockSpec((1, tm, d), lambda i, j: (i, j, 0))
    per_b = pl.BlockSpec((1, 1, d), lambda i, j: (i, 0, 0))
    vec = pl.BlockSpec((1, d), lambda i, j: (0, 0))
    return pl.pallas_call(
        functools.partial(_proj_ln_route_kernel, alpha=alpha),
        grid=(b, s // tm),
        in_specs=[tile, pl.BlockSpec((d, d), lambda i, j: (0, 0)), tile, per_b, vec, vec, per_b, per_b,
                  pl.BlockSpec((d, 2 * LANES), lambda i, j: (0, 0))],
        out_specs=[tile, pl.BlockSpec((1, tm * sub, LANES), lambda i, j: (i, j, 0)),
                   pl.BlockSpec((1, tm, LANES), lambda i, j: (i, j, 0))],
        out_shape=[jax.ShapeDtypeStruct((b, s, d), F32), jax.ShapeDtypeStruct((b, s * sub, LANES), F32),
                   jax.ShapeDtypeStruct((b, s, LANES), F32)],
        compiler_params=_params("arbitrary", "arbitrary"),
        name="proj_ln_route",
    )(a, w, x, g1p, lng, lnb, sc, sh, wr)


def _proj_ffn_kernel(a_ref, wo_ref, x_ref, g1m_ref, lng_ref, lnb_ref, sc_ref, sh_ref, wg_ref, wu_ref, wd_ref,
                     g1f_ref, xo_ref, h_scr, *, alpha, tf):
    x1 = _layer_norm(alpha * x_ref[0] + g1m_ref[0] * _dot(a_ref[0], wo_ref[...]), lng_ref[0], lnb_ref[0])
    u = (x1 * sc_ref[0] + sh_ref[0]).astype(BF16)
    f = wg_ref.shape[1]
    for j in range(f // tf):
        cols = slice(j * tf, (j + 1) * tf)
        hj = _silu(_dot(u, wg_ref[:, cols])) * _dot(u, wu_ref[:, cols])
        h_scr[:, cols] = hj.astype(BF16)
    y = _dot(h_scr[...], wd_ref[...])
    xo_ref[0] = _layer_norm(alpha * x1 + g1f_ref[0] * y, lng_ref[1], lnb_ref[1])


def _proj_ffn_dense(a, wo, x, g1m, lng, lnb, sc, sh, wg, wu, wd, g1f, alpha, tm):
    b, s, d = x.shape
    f = wg.shape[1]
    tile = pl.BlockSpec((1, tm, d), lambda i, j: (i, j, 0))
    per_b = pl.BlockSpec((1, 1, d), lambda i, j: (i, 0, 0))
    ln_pair = pl.BlockSpec((2, 1, d), lambda i, j: (0, 0, 0))
    resident = dict(pipeline_mode=pl.Buffered(1))
    return pl.pallas_call(
        functools.partial(_proj_ffn_kernel, alpha=alpha, tf=MXU_N),
        grid=(b, s // tm),
        in_specs=[tile, pl.BlockSpec((d, d), lambda i, j: (0, 0), **resident), tile, per_b, ln_pair, ln_pair,
                  per_b, per_b,
                  pl.BlockSpec((d, f), lambda i, j: (0, 0), **resident),
                  pl.BlockSpec((d, f), lambda i, j: (0, 0), **resident),
                  pl.BlockSpec((f, d), lambda i, j: (0, 0), **resident),
                  per_b],
        out_specs=tile,
        out_shape=jax.ShapeDtypeStruct((b, s, d), F32),
        scratch_shapes=[pltpu.VMEM((tm, f), BF16)],
        compiler_params=_params("arbitrary", "arbitrary"),
        name="proj_ffn_dense",
    )(a, wo, x, g1m, lng, lnb, sc, sh, wg, wu, wd, g1f)


def _rec_in_kernel(x_ref, sc_ref, sh_ref, w_ref, lbl_ref, qf_ref, key_ref, v_ref, gs_ref, lf_ref,
                   *, layer_idx):
    u = (x_ref[0] * sc_ref[0] + sh_ref[0]).astype(BF16)
    d = x_ref.shape[2]
    lbl = lbl_ref[...]
    e = jnp.exp(lbl - jnp.max(lbl, axis=0, keepdims=True))
    sm = e / jnp.sum(e, axis=0, keepdims=True)
    lb = jnp.zeros((1, d), F32)
    for r in range(1, layer_idx + 1):
        lb = lb + sm[r:r + 1, :]
    for c in range(d // MXU_N):
        cols = slice(c * MXU_N, (c + 1) * MXU_N)
        q = _dot(u, w_ref[:, c * MXU_N:(c + 1) * MXU_N])
        qf_ref[0, :, cols] = _silu(q).astype(BF16)
        f = _dot(u, w_ref[:, d + c * MXU_N:d + (c + 1) * MXU_N])
        lbc = lb[:, cols]
        f_gate = lbc + (1.0 - lbc) * jax.nn.sigmoid(f)
        lf_ref[0, :, cols] = jnp.log(f_gate)
        key_ref[0, :, cols] = (1.0 - f_gate).astype(BF16)
        v = _dot(u, w_ref[:, 2 * d + c * MXU_N:2 * d + (c + 1) * MXU_N])
        v_ref[0, :, cols] = v.astype(BF16)
        g = _dot(u, w_ref[:, 3 * d + c * MXU_N:3 * d + (c + 1) * MXU_N])
        gs_ref[0, :, cols] = _silu(g).astype(BF16)


def _rec_in(x, sc, sh, w, lb_logits, layer_idx, ts):
    b, s, d = x.shape
    depth = lb_logits.shape[0]
    tile = pl.BlockSpec((1, ts, d), lambda i, j: (i, j, 0))
    per_b = pl.BlockSpec((1, 1, d), lambda i, j: (i, 0, 0))
    bf = jax.ShapeDtypeStruct((b, s, d), BF16)
    return pl.pallas_call(
        functools.partial(_rec_in_kernel, layer_idx=layer_idx),
        grid=(b, s // ts),
        in_specs=[tile, per_b, per_b,
                  pl.BlockSpec((d, 4 * d), lambda i, j: (0, 0)),
                  pl.BlockSpec((depth, d), lambda i, j: (0, 0))],
        out_specs=[tile] * 5,
        out_shape=[bf, bf, bf, bf, jax.ShapeDtypeStruct((b, s, d), F32)],
        compiler_params=_params("arbitrary", "arbitrary"),
        name="rec_in",
    )(x, sc, sh, w, lb_logits)


def _hgrn_kernel(qf_ref, key_ref, v_ref, lf_ref, gs_ref, nw_ref, o_ref, st_scr, *, n_heads):
    cs = HGRN_CHUNK
    dk = HGRN_EXPAND

    @pl.when(pl.program_id(1) == 0)
    def _():
        st_scr[...] = jnp.zeros(st_scr.shape, F32)

    row = lax.broadcasted_iota(jnp.int32, (cs, cs), 0)
    col = lax.broadcasted_iota(jnp.int32, (cs, cs), 1)
    causal = row >= col
    tri = jnp.where(causal, 1.0, 0.0).astype(BF16)
    n_chunks = qf_ref.shape[1] // cs

    def chunk(c, carry):
        rows = pl.ds(pl.multiple_of(c * cs, cs), cs)
        lf = lf_ref[0, rows, :]
        hi = lf.astype(BF16)
        r1 = lf - hi.astype(F32)
        mid = r1.astype(BF16)
        lo = (r1 - mid.astype(F32)).astype(BF16)
        g = _dot(tri, hi) + _dot(tri, mid) + _dot(tri, lo)
        g_last = g[cs - 1:cs, :]
        qf = qf_ref[0, rows, :].astype(F32)
        key = key_ref[0, rows, :].astype(F32)
        v = v_ref[0, rows, :]
        q_dec = (qf * jnp.exp(g)).astype(BF16)
        k_dec = (key * jnp.exp(-g)).astype(BF16)
        k_state = (key * jnp.exp(g_last - g)).astype(BF16)
        decay = jnp.exp(g_last)
        heads = [slice(h * dk, (h + 1) * dk) for h in range(n_heads)]
        a = [jnp.where(causal, _dot_nt(q_dec[:, c], k_dec[:, c]), 0.0).astype(BF16) for c in heads]
        st = [st_scr[h] for h in range(n_heads)]
        o = [_dot(a[h], v[:, c]) + _dot_nt(q_dec[:, c], st[h].astype(BF16))
             for h, c in enumerate(heads)]
        for h, c in enumerate(heads):
            v_t = v[:, c].astype(F32).T.astype(BF16)
            st_scr[h] = st[h] * decay[:, c] + _dot(v_t, k_state[:, c])
        on = jnp.concatenate(
            [oh * lax.rsqrt(jnp.mean(oh * oh, axis=-1, keepdims=True) + RMS_EPS) for oh in o], axis=1)
        o_ref[0, rows, :] = (on * nw_ref[...] * gs_ref[0, rows, :].astype(F32)).astype(BF16)
        return carry

    lax.fori_loop(0, n_chunks, chunk, 0, unroll=4)


def _hgrn_recurrence(qf, key, v, lf, gs, norm_w, ts):
    b, s, d = qf.shape
    h = d // HGRN_EXPAND
    tile = pl.BlockSpec((1, ts, d), lambda i, j: (i, j, 0))
    return pl.pallas_call(
        functools.partial(_hgrn_kernel, n_heads=h),
        grid=(b, s // ts),
        in_specs=[tile, tile, tile, tile, tile, pl.BlockSpec((1, d), lambda i, j: (0, 0))],
        out_specs=tile,
        out_shape=jax.ShapeDtypeStruct((b, s, d), BF16),
        scratch_shapes=[pltpu.VMEM((h, HGRN_EXPAND, HGRN_EXPAND), F32)],
        compiler_params=_params("arbitrary", "arbitrary"),
        name="hgrn_recurrence",
    )(qf, key, v, lf, gs, norm_w)


GATHER_UNROLL = 8


ROW_TILE = 8


def _start_row(idx_at, src_hbm, dst, sem, r):
    first = pl.multiple_of(idx_at(r), ROW_TILE)
    pltpu.make_async_copy(src_hbm.at[pl.ds(first, ROW_TILE), :],
                          dst.at[pl.ds(r * ROW_TILE, ROW_TILE), :], sem).start()


def _start_row_gather(idx_at, src_hbm, dst, sem, n_rows):
    def one(r, carry):
        _start_row(idx_at, src_hbm, dst, sem, r)
        return carry

    lax.fori_loop(0, n_rows, one, 0, unroll=GATHER_UNROLL)


def _wait_rows(buf, sem):
    pltpu.make_async_copy(buf, buf, sem).wait()


MOE_AHEAD = 2
MOE_XBUFS = MOE_AHEAD + 1


def _moe_ffn_kernel(te_ref, na_ref, src_t0_ref, src_t1_ref, src_ahead_ref, u_hbm, wg_ref, wu_ref, wd_ref,
                    o_ref, xbuf, xb_scr, h_scr, acc_scr, sems, *, tm, nf, n_tiles):
    i = pl.program_id(0)
    j = pl.program_id(1)
    rows_per_step = tm // nf
    n_active = na_ref[0]
    active = i < n_active
    slot = i % MOE_XBUFS
    ahead = (i + MOE_AHEAD) % MOE_XBUFS

    assert MOE_AHEAD == 2

    @pl.when((j == 0) & (i == 0))
    def _():
        _start_row_gather(lambda r: src_t0_ref[0, 0, r], u_hbm, xbuf.at[0], sems.at[0], tm)
        _start_row_gather(lambda r: src_t1_ref[0, 0, r], u_hbm, xbuf.at[1], sems.at[1], tm)

    @pl.when((j == 0) & (i <= n_active + 1))
    def _():
        _wait_rows(xbuf.at[slot], sems.at[slot])

    @pl.when(active & (j == 0))
    def _():
        xb_scr[...] = _load_row_tiles(xbuf.at[slot], tm, BF16)

    @pl.when(active)
    def _():
        for k in range(rows_per_step):
            _start_row(lambda r: src_ahead_ref[0, 0, r], u_hbm, xbuf.at[ahead], sems.at[ahead],
                       j * rows_per_step + k)
        x = xb_scr[...]
        for c in range(wg_ref.shape[2] // MXU_N):
            cols = slice(c * MXU_N, (c + 1) * MXU_N)
            hc = _silu(_dot(x, wg_ref[0, :, cols])) * _dot(x, wu_ref[0, :, cols])
            h_scr[:, cols] = hc.astype(BF16)
        y = _dot(h_scr[...], wd_ref[0])
        if nf == 1:
            _store_row_tiles(o_ref, y)
        else:
            @pl.when(j == 0)
            def _():
                acc_scr[...] = y

            if nf > 2:
                @pl.when((j > 0) & (j < nf - 1))
                def _():
                    acc_scr[...] += y

            @pl.when(j == nf - 1)
            def _():
                _store_row_tiles(o_ref, acc_scr[...] + y)

    @pl.when(jnp.logical_not(active) & (j == nf - 1) & (i < n_tiles))
    def _():
        o_ref[...] = jnp.zeros(o_ref.shape, F32)


def _moe_ffn(u_tiles, src, wg, wu, wd, tile_expert, n_active, tm, tf):
    d = wd.shape[2]
    sub = d // LANES
    assert sub == ROW_TILE
    n_tiles = src.shape[0] // tm
    assert tile_expert.shape[0] == n_tiles + MOE_AHEAD
    nf = wg.shape[2] // tf

    def frozen(j, i, na):
        return jnp.where(i < na[0], j, nf - 1)

    def idx_block(tile_of):
        return pl.BlockSpec((1, 1, tm), lambda i, j, te, na: (jnp.minimum(tile_of(i), n_tiles - 1), 0, 0),
                            memory_space=pltpu.SMEM)

    grid_spec = pltpu.PrefetchScalarGridSpec(
        num_scalar_prefetch=2,
        grid=(n_tiles + MOE_AHEAD, nf),
        in_specs=[idx_block(lambda i: 0), idx_block(lambda i: 1), idx_block(lambda i: i + MOE_AHEAD),
                  pl.BlockSpec(memory_space=pl.ANY),
                  pl.BlockSpec((1, d, tf), lambda i, j, te, na: (te[i], 0, frozen(j, i, na))),
                  pl.BlockSpec((1, d, tf), lambda i, j, te, na: (te[i], 0, frozen(j, i, na))),
                  pl.BlockSpec((1, tf, d), lambda i, j, te, na: (te[i], frozen(j, i, na), 0))],
        out_specs=pl.BlockSpec((tm * sub, LANES), lambda i, j, te, na: (jnp.minimum(i, n_tiles - 1), 0)),
        scratch_shapes=[pltpu.VMEM((MOE_XBUFS, tm * sub, LANES), F32),
                        pltpu.VMEM((tm, d), BF16),
                        pltpu.VMEM((tm, tf), BF16),
                        pltpu.VMEM((tm, d), F32),
                        pltpu.SemaphoreType.DMA((MOE_XBUFS,))],
    )
    src3 = src.reshape(n_tiles, 1, tm)
    return pl.pallas_call(
        functools.partial(_moe_ffn_kernel, tm=tm, nf=nf, n_tiles=n_tiles),
        grid_spec=grid_spec,
        out_shape=jax.ShapeDtypeStruct((n_tiles * tm * sub, LANES), F32),
        compiler_params=_params("arbitrary", "arbitrary"),
        name="moe_ffn",
    )(tile_expert, n_active, src3, src3, src3, u_tiles, wg, wu, wd)


def _combine_ln_kernel(sa_cur, sb_cur, sa_nxt, sb_nxt, ys_hbm, r_ref, x_ref, g1p_ref, lng_ref, lnb_ref,
                       xo_ref, ybuf, sems, *, alpha, tm):
    i = pl.program_id(0)
    slot = i % 2

    def start(sa, sb, s):
        _start_row_gather(lambda r: sa[r], ys_hbm, ybuf.at[s, 0], sems.at[s], tm)
        _start_row_gather(lambda r: sb[r], ys_hbm, ybuf.at[s, 1], sems.at[s], tm)

    @pl.when(i == 0)
    def _():
        start(sa_cur, sb_cur, 0)

    @pl.when(i + 1 < pl.num_programs(0))
    def _():
        start(sa_nxt, sb_nxt, 1 - slot)

    _wait_rows(ybuf.at[slot], sems.at[slot])
    r = r_ref[...]
    y = (r[:, 2:3] * _load_row_tiles(ybuf.at[slot, 0], tm, F32)
         + r[:, 3:4] * _load_row_tiles(ybuf.at[slot, 1], tm, F32))
    xo_ref[...] = _layer_norm(alpha * x_ref[...] + g1p_ref[0] * y, lng_ref[...], lnb_ref[...])


def _combine_ln(ys_tiles, slot_a, slot_b, route, x, g1p, lng, lnb, alpha, tm):
    b, s, d = x.shape
    n = b * s
    sub = d // LANES
    n_tiles = n // tm
    per_seq = s // tm
    cur = pl.BlockSpec((tm,), lambda i: (i,), memory_space=pltpu.SMEM)
    nxt = pl.BlockSpec((tm,), lambda i: (jnp.minimum(i + 1, n_tiles - 1),), memory_space=pltpu.SMEM)
    tile = pl.BlockSpec((tm, d), lambda i: (i, 0))
    vec = pl.BlockSpec((1, d), lambda i: (0, 0))
    out = pl.pallas_call(
        functools.partial(_combine_ln_kernel, alpha=alpha, tm=tm),
        grid=(n_tiles,),
        in_specs=[cur, cur, nxt, nxt, pl.BlockSpec(memory_space=pl.ANY),
                  pl.BlockSpec((tm, LANES), lambda i: (i, 0)), tile,
                  pl.BlockSpec((1, 1, d), lambda i: (i // per_seq, 0, 0)), vec, vec],
        out_specs=tile,
        out_shape=jax.ShapeDtypeStruct((n, d), F32),
        scratch_shapes=[pltpu.VMEM((2, 2, tm * sub, LANES), F32), pltpu.SemaphoreType.DMA((2,))],
        compiler_params=_params("arbitrary"),
        name="combine_ln",
    )(slot_a, slot_b, slot_a, slot_b, ys_tiles, route.reshape(n, LANES), x.reshape(n, d), g1p, lng, lnb)
    return out.reshape(b, s, d)


def _moe_plan(route, tm):
    n = route.shape[0]
    e = N_EXPERTS
    n_tiles = (TOP_K * n + e * (tm - 1)) // tm
    flat_e = route[:, :TOP_K].astype(jnp.int32).reshape(-1)
    onehot = (flat_e[:, None] == jnp.arange(e, dtype=jnp.int32)[None, :]).astype(jnp.int32)
    incl = jnp.cumsum(onehot, axis=0)
    counts = incl[-1]
    padded = (counts + tm - 1) // tm * tm
    ends = jnp.cumsum(padded)
    offs = ends - padded
    slot = jnp.sum(onehot * (offs[None, :] + incl - 1), axis=1)
    n_active = ends[-1] // tm
    tile_id = jnp.minimum(jnp.arange(n_tiles + MOE_AHEAD, dtype=jnp.int32), n_active - 1)
    tile_expert = jnp.minimum(jnp.sum((tile_id[:, None] * tm >= ends[None, :]).astype(jnp.int32), axis=1), e - 1)
    by_expert = jnp.argsort(flat_e, stable=True).astype(jnp.int32) // TOP_K
    by_expert = jnp.pad(by_expert, (0, n_tiles * tm - TOP_K * n))
    row_expert = jnp.repeat(tile_expert[:n_tiles], tm)
    shift = offs - (jnp.cumsum(counts) - counts)
    src = jnp.zeros((n_tiles * tm,), jnp.int32)
    for k in range(e):
        src = jnp.where(row_expert == k, jnp.roll(by_expert, shift[k]), src)
    return ((slot * ROW_TILE).reshape(n, TOP_K), src * ROW_TILE, tile_expert.astype(jnp.int32),
            n_active.reshape(1).astype(jnp.int32))


def _moe_swiglu_ln(u_tiles, route, wg, wu, wd, x, g1p, lng, lnb, alpha, tiles):
    assert TOP_K == 2
    b, s, d = x.shape
    n = b * s
    nf = tiles.moe_slabs
    tf = wg.shape[2] // nf
    assert tiles.moe_rows % nf == 0
    slot, src, tile_expert, n_active = _moe_plan(route.reshape(n, LANES), tiles.moe_rows)
    ys = _moe_ffn(u_tiles.reshape(n * (d // LANES), LANES), src, wg, wu, wd, tile_expert, n_active,
                  tiles.moe_rows, tf)
    return _combine_ln(ys, slot[:, 0], slot[:, 1], route, x, g1p, lng, lnb, alpha, tiles.token_rows)


class _Tiles(NamedTuple):
    token_rows: int
    attn_heads: int
    attn_chunk: int
    moe_rows: int
    moe_slabs: int


def _tiles(s, d, d_ff_expert):
    return _Tiles(token_rows=1024,
                  attn_heads=math.gcd(d // HEAD_DIM, 8),
                  attn_chunk=math.gcd(s // MOBA_BLOCK, 4),
                  moe_rows=896,
                  moe_slabs=2 if d_ff_expert % (2 * MXU_N) == 0 else 1)


def kernel(x, c, positions, ada_w, ada_b, ln_g, ln_b, attn_w_in, attn_w_out, rec_w_in, rec_lb_logits,
           rec_norm_w, rec_w_out, ffn_w_gate, ffn_w_up, ffn_w_down, router_w, moe_w_gate, moe_w_up,
           moe_w_down):
    b, s, d = x.shape
    depth = ada_w.shape[0]
    alpha = (2.0 * depth) ** 0.25
    tiles = _tiles(s, d, moe_w_gate.shape[-1])
    ts = tiles.token_rows

    mods = _ada_mods(c, ada_w, ada_b).reshape(depth, b, 6, 1, d)
    one_plus = lambda t: 1.0 + t
    pos = positions.reshape(b, s, 1)

    for i in range(depth):
        shift_m, scale_m, gate_m, shift_f, scale_f, gate_f = (mods[i, :, r] for r in range(6))
        j = i // 2
        lng = ln_g[i].reshape(2, 1, d)
        lnb = ln_b[i].reshape(2, 1, d)
        if i % 2 == 0:
            qkv = _qkv_proj(x, one_plus(scale_m), shift_m, pos, attn_w_in[j].astype(BF16), ts)
            o = _moba_attention(qkv, d, n_grp=tiles.attn_heads, ch=tiles.attn_chunk)
            x = _proj_ffn_dense(o, attn_w_out[j].astype(BF16), x, one_plus(gate_m), lng, lnb,
                                one_plus(scale_f), shift_f, ffn_w_gate[j].astype(BF16),
                                ffn_w_up[j].astype(BF16), ffn_w_down[j].astype(BF16), one_plus(gate_f),
                                alpha, ts)
        else:
            qf, key, v, gs, lf = _rec_in(x, one_plus(scale_m), shift_m, rec_w_in[j].astype(BF16),
                                         rec_lb_logits, i, ts)
            o = _hgrn_recurrence(qf, key, v, lf, gs, rec_norm_w[j].reshape(1, d), ts)
            w_hi = router_w[j].astype(BF16)
            w_lo = (router_w[j] - w_hi.astype(F32)).astype(BF16)
            wr = (jnp.zeros((d, 2 * LANES), BF16).at[:, :N_EXPERTS].set(w_hi)
                  .at[:, LANES:LANES + N_EXPERTS].set(w_lo))
            x, u, route = _proj_ln_route(o, rec_w_out[j].astype(BF16), x, one_plus(gate_m), lng[0], lnb[0],
                                         one_plus(scale_f), shift_f, wr, alpha, ts)
            x = _moe_swiglu_ln(u, route, moe_w_gate[j].astype(BF16), moe_w_up[j].astype(BF16),
                               moe_w_down[j].astype(BF16), x, one_plus(gate_f), lng[1], lnb[1], alpha, tiles)
    return x
```

```python
import functools
import math
from typing import NamedTuple

import jax
import jax.numpy as jnp
from jax import lax
from jax.experimental import pallas as pl
from jax.experimental.pallas import tpu as pltpu

HEAD_DIM = 128
ROPE_DIM = HEAD_DIM // 4
ROPE_THETA = 500000.0
MOBA_BLOCK = 256
MOBA_TOPK = 3
HGRN_EXPAND = 128
HGRN_CHUNK = 64
N_EXPERTS = 8
TOP_K = 2
LN_EPS = 1e-5
RMS_EPS = 1e-6

LANES = 128
MXU_N = 256
VMEM_LIMIT = 56 * 1024 * 1024
NEG_BIG = -1e30

F32 = jnp.float32
BF16 = jnp.bfloat16
HIGHEST = lax.Precision.HIGHEST


def _params(*sem):
    return pltpu.CompilerParams(dimension_semantics=sem, vmem_limit_bytes=VMEM_LIMIT)


def _dot(a, b):
    return jnp.dot(a, b, preferred_element_type=F32)


def _dot_nt(a, b, precision=None):
    return lax.dot_general(a, b, (((1,), (1,)), ((), ())), precision=precision,
                           preferred_element_type=F32)


def _silu(x):
    return x * jax.nn.sigmoid(x)


def _layer_norm(z, g, b):
    mu = jnp.mean(z, axis=-1, keepdims=True)
    d = z - mu
    var = jnp.mean(d * d, axis=-1, keepdims=True)
    return d * lax.rsqrt(var + LN_EPS) * g + b


def _ada_kernel(c_ref, w_ref, b_ref, o_ref):
    a = _silu(c_ref[...])
    o_ref[0] = jnp.dot(a, w_ref[0], precision=HIGHEST, preferred_element_type=F32) + b_ref[0]


def _ada_mods(c, ada_w, ada_b):
    depth, d, m = ada_w.shape
    b = c.shape[0]
    tn = m // 4
    return pl.pallas_call(
        _ada_kernel,
        grid=(depth, m // tn),
        in_specs=[pl.BlockSpec((b, d), lambda l, j: (0, 0)),
                  pl.BlockSpec((1, d, tn), lambda l, j: (l, 0, j)),
                  pl.BlockSpec((1, 1, tn), lambda l, j: (l, 0, j))],
        out_specs=pl.BlockSpec((1, b, tn), lambda l, j: (l, 0, j)),
        out_shape=jax.ShapeDtypeStruct((depth, b, m), F32),
        compiler_params=_params("arbitrary", "arbitrary"),
        name="ada_mods",
    )(c, ada_w, ada_b.reshape(depth, 1, m))


def _qkv_kernel(x_ref, sc_ref, sh_ref, pos_ref, w_ref, o_ref, trig_scr, *, n_rot):
    u = (x_ref[0] * sc_ref[0] + sh_ref[0]).astype(BF16)
    half = ROPE_DIM // 2
    pack = LANES // ROPE_DIM
    rows = x_ref.shape[1] // pack
    lane = lax.broadcasted_iota(jnp.int32, (1, LANES), 1)
    inv = jnp.exp(-math.log(ROPE_THETA) * (lane % half).astype(F32) * (2.0 / ROPE_DIM))
    pos = jnp.zeros((rows, LANES), F32)
    for m in range(pack):
        pos = jnp.where(lane // ROPE_DIM == m, pos_ref[0, pl.ds(m, rows, stride=pack), :].astype(F32), pos)
    ang = pos * inv
    for t, trig in enumerate((jnp.cos(ang), jnp.sin(ang))):
        for m in range(pack):
            moved = trig if m == 0 else pltpu.roll(trig, LANES - ROPE_DIM * m, 1)
            trig_scr[t, pl.ds(m, rows, stride=pack), :] = moved
    cos_t = jnp.where(lane < ROPE_DIM, trig_scr[0], 1.0)
    sin_t = jnp.where(lane < ROPE_DIM, trig_scr[1], 0.0)
    sin_lo = jnp.where(lane < half, -sin_t, 0.0)
    sin_hi = jnp.where((lane >= half) & (lane < ROPE_DIM), sin_t, 0.0)
    n_cols = w_ref.shape[1]
    for c in range(n_cols // MXU_N):
        y = _dot(u, w_ref[:, c * MXU_N:(c + 1) * MXU_N])
        for s in range(MXU_N // HEAD_DIM):
            col = c * MXU_N + s * HEAD_DIM
            t = y[:, s * HEAD_DIM:(s + 1) * HEAD_DIM]
            if col < n_rot:
                t = (t * cos_t + pltpu.roll(t, HEAD_DIM - half, 1) * sin_lo
                     + pltpu.roll(t, half, 1) * sin_hi)
            if col < n_rot // 2:
                t = t * (HEAD_DIM ** -0.5 * math.log2(math.e))
            o_ref[0, :, col:col + HEAD_DIM] = t.astype(BF16)


def _qkv_proj(x, sc, sh, pos, w, ts):
    b, s, d = x.shape
    n = w.shape[1]
    return pl.pallas_call(
        functools.partial(_qkv_kernel, n_rot=2 * d),
        grid=(b, s // ts),
        in_specs=[pl.BlockSpec((1, ts, d), lambda i, j: (i, j, 0)),
                  pl.BlockSpec((1, 1, d), lambda i, j: (i, 0, 0)),
                  pl.BlockSpec((1, 1, d), lambda i, j: (i, 0, 0)),
                  pl.BlockSpec((1, ts, 1), lambda i, j: (i, j, 0)),
                  pl.BlockSpec((d, n), lambda i, j: (0, 0))],
        out_specs=pl.BlockSpec((1, ts, n), lambda i, j: (i, j, 0)),
        out_shape=jax.ShapeDtypeStruct((b, s, n), BF16),
        scratch_shapes=[pltpu.VMEM((2, ts, LANES), F32)],
        compiler_params=_params("arbitrary", "arbitrary"),
        name="qkv_proj",
    )(x, sc, sh, pos, w)


def _moba_kernel(q_ref, k_ref, v_ref, o_ref, kmean_scr, vt_scr, bias_scr, m_scr, l_scr, acc_scr,
                 *, nb, n_grp, ch):
    blk = MOBA_BLOCK
    hd = HEAD_DIM
    cw = ch * blk
    i = pl.program_id(2)

    @pl.when(i == 0)
    def _():
        for hh in range(n_grp):
            means = []
            for n in range(nb):
                kb = k_ref[0, n * blk:(n + 1) * blk, hh * hd:(hh + 1) * hd].astype(F32)
                means.append(jnp.mean(kb, axis=0, keepdims=True))
                vb = v_ref[0, n * blk:(n + 1) * blk, hh * hd:(hh + 1) * hd]
                vt_scr[hh, n // ch, :, (n % ch) * blk:(n % ch + 1) * blk] = vb.astype(F32).T.astype(BF16)
            km = jnp.concatenate(means, axis=0)
            km_hi = km.astype(BF16)
            km_lo = (km - km_hi.astype(F32)).astype(BF16)
            kmean_scr[hh] = jnp.concatenate([km_hi, km_lo], axis=0)

    def sweep(c, n_own):
        own = n_own > 0
        n_blk = n_own if own else ch
        k0 = pl.multiple_of(c * cw, cw)
        heads = [slice(hh * hd, (hh + 1) * hd) for hh in range(n_grp)]
        scores = [_dot_nt(k_ref[0, pl.ds(k0, n_blk * blk), cols], q_ref[0, :, cols])
                  for cols in heads]
        m_news, l_news, probs = [], [], []
        for hh, s in enumerate(scores):
            parts = [s[t * blk:(t + 1) * blk] + bias_scr[hh, pl.ds(c * ch + t, 1), :]
                     for t in range(n_blk - 1 if own else n_blk)]
            if own:
                s_own = s[(n_blk - 1) * blk:]
                kpos = lax.broadcasted_iota(jnp.int32, s_own.shape, 0)
                qpos = lax.broadcasted_iota(jnp.int32, s_own.shape, 1)
                parts.append(jnp.where(kpos <= qpos, s_own, NEG_BIG))
            m_blk = parts[0]
            for t in range(1, n_blk):
                m_blk = jnp.maximum(m_blk, parts[t])
            m_new = jnp.max(m_blk, axis=0, keepdims=True)
            if not own:
                m_new = jnp.maximum(m_new, m_scr[hh])
            ps = [jnp.exp2(part - m_new) for part in parts]
            l_new = ps[0]
            for t in range(1, n_blk):
                l_new = l_new + ps[t]
            m_news.append(m_new)
            l_news.append(jnp.sum(l_new, axis=0, keepdims=True))
            probs.append(jnp.concatenate([p.astype(BF16) for p in ps], axis=0) if n_blk > 1
                         else ps[0].astype(BF16))
        pvs = [_dot(vt_scr[hh, c, :, 0:n_blk * blk], probs[hh]) for hh in range(n_grp)]
        for hh in range(n_grp):
            if own:
                l_scr[hh] = l_news[hh]
                acc_scr[hh] = pvs[hh]
            else:
                alpha = jnp.exp2(m_scr[hh] - m_news[hh])
                l_scr[hh] = alpha * l_scr[hh] + l_news[hh]
                acc_scr[hh] = alpha * acc_scr[hh] + pvs[hh]
            m_scr[hh] = m_news[hh]

    c_own = i // ch
    for hh in range(n_grp):
        q = q_ref[0, :, hh * hd:(hh + 1) * hd]
        gate2 = _dot_nt(kmean_scr[hh], q)
        gate = gate2[:nb] + gate2[nb:]
        blk_id = lax.broadcasted_iota(jnp.int32, gate.shape, 0)
        past = blk_id < i
        g = jnp.where(past, gate, -jnp.inf)
        rank = jnp.zeros(gate.shape, jnp.int32)
        for m in range(nb):
            gm = g[m:m + 1, :]
            beats = jnp.where(gm > g, 1, jnp.where(gm == g, jnp.where(blk_id > m, 1, 0), 0))
            rank = rank + beats
        bias_scr[hh] = jnp.where(past, jnp.where(rank < MOBA_TOPK, 0.0, NEG_BIG), NEG_BIG)

    for r in range(ch):
        @pl.when(i % ch == r)
        def _():
            sweep(c_own, r + 1)

    def body(c, carry):
        sweep(c, 0)
        return carry

    lax.fori_loop(0, c_own, body, 0)
    for hh in range(n_grp):
        o_ref[0, :, hh * hd:(hh + 1) * hd] = (acc_scr[hh] / l_scr[hh]).T.astype(BF16)


def _moba_attention(qkv, d, n_grp, ch):
    b, s, _ = qkv.shape
    h = d // HEAD_DIM
    blk = MOBA_BLOCK
    nb = s // blk
    gw = n_grp * HEAD_DIM
    ng = h // n_grp
    return pl.pallas_call(
        functools.partial(_moba_kernel, nb=nb, n_grp=n_grp, ch=ch),
        grid=(b, ng, nb),
        in_specs=[pl.BlockSpec((1, blk, gw), lambda bi, hi, i: (bi, i, hi)),
                  pl.BlockSpec((1, s, gw), lambda bi, hi, i: (bi, 0, ng + hi)),
                  pl.BlockSpec((1, s, gw), lambda bi, hi, i: (bi, 0, 2 * ng + hi))],
        out_specs=pl.BlockSpec((1, blk, gw), lambda bi, hi, i: (bi, i, hi)),
        out_shape=jax.ShapeDtypeStruct((b, s, d), BF16),
        scratch_shapes=[pltpu.VMEM((n_grp, 2 * nb, HEAD_DIM), BF16),
                        pltpu.VMEM((n_grp, nb // ch, HEAD_DIM, ch * blk), BF16),
                        pltpu.VMEM((n_grp, nb, blk), F32),
                        pltpu.VMEM((n_grp, 1, blk), F32),
                        pltpu.VMEM((n_grp, 1, blk), F32),
                        pltpu.VMEM((n_grp, HEAD_DIM, blk), F32)],
        compiler_params=_params("arbitrary", "arbitrary", "arbitrary"),
        name="moba_attention",
    )(qkv, qkv, qkv)


def _store_row_tiles(ref, val):
    rows, d = val.shape
    sub = d // LANES
    for c in range(sub):
        ref[pl.ds(c, rows, stride=sub), :] = val[:, c * LANES:(c + 1) * LANES]


def _load_row_tiles(ref, rows, dtype):
    sub = ref.shape[0] // rows
    return jnp.concatenate([ref[pl.ds(c, rows, stride=sub), :].astype(dtype) for c in range(sub)], axis=1)


def _route_top2(u, wr_ref):
    u_hi = u.astype(BF16)
    u_lo = (u - u_hi.astype(F32)).astype(BF16)
    both = _dot(u_hi, wr_ref[...])
    logits = both[:, :LANES] + both[:, LANES:] + _dot(u_lo, wr_ref[:, :LANES])
    lane = lax.broadcasted_iota(jnp.int32, logits.shape, 1)
    logits = jnp.where(lane < N_EXPERTS, logits, -jnp.inf)
    m1 = jnp.max(logits, axis=-1, keepdims=True)
    i1 = jnp.min(jnp.where(logits == m1, lane, LANES), axis=-1, keepdims=True)
    rest = jnp.where(lane == i1, -jnp.inf, logits)
    m2 = jnp.max(rest, axis=-1, keepdims=True)
    i2 = jnp.min(jnp.where(rest == m2, lane, LANES), axis=-1, keepdims=True)
    e2 = jnp.exp(m2 - m1)
    w1 = 1.0 / (1.0 + e2)
    w2 = e2 / (1.0 + e2)
    return jnp.where(lane == 0, i1.astype(F32),
                     jnp.where(lane == 1, i2.astype(F32),
                               jnp.where(lane == 2, w1, jnp.where(lane == 3, w2, 0.0))))


def _proj_ln_route_kernel(a_ref, w_ref, x_ref, g1p_ref, lng_ref, lnb_ref, sc_ref, sh_ref, wr_ref,
                          xo_ref, uo_ref, ro_ref, *, alpha):
    y = _dot(a_ref[0], w_ref[...])
    xn = _layer_norm(alpha * x_ref[0] + g1p_ref[0] * y, lng_ref[...], lnb_ref[...])
    xo_ref[0] = xn
    u = xn * sc_ref[0] + sh_ref[0]
    _store_row_tiles(uo_ref.at[0], u)
    ro_ref[0] = _route_top2(u, wr_ref)


def _proj_ln_route(a, w, x, g1p, lng, lnb, sc, sh, wr, alpha, tm):
    b, s, d = x.shape
    sub = d // LANES
    tile = pl.BlockSpec((1, tm, d), lambda i, j: (i, j, 0))
    per_b = pl.BlockSpec((1, 1, d), lambda i, j: (i, 0, 0))
    vec = pl.BlockSpec((1, d), lambda i, j: (0, 0))
    return pl.pallas_call(
        functools.partial(_proj_ln_route_kernel, alpha=alpha),
        grid=(b, s // tm),
        in_specs=[tile, pl.BlockSpec((d, d), lambda i, j: (0, 0)), tile, per_b, vec, vec, per_b, per_b,
                  pl.BlockSpec((d, 2 * LANES), lambda i, j: (0, 0))],
        out_specs=[tile, pl.BlockSpec((1, tm * sub, LANES), lambda i, j: (i, j, 0)),
                   pl.BlockSpec((1, tm, LANES), lambda i, j: (i, j, 0))],
        out_shape=[jax.ShapeDtypeStruct((b, s, d), F32), jax.ShapeDtypeStruct((b, s * sub, LANES), F32),
                   jax.ShapeDtypeStruct((b, s, LANES), F32)],
        compiler_params=_params("arbitrary", "arbitrary"),
        name="proj_ln_route",
    )(a, w, x, g1p, lng, lnb, sc, sh, wr)


def _proj_ffn_kernel(a_ref, wo_ref, x_ref, g1m_ref, lng_ref, lnb_ref, sc_ref, sh_ref, wg_ref, wu_ref, wd_ref,
                     g1f_ref, xo_ref, h_scr, *, alpha, tf):
    x1 = _layer_norm(alpha * x_ref[0] + g1m_ref[0] * _dot(a_ref[0], wo_ref[...]), lng_ref[0], lnb_ref[0])
    u = (x1 * sc_ref[0] + sh_ref[0]).astype(BF16)
    f = wg_ref.shape[1]
    for j in range(f // tf):
        cols = slice(j * tf, (j + 1) * tf)
        hj = _silu(_dot(u, wg_ref[:, cols])) * _dot(u, wu_ref[:, cols])
        h_scr[:, cols] = hj.astype(BF16)
    y = _dot(h_scr[...], wd_ref[...])
    xo_ref[0] = _layer_norm(alpha * x1 + g1f_ref[0] * y, lng_ref[1], lnb_ref[1])


def _proj_ffn_dense(a, wo, x, g1m, lng, lnb, sc, sh, wg, wu, wd, g1f, alpha, tm):
    b, s, d = x.shape
    f = wg.shape[1]
    tile = pl.BlockSpec((1, tm, d), lambda i, j: (i, j, 0))
    per_b = pl.BlockSpec((1, 1, d), lambda i, j: (i, 0, 0))
    ln_pair = pl.BlockSpec((2, 1, d), lambda i, j: (0, 0, 0))
    resident = dict(pipeline_mode=pl.Buffered(1))
    return pl.pallas_call(
        functools.partial(_proj_ffn_kernel, alpha=alpha, tf=MXU_N),
        grid=(b, s // tm),
        in_specs=[tile, pl.BlockSpec((d, d), lambda i, j: (0, 0), **resident), tile, per_b, ln_pair, ln_pair,
                  per_b, per_b,
                  pl.BlockSpec((d, f), lambda i, j: (0, 0), **resident),
                  pl.BlockSpec((d, f), lambda i, j: (0, 0), **resident),
                  pl.BlockSpec((f, d), lambda i, j: (0, 0), **resident),
                  per_b],
        out_specs=tile,
        out_shape=jax.ShapeDtypeStruct((b, s, d), F32),
        scratch_shapes=[pltpu.VMEM((tm, f), BF16)],
        compiler_params=_params("arbitrary", "arbitrary"),
        name="proj_ffn_dense",
    )(a, wo, x, g1m, lng, lnb, sc, sh, wg, wu, wd, g1f)


def _rec_in_kernel(x_ref, sc_ref, sh_ref, w_ref, lbl_ref, qf_ref, key_ref, v_ref, gs_ref, lf_ref,
                   *, layer_idx):
    u = (x_ref[0] * sc_ref[0] + sh_ref[0]).astype(BF16)
    d = x_ref.shape[2]
    lbl = lbl_ref[...]
    e = jnp.exp(lbl - jnp.max(lbl, axis=0, keepdims=True))
    sm = e / jnp.sum(e, axis=0, keepdims=True)
    lb = jnp.zeros((1, d), F32)
    for r in range(1, layer_idx + 1):
        lb = lb + sm[r:r + 1, :]
    for c in range(d // MXU_N):
        cols = slice(c * MXU_N, (c + 1) * MXU_N)
        q = _dot(u, w_ref[:, c * MXU_N:(c + 1) * MXU_N])
        qf_ref[0, :, cols] = _silu(q).astype(BF16)
        f = _dot(u, w_ref[:, d + c * MXU_N:d + (c + 1) * MXU_N])
        lbc = lb[:, cols]
        f_gate = lbc + (1.0 - lbc) * jax.nn.sigmoid(f)
        lf_ref[0, :, cols] = jnp.log(f_gate)
        key_ref[0, :, cols] = (1.0 - f_gate).astype(BF16)
        v = _dot(u, w_ref[:, 2 * d + c * MXU_N:2 * d + (c + 1) * MXU_N])
        v_ref[0, :, cols] = v.astype(BF16)
        g = _dot(u, w_ref[:, 3 * d + c * MXU_N:3 * d + (c + 1) * MXU_N])
        gs_ref[0, :, cols] = _silu(g).astype(BF16)


def _rec_in(x, sc, sh, w, lb_logits, layer_idx, ts):
    b, s, d = x.shape
    depth = lb_logits.shape[0]
    tile = pl.BlockSpec((1, ts, d), lambda i, j: (i, j, 0))
    per_b = pl.BlockSpec((1, 1, d), lambda i, j: (i, 0, 0))
    bf = jax.ShapeDtypeStruct((b, s, d), BF16)
    return pl.pallas_call(
        functools.partial(_rec_in_kernel, layer_idx=layer_idx),
        grid=(b, s // ts),
        in_specs=[tile, per_b, per_b,
                  pl.BlockSpec((d, 4 * d), lambda i, j: (0, 0)),
                  pl.BlockSpec((depth, d), lambda i, j: (0, 0))],
        out_specs=[tile] * 5,
        out_shape=[bf, bf, bf, bf, jax.ShapeDtypeStruct((b, s, d), F32)],
        compiler_params=_params("arbitrary", "arbitrary"),
        name="rec_in",
    )(x, sc, sh, w, lb_logits)


def _hgrn_kernel(qf_ref, key_ref, v_ref, lf_ref, gs_ref, nw_ref, o_ref, st_scr, *, n_heads):
    cs = HGRN_CHUNK
    dk = HGRN_EXPAND

    @pl.when(pl.program_id(1) == 0)
    def _():
        st_scr[...] = jnp.zeros(st_scr.shape, F32)

    row = lax.broadcasted_iota(jnp.int32, (cs, cs), 0)
    col = lax.broadcasted_iota(jnp.int32, (cs, cs), 1)
    causal = row >= col
    tri = jnp.where(causal, 1.0, 0.0).astype(BF16)
    n_chunks = qf_ref.shape[1] // cs

    def chunk(c, carry):
        rows = pl.ds(pl.multiple_of(c * cs, cs), cs)
        lf = lf_ref[0, rows, :]
        hi = lf.astype(BF16)
        r1 = lf - hi.astype(F32)
        mid = r1.astype(BF16)
        lo = (r1 - mid.astype(F32)).astype(BF16)
        g = _dot(tri, hi) + _dot(tri, mid) + _dot(tri, lo)
        g_last = g[cs - 1:cs, :]
        qf = qf_ref[0, rows, :].astype(F32)
        key = key_ref[0, rows, :].astype(F32)
        v = v_ref[0, rows, :]
        q_dec = (qf * jnp.exp(g)).astype(BF16)
        k_dec = (key * jnp.exp(-g)).astype(BF16)
        k_state = (key * jnp.exp(g_last - g)).astype(BF16)
        decay = jnp.exp(g_last)
        heads = [slice(h * dk, (h + 1) * dk) for h in range(n_heads)]
        a = [jnp.where(causal, _dot_nt(q_dec[:, c], k_dec[:, c]), 0.0).astype(BF16) for c in heads]
        st = [st_scr[h] for h in range(n_heads)]
        o = [_dot(a[h], v[:, c]) + _dot_nt(q_dec[:, c], st[h].astype(BF16))
             for h, c in enumerate(heads)]
        for h, c in enumerate(heads):
            v_t = v[:, c].astype(F32).T.astype(BF16)
            st_scr[h] = st[h] * decay[:, c] + _dot(v_t, k_state[:, c])
        on = jnp.concatenate(
            [oh * lax.rsqrt(jnp.mean(oh * oh, axis=-1, keepdims=True) + RMS_EPS) for oh in o], axis=1)
        o_ref[0, rows, :] = (on * nw_ref[...] * gs_ref[0, rows, :].astype(F32)).astype(BF16)
        return carry

    lax.fori_loop(0, n_chunks, chunk, 0, unroll=4)


def _hgrn_recurrence(qf, key, v, lf, gs, norm_w, ts):
    b, s, d = qf.shape
    h = d // HGRN_EXPAND
    tile = pl.BlockSpec((1, ts, d), lambda i, j: (i, j, 0))
    return pl.pallas_call(
        functools.partial(_hgrn_kernel, n_heads=h),
        grid=(b, s // ts),
        in_specs=[tile, tile, tile, tile, tile, pl.BlockSpec((1, d), lambda i, j: (0, 0))],
        out_specs=tile,
        out_shape=jax.ShapeDtypeStruct((b, s, d), BF16),
        scratch_shapes=[pltpu.VMEM((h, HGRN_EXPAND, HGRN_EXPAND), F32)],
        compiler_params=_params("arbitrary", "arbitrary"),
        name="hgrn_recurrence",
    )(qf, key, v, lf, gs, norm_w)


GATHER_UNROLL = 8


ROW_TILE = 8


def _start_row(idx_at, src_hbm, dst, sem, r):
    first = pl.multiple_of(idx_at(r), ROW_TILE)
    pltpu.make_async_copy(src_hbm.at[pl.ds(first, ROW_TILE), :],
                          dst.at[pl.ds(r * ROW_TILE, ROW_TILE), :], sem).start()


def _start_row_gather(idx_at, src_hbm, dst, sem, n_rows):
    def one(r, carry):
        _start_row(idx_at, src_hbm, dst, sem, r)
        return carry

    lax.fori_loop(0, n_rows, one, 0, unroll=GATHER_UNROLL)


def _wait_rows(buf, sem):
    pltpu.make_async_copy(buf, buf, sem).wait()


MOE_AHEAD = 2
MOE_XBUFS = MOE_AHEAD + 1


def _moe_ffn_kernel(te_ref, na_ref, src_t0_ref, src_t1_ref, src_ahead_ref, u_hbm, wg_ref, wu_ref, wd_ref,
                    o_ref, xbuf, xb_scr, h_scr, acc_scr, sems, *, tm, nf, n_tiles):
    i = pl.program_id(0)
    j = pl.program_id(1)
    rows_per_step = tm // nf
    n_active = na_ref[0]
    active = i < n_active
    slot = i % MOE_XBUFS
    ahead = (i + MOE_AHEAD) % MOE_XBUFS

    assert MOE_AHEAD == 2

    @pl.when((j == 0) & (i == 0))
    def _():
        _start_row_gather(lambda r: src_t0_ref[0, 0, r], u_hbm, xbuf.at[0], sems.at[0], tm)
        _start_row_gather(lambda r: src_t1_ref[0, 0, r], u_hbm, xbuf.at[1], sems.at[1], tm)

    @pl.when((j == 0) & (i <= n_active + 1))
    def _():
        _wait_rows(xbuf.at[slot], sems.at[slot])

    @pl.when(active & (j == 0))
    def _():
        xb_scr[...] = _load_row_tiles(xbuf.at[slot], tm, BF16)

    @pl.when(active)
    def _():
        for k in range(rows_per_step):
            _start_row(lambda r: src_ahead_ref[0, 0, r], u_hbm, xbuf.at[ahead], sems.at[ahead],
                       j * rows_per_step + k)
        x = xb_scr[...]
        for c in range(wg_ref.shape[2] // MXU_N):
            cols = slice(c * MXU_N, (c + 1) * MXU_N)
            hc = _silu(_dot(x, wg_ref[0, :, cols])) * _dot(x, wu_ref[0, :, cols])
            h_scr[:, cols] = hc.astype(BF16)
        y = _dot(h_scr[...], wd_ref[0])
        if nf == 1:
            _store_row_tiles(o_ref, y)
        else:
            @pl.when(j == 0)
            def _():
                acc_scr[...] = y

            if nf > 2:
                @pl.when((j > 0) & (j < nf - 1))
                def _():
                    acc_scr[...] += y

            @pl.when(j == nf - 1)
            def _():
                _store_row_tiles(o_ref, acc_scr[...] + y)

    @pl.when(jnp.logical_not(active) & (j == nf - 1) & (i < n_tiles))
    def _():
        o_ref[...] = jnp.zeros(o_ref.shape, F32)


def _moe_ffn(u_tiles, src, wg, wu, wd, tile_expert, n_active, tm, tf):
    d = wd.shape[2]
    sub = d // LANES
    assert sub == ROW_TILE
    n_tiles = src.shape[0] // tm
    assert tile_expert.shape[0] == n_tiles + MOE_AHEAD
    nf = wg.shape[2] // tf

    def frozen(j, i, na):
        return jnp.where(i < na[0], j, nf - 1)

    def idx_block(tile_of):
        return pl.BlockSpec((1, 1, tm), lambda i, j, te, na: (jnp.minimum(tile_of(i), n_tiles - 1), 0, 0),
                            memory_space=pltpu.SMEM)

    grid_spec = pltpu.PrefetchScalarGridSpec(
        num_scalar_prefetch=2,
        grid=(n_tiles + MOE_AHEAD, nf),
        in_specs=[idx_block(lambda i: 0), idx_block(lambda i: 1), idx_block(lambda i: i + MOE_AHEAD),
                  pl.BlockSpec(memory_space=pl.ANY),
                  pl.BlockSpec((1, d, tf), lambda i, j, te, na: (te[i], 0, frozen(j, i, na))),
                  pl.BlockSpec((1, d, tf), lambda i, j, te, na: (te[i], 0, frozen(j, i, na))),
                  pl.BlockSpec((1, tf, d), lambda i, j, te, na: (te[i], frozen(j, i, na), 0))],
        out_specs=pl.BlockSpec((tm * sub, LANES), lambda i, j, te, na: (jnp.minimum(i, n_tiles - 1), 0)),
        scratch_shapes=[pltpu.VMEM((MOE_XBUFS, tm * sub, LANES), F32),
                        pltpu.VMEM((tm, d), BF16),
                        pltpu.VMEM((tm, tf), BF16),
                        pltpu.VMEM((tm, d), F32),
                        pltpu.SemaphoreType.DMA((MOE_XBUFS,))],
    )
    src3 = src.reshape(n_tiles, 1, tm)
    return pl.pallas_call(
        functools.partial(_moe_ffn_kernel, tm=tm, nf=nf, n_tiles=n_tiles),
        grid_spec=grid_spec,
        out_shape=jax.ShapeDtypeStruct((n_tiles * tm * sub, LANES), F32),
        compiler_params=_params("arbitrary", "arbitrary"),
        name="moe_ffn",
    )(tile_expert, n_active, src3, src3, src3, u_tiles, wg, wu, wd)


def _combine_ln_kernel(sa_cur, sb_cur, sa_nxt, sb_nxt, ys_hbm, r_ref, x_ref, g1p_ref, lng_ref, lnb_ref,
                       xo_ref, ybuf, sems, *, alpha, tm):
    i = pl.program_id(0)
    slot = i % 2

    def start(sa, sb, s):
        _start_row_gather(lambda r: sa[r], ys_hbm, ybuf.at[s, 0], sems.at[s], tm)
        _start_row_gather(lambda r: sb[r], ys_hbm, ybuf.at[s, 1], sems.at[s], tm)

    @pl.when(i == 0)
    def _():
        start(sa_cur, sb_cur, 0)

    @pl.when(i + 1 < pl.num_programs(0))
    def _():
        start(sa_nxt, sb_nxt, 1 - slot)

    _wait_rows(ybuf.at[slot], sems.at[slot])
    r = r_ref[...]
    y = (r[:, 2:3] * _load_row_tiles(ybuf.at[slot, 0], tm, F32)
         + r[:, 3:4] * _load_row_tiles(ybuf.at[slot, 1], tm, F32))
    xo_ref[...] = _layer_norm(alpha * x_ref[...] + g1p_ref[0] * y, lng_ref[...], lnb_ref[...])


def _combine_ln(ys_tiles, slot_a, slot_b, route, x, g1p, lng, lnb, alpha, tm):
    b, s, d = x.shape
    n = b * s
    sub = d // LANES
    n_tiles = n // tm
    per_seq = s // tm
    cur = pl.BlockSpec((tm,), lambda i: (i,), memory_space=pltpu.SMEM)
    nxt = pl.BlockSpec((tm,), lambda i: (jnp.minimum(i + 1, n_tiles - 1),), memory_space=pltpu.SMEM)
    tile = pl.BlockSpec((tm, d), lambda i: (i, 0))
    vec = pl.BlockSpec((1, d), lambda i: (0, 0))
    out = pl.pallas_call(
        functools.partial(_combine_ln_kernel, alpha=alpha, tm=tm),
        grid=(n_tiles,),
        in_specs=[cur, cur, nxt, nxt, pl.BlockSpec(memory_space=pl.ANY),
                  pl.BlockSpec((tm, LANES), lambda i: (i, 0)), tile,
                  pl.BlockSpec((1, 1, d), lambda i: (i // per_seq, 0, 0)), vec, vec],
        out_specs=tile,
        out_shape=jax.ShapeDtypeStruct((n, d), F32),
        scratch_shapes=[pltpu.VMEM((2, 2, tm * sub, LANES), F32), pltpu.SemaphoreType.DMA((2,))],
        compiler_params=_params("arbitrary"),
        name="combine_ln",
    )(slot_a, slot_b, slot_a, slot_b, ys_tiles, route.reshape(n, LANES), x.reshape(n, d), g1p, lng, lnb)
    return out.reshape(b, s, d)


def _moe_plan(route, tm):
    n = route.shape[0]
    e = N_EXPERTS
    n_tiles = (TOP_K * n + e * (tm - 1)) // tm
    flat_e = route[:, :TOP_K].astype(jnp.int32).reshape(-1)
    onehot = (flat_e[:, None] == jnp.arange(e, dtype=jnp.int32)[None, :]).astype(jnp.int32)
    incl = jnp.cumsum(onehot, axis=0)
    counts = incl[-1]
    padded = (counts + tm - 1) // tm * tm
    ends = jnp.cumsum(padded)
    offs = ends - padded
    slot = jnp.sum(onehot * (offs[None, :] + incl - 1), axis=1)
    n_active = ends[-1] // tm
    tile_id = jnp.minimum(jnp.arange(n_tiles + MOE_AHEAD, dtype=jnp.int32), n_active - 1)
    tile_expert = jnp.minimum(jnp.sum((tile_id[:, None] * tm >= ends[None, :]).astype(jnp.int32), axis=1), e - 1)
    by_expert = jnp.argsort(flat_e, stable=True).astype(jnp.int32) // TOP_K
    by_expert = jnp.pad(by_expert, (0, n_tiles * tm - TOP_K * n))
    row_expert = jnp.repeat(tile_expert[:n_tiles], tm)
    shift = offs - (jnp.cumsum(counts) - counts)
    src = jnp.zeros((n_tiles * tm,), jnp.int32)
    for k in range(e):
        src = jnp.where(row_expert == k, jnp.roll(by_expert, shift[k]), src)
    return ((slot * ROW_TILE).reshape(n, TOP_K), src * ROW_TILE, tile_expert.astype(jnp.int32),
            n_active.reshape(1).astype(jnp.int32))


def _moe_swiglu_ln(u_tiles, route, wg, wu, wd, x, g1p, lng, lnb, alpha, tiles):
    assert TOP_K == 2
    b, s, d = x.shape
    n = b * s
    nf = tiles.moe_slabs
    tf = wg.shape[2] // nf
    assert tiles.moe_rows % nf == 0
    slot, src, tile_expert, n_active = _moe_plan(route.reshape(n, LANES), tiles.moe_rows)
    ys = _moe_ffn(u_tiles.reshape(n * (d // LANES), LANES), src, wg, wu, wd, tile_expert, n_active,
                  tiles.moe_rows, tf)
    return _combine_ln(ys, slot[:, 0], slot[:, 1], route, x, g1p, lng, lnb, alpha, tiles.combine_rows)


class _Tiles(NamedTuple):
    token_rows: int
    combine_rows: int
    attn_heads: int
    attn_chunk: int
    moe_rows: int
    moe_slabs: int


def _tiles(s, d, d_ff_expert):
    return _Tiles(token_rows=1024,
                  combine_rows=512,
                  attn_heads=math.gcd(d // HEAD_DIM, 8),
                  attn_chunk=math.gcd(s // MOBA_BLOCK, 4),
                  moe_rows=896,
                  moe_slabs=2 if d_ff_expert % (2 * MXU_N) == 0 else 1)


def kernel(x, c, positions, ada_w, ada_b, ln_g, ln_b, attn_w_in, attn_w_out, rec_w_in, rec_lb_logits,
           rec_norm_w, rec_w_out, ffn_w_gate, ffn_w_up, ffn_w_down, router_w, moe_w_gate, moe_w_up,
           moe_w_down):
    b, s, d = x.shape
    depth = ada_w.shape[0]
    alpha = (2.0 * depth) ** 0.25
    tiles = _tiles(s, d, moe_w_gate.shape[-1])
    ts = tiles.token_rows

    mods = _ada_mods(c, ada_w, ada_b).reshape(depth, b, 6, 1, d)
    one_plus = lambda t: 1.0 + t
    pos = positions.reshape(b, s, 1)

    for i in range(depth):
        shift_m, scale_m, gate_m, shift_f, scale_f, gate_f = (mods[i, :, r] for r in range(6))
        j = i // 2
        lng = ln_g[i].reshape(2, 1, d)
        lnb = ln_b[i].reshape(2, 1, d)
        if i % 2 == 0:
            qkv = _qkv_proj(x, one_plus(scale_m), shift_m, pos, attn_w_in[j].astype(BF16), ts)
            o = _moba_attention(qkv, d, n_grp=tiles.attn_heads, ch=tiles.attn_chunk)
            x = _proj_ffn_dense(o, attn_w_out[j].astype(BF16), x, one_plus(gate_m), lng, lnb,
                                one_plus(scale_f), shift_f, ffn_w_gate[j].astype(BF16),
                                ffn_w_up[j].astype(BF16), ffn_w_down[j].astype(BF16), one_plus(gate_f),
                                alpha, ts)
        else:
            qf, key, v, gs, lf = _rec_in(x, one_plus(scale_m), shift_m, rec_w_in[j].astype(BF16),
                                         rec_lb_logits, i, ts)
            o = _hgrn_recurrence(qf, key, v, lf, gs, rec_norm_w[j].reshape(1, d), ts)
            w_hi = router_w[j].astype(BF16)
            w_lo = (router_w[j] - w_hi.astype(F32)).astype(BF16)
            wr = (jnp.zeros((d, 2 * LANES), BF16).at[:, :N_EXPERTS].set(w_hi)
                  .at[:, LANES:LANES + N_EXPERTS].set(w_lo))
            x, u, route = _proj_ln_route(o, rec_w_out[j].astype(BF16), x, one_plus(gate_m), lng[0], lnb[0],
                                         one_plus(scale_f), shift_f, wr, alpha, ts)
            x = _moe_swiglu_ln(u, route, moe_w_gate[j].astype(BF16), moe_w_up[j].astype(BF16),
                               moe_w_down[j].astype(BF16), x, one_plus(gate_f), lng[1], lnb[1], alpha, tiles)
    return x
```

```python
import functools
import math
from typing import NamedTuple

import jax
import jax.numpy as jnp
from jax import lax
from jax.experimental import pallas as pl
from jax.experimental.pallas import tpu as pltpu

HEAD_DIM = 128
ROPE_DIM = HEAD_DIM // 4
ROPE_THETA = 500000.0
MOBA_BLOCK = 256
MOBA_TOPK = 3
HGRN_EXPAND = 128
HGRN_CHUNK = 64
N_EXPERTS = 8
TOP_K = 2
LN_EPS = 1e-5
RMS_EPS = 1e-6

LANES = 128
MXU_N = 256
VMEM_LIMIT = 56 * 1024 * 1024
NEG_BIG = -1e30

F32 = jnp.float32
BF16 = jnp.bfloat16
HIGHEST = lax.Precision.HIGHEST


def _params(*sem):
    return pltpu.CompilerParams(dimension_semantics=sem, vmem_limit_bytes=VMEM_LIMIT)


def _dot(a, b):
    return jnp.dot(a, b, preferred_element_type=F32)


def _dot_nt(a, b, precision=None):
    return lax.dot_general(a, b, (((1,), (1,)), ((), ())), precision=precision,
                           preferred_element_type=F32)


def _silu(x):
    return x * jax.nn.sigmoid(x)


def _layer_norm(z, g, b):
    mu = jnp.mean(z, axis=-1, keepdims=True)
    d = z - mu
    var = jnp.mean(d * d, axis=-1, keepdims=True)
    return d * lax.rsqrt(var + LN_EPS) * g + b


def _ada_kernel(c_ref, w_ref, b_ref, o_ref):
    a = _silu(c_ref[...])
    o_ref[0] = jnp.dot(a, w_ref[0], precision=HIGHEST, preferred_element_type=F32) + b_ref[0]


def _ada_mods(c, ada_w, ada_b):
    depth, d, m = ada_w.shape
    b = c.shape[0]
    tn = m // 4
    return pl.pallas_call(
        _ada_kernel,
        grid=(depth, m // tn),
        in_specs=[pl.BlockSpec((b, d), lambda l, j: (0, 0)),
                  pl.BlockSpec((1, d, tn), lambda l, j: (l, 0, j)),
                  pl.BlockSpec((1, 1, tn), lambda l, j: (l, 0, j))],
        out_specs=pl.BlockSpec((1, b, tn), lambda l, j: (l, 0, j)),
        out_shape=jax.ShapeDtypeStruct((depth, b, m), F32),
        compiler_params=_params("arbitrary", "arbitrary"),
        name="ada_mods",
    )(c, ada_w, ada_b.reshape(depth, 1, m))


def _qkv_kernel(x_ref, sc_ref, sh_ref, pos_ref, w_ref, o_ref, trig_scr, *, n_rot):
    u = (x_ref[0] * sc_ref[0] + sh_ref[0]).astype(BF16)
    half = ROPE_DIM // 2
    pack = LANES // ROPE_DIM
    rows = x_ref.shape[1] // pack
    lane = lax.broadcasted_iota(jnp.int32, (1, LANES), 1)
    inv = jnp.exp(-math.log(ROPE_THETA) * (lane % half).astype(F32) * (2.0 / ROPE_DIM))
    pos = jnp.zeros((rows, LANES), F32)
    for m in range(pack):
        pos = jnp.where(lane // ROPE_DIM == m, pos_ref[0, pl.ds(m, rows, stride=pack), :].astype(F32), pos)
    ang = pos * inv
    for t, trig in enumerate((jnp.cos(ang), jnp.sin(ang))):
        for m in range(pack):
            moved = trig if m == 0 else pltpu.roll(trig, LANES - ROPE_DIM * m, 1)
            trig_scr[t, pl.ds(m, rows, stride=pack), :] = moved
    cos_t = jnp.where(lane < ROPE_DIM, trig_scr[0], 1.0)
    sin_t = jnp.where(lane < ROPE_DIM, trig_scr[1], 0.0)
    sin_lo = jnp.where(lane < half, -sin_t, 0.0)
    sin_hi = jnp.where((lane >= half) & (lane < ROPE_DIM), sin_t, 0.0)
    n_cols = w_ref.shape[1]
    for c in range(n_cols // MXU_N):
        y = _dot(u, w_ref[:, c * MXU_N:(c + 1) * MXU_N])
        for s in range(MXU_N // HEAD_DIM):
            col = c * MXU_N + s * HEAD_DIM
            t = y[:, s * HEAD_DIM:(s + 1) * HEAD_DIM]
            if col < n_rot:
                t = (t * cos_t + pltpu.roll(t, HEAD_DIM - half, 1) * sin_lo
                     + pltpu.roll(t, half, 1) * sin_hi)
            if col < n_rot // 2:
                t = t * (HEAD_DIM ** -0.5 * math.log2(math.e))
            o_ref[0, :, col:col + HEAD_DIM] = t.astype(BF16)


def _qkv_proj(x, sc, sh, pos, w, ts):
    b, s, d = x.shape
    n = w.shape[1]
    return pl.pallas_call(
        functools.partial(_qkv_kernel, n_rot=2 * d),
        grid=(b, s // ts),
        in_specs=[pl.BlockSpec((1, ts, d), lambda i, j: (i, j, 0)),
                  pl.BlockSpec((1, 1, d), lambda i, j: (i, 0, 0)),
                  pl.BlockSpec((1, 1, d), lambda i, j: (i, 0, 0)),
                  pl.BlockSpec((1, ts, 1), lambda i, j: (i, j, 0)),
                  pl.BlockSpec((d, n), lambda i, j: (0, 0))],
        out_specs=pl.BlockSpec((1, ts, n), lambda i, j: (i, j, 0)),
        out_shape=jax.ShapeDtypeStruct((b, s, n), BF16),
        scratch_shapes=[pltpu.VMEM((2, ts, LANES), F32)],
        compiler_params=_params("arbitrary", "arbitrary"),
        name="qkv_proj",
    )(x, sc, sh, pos, w)


def _moba_kernel(q_ref, k_ref, v_ref, o_ref, kmean_scr, vt_scr, bias_scr, m_scr, l_scr, acc_scr,
                 *, nb, n_grp, ch):
    blk = MOBA_BLOCK
    hd = HEAD_DIM
    cw = ch * blk
    i = pl.program_id(2)

    @pl.when(i == 0)
    def _():
        for hh in range(n_grp):
            means = []
            for n in range(nb):
                kb = k_ref[0, n * blk:(n + 1) * blk, hh * hd:(hh + 1) * hd].astype(F32)
                means.append(jnp.mean(kb, axis=0, keepdims=True))
                vb = v_ref[0, n * blk:(n + 1) * blk, hh * hd:(hh + 1) * hd]
                vt_scr[hh, n // ch, :, (n % ch) * blk:(n % ch + 1) * blk] = vb.astype(F32).T.astype(BF16)
            km = jnp.concatenate(means, axis=0)
            km_hi = km.astype(BF16)
            km_lo = (km - km_hi.astype(F32)).astype(BF16)
            kmean_scr[hh] = jnp.concatenate([km_hi, km_lo], axis=0)

    def sweep(c, n_own):
        own = n_own > 0
        n_blk = n_own if own else ch
        k0 = pl.multiple_of(c * cw, cw)
        heads = [slice(hh * hd, (hh + 1) * hd) for hh in range(n_grp)]
        scores = [_dot_nt(k_ref[0, pl.ds(k0, n_blk * blk), cols], q_ref[0, :, cols])
                  for cols in heads]
        m_news, l_news, probs = [], [], []
        for hh, s in enumerate(scores):
            parts = [s[t * blk:(t + 1) * blk] + bias_scr[hh, pl.ds(c * ch + t, 1), :]
                     for t in range(n_blk - 1 if own else n_blk)]
            if own:
                s_own = s[(n_blk - 1) * blk:]
                kpos = lax.broadcasted_iota(jnp.int32, s_own.shape, 0)
                qpos = lax.broadcasted_iota(jnp.int32, s_own.shape, 1)
                parts.append(jnp.where(kpos <= qpos, s_own, NEG_BIG))
            m_blk = parts[0]
            for t in range(1, n_blk):
                m_blk = jnp.maximum(m_blk, parts[t])
            m_new = jnp.max(m_blk, axis=0, keepdims=True)
            if not own:
                m_new = jnp.maximum(m_new, m_scr[hh])
            ps = [jnp.exp2(part - m_new) for part in parts]
            l_new = ps[0]
            for t in range(1, n_blk):
                l_new = l_new + ps[t]
            m_news.append(m_new)
            l_news.append(jnp.sum(l_new, axis=0, keepdims=True))
            probs.append(jnp.concatenate([p.astype(BF16) for p in ps], axis=0) if n_blk > 1
                         else ps[0].astype(BF16))
        pvs = [_dot(vt_scr[hh, c, :, 0:n_blk * blk], probs[hh]) for hh in range(n_grp)]
        for hh in range(n_grp):
            if own:
                l_scr[hh] = l_news[hh]
                acc_scr[hh] = pvs[hh]
            else:
                alpha = jnp.exp2(m_scr[hh] - m_news[hh])
                l_scr[hh] = alpha * l_scr[hh] + l_news[hh]
                acc_scr[hh] = alpha * acc_scr[hh] + pvs[hh]
            m_scr[hh] = m_news[hh]

    c_own = i // ch
    for hh in range(n_grp):
        q = q_ref[0, :, hh * hd:(hh + 1) * hd]
        gate2 = _dot_nt(kmean_scr[hh], q)
        gate = gate2[:nb] + gate2[nb:]
        blk_id = lax.broadcasted_iota(jnp.int32, gate.shape, 0)
        past = blk_id < i
        g = jnp.where(past, gate, -jnp.inf)
        rank = jnp.zeros(gate.shape, jnp.int32)
        for m in range(nb):
            gm = g[m:m + 1, :]
            beats = jnp.where(gm > g, 1, jnp.where(gm == g, jnp.where(blk_id > m, 1, 0), 0))
            rank = rank + beats
        bias_scr[hh] = jnp.where(past, jnp.where(rank < MOBA_TOPK, 0.0, NEG_BIG), NEG_BIG)

    for r in range(ch):
        @pl.when(i % ch == r)
        def _():
            sweep(c_own, r + 1)

    def body(c, carry):
        sweep(c, 0)
        return carry

    lax.fori_loop(0, c_own, body, 0)
    for hh in range(n_grp):
        o_ref[0, :, hh * hd:(hh + 1) * hd] = (acc_scr[hh] / l_scr[hh]).T.astype(BF16)


def _moba_attention(qkv, d, n_grp, ch):
    b, s, _ = qkv.shape
    h = d // HEAD_DIM
    blk = MOBA_BLOCK
    nb = s // blk
    gw = n_grp * HEAD_DIM
    ng = h // n_grp
    return pl.pallas_call(
        functools.partial(_moba_kernel, nb=nb, n_grp=n_grp, ch=ch),
        grid=(b, ng, nb),
        in_specs=[pl.BlockSpec((1, blk, gw), lambda bi, hi, i: (bi, i, hi)),
                  pl.BlockSpec((1, s, gw), lambda bi, hi, i: (bi, 0, ng + hi)),
                  pl.BlockSpec((1, s, gw), lambda bi, hi, i: (bi, 0, 2 * ng + hi))],
        out_specs=pl.BlockSpec((1, blk, gw), lambda bi, hi, i: (bi, i, hi)),
        out_shape=jax.ShapeDtypeStruct((b, s, d), BF16),
        scratch_shapes=[pltpu.VMEM((n_grp, 2 * nb, HEAD_DIM), BF16),
                        pltpu.VMEM((n_grp, nb // ch, HEAD_DIM, ch * blk), BF16),
                        pltpu.VMEM((n_grp, nb, blk), F32),
                        pltpu.VMEM((n_grp, 1, blk), F32),
                        pltpu.VMEM((n_grp, 1, blk), F32),
                        pltpu.VMEM((n_grp, HEAD_DIM, blk), F32)],
        compiler_params=_params("arbitrary", "arbitrary", "arbitrary"),
        name="moba_attention",
    )(qkv, qkv, qkv)


def _store_row_tiles(ref, val):
    rows, d = val.shape
    sub = d // LANES
    for c in range(sub):
        ref[pl.ds(c, rows, stride=sub), :] = val[:, c * LANES:(c + 1) * LANES]


def _load_row_tiles(ref, rows, dtype):
    sub = ref.shape[0] // rows
    return jnp.concatenate([ref[pl.ds(c, rows, stride=sub), :].astype(dtype) for c in range(sub)], axis=1)


def _route_top2(u, wr_ref):
    u_hi = u.astype(BF16)
    u_lo = (u - u_hi.astype(F32)).astype(BF16)
    both = _dot(u_hi, wr_ref[...])
    logits = both[:, :LANES] + both[:, LANES:] + _dot(u_lo, wr_ref[:, :LANES])
    lane = lax.broadcasted_iota(jnp.int32, logits.shape, 1)
    logits = jnp.where(lane < N_EXPERTS, logits, -jnp.inf)
    m1 = jnp.max(logits, axis=-1, keepdims=True)
    i1 = jnp.min(jnp.where(logits == m1, lane, LANES), axis=-1, keepdims=True)
    rest = jnp.where(lane == i1, -jnp.inf, logits)
    m2 = jnp.max(rest, axis=-1, keepdims=True)
    i2 = jnp.min(jnp.where(rest == m2, lane, LANES), axis=-1, keepdims=True)
    e2 = jnp.exp(m2 - m1)
    w1 = 1.0 / (1.0 + e2)
    w2 = e2 / (1.0 + e2)
    return jnp.where(lane == 0, i1.astype(F32),
                     jnp.where(lane == 1, i2.astype(F32),
                               jnp.where(lane == 2, w1, jnp.where(lane == 3, w2, 0.0))))


def _proj_ln_route_kernel(a_ref, w_ref, x_ref, g1p_ref, lng_ref, lnb_ref, sc_ref, sh_ref, wr_ref,
                          xo_ref, uo_ref, ro_ref, *, alpha):
    y = _dot(a_ref[0], w_ref[...])
    xn = _layer_norm(alpha * x_ref[0] + g1p_ref[0] * y, lng_ref[...], lnb_ref[...])
    xo_ref[0] = xn
    u = xn * sc_ref[0] + sh_ref[0]
    _store_row_tiles(uo_ref.at[0], u)
    ro_ref[0] = _route_top2(u, wr_ref)


def _proj_ln_route(a, w, x, g1p, lng, lnb, sc, sh, wr, alpha, tm):
    b, s, d = x.shape
    sub = d // LANES
    tile = pl.BlockSpec((1, tm, d), lambda i, j: (i, j, 0))
    per_b = pl.BlockSpec((1, 1, d), lambda i, j: (i, 0, 0))
    vec = pl.BlockSpec((1, d), lambda i, j: (0, 0))
    return pl.pallas_call(
        functools.partial(_proj_ln_route_kernel, alpha=alpha),
        grid=(b, s // tm),
        in_specs=[tile, pl.BlockSpec((d, d), lambda i, j: (0, 0)), tile, per_b, vec, vec, per_b, per_b,
                  pl.BlockSpec((d, 2 * LANES), lambda i, j: (0, 0))],
        out_specs=[tile, pl.BlockSpec((1, tm * sub, LANES), lambda i, j: (i, j, 0)),
                   pl.BlockSpec((1, tm, LANES), lambda i, j: (i, j, 0))],
        out_shape=[jax.ShapeDtypeStruct((b, s, d), F32), jax.ShapeDtypeStruct((b, s * sub, LANES), F32),
                   jax.ShapeDtypeStruct((b, s, LANES), F32)],
        compiler_params=_params("arbitrary", "arbitrary"),
        name="proj_ln_route",
    )(a, w, x, g1p, lng, lnb, sc, sh, wr)


def _proj_ffn_kernel(a_ref, wo_ref, x_ref, g1m_ref, lng_ref, lnb_ref, sc_ref, sh_ref, wg_ref, wu_ref, wd_ref,
                     g1f_ref, xo_ref, h_scr, *, alpha, tf):
    x1 = _layer_norm(alpha * x_ref[0] + g1m_ref[0] * _dot(a_ref[0], wo_ref[...]), lng_ref[0], lnb_ref[0])
    u = (x1 * sc_ref[0] + sh_ref[0]).astype(BF16)
    f = wg_ref.shape[1]
    for j in range(f // tf):
        cols = slice(j * tf, (j + 1) * tf)
        hj = _silu(_dot(u, wg_ref[:, cols])) * _dot(u, wu_ref[:, cols])
        h_scr[:, cols] = hj.astype(BF16)
    y = _dot(h_scr[...], wd_ref[...])
    xo_ref[0] = _layer_norm(alpha * x1 + g1f_ref[0] * y, lng_ref[1], lnb_ref[1])


def _proj_ffn_dense(a, wo, x, g1m, lng, lnb, sc, sh, wg, wu, wd, g1f, alpha, tm):
    b, s, d = x.shape
    f = wg.shape[1]
    tile = pl.BlockSpec((1, tm, d), lambda i, j: (i, j, 0))
    per_b = pl.BlockSpec((1, 1, d), lambda i, j: (i, 0, 0))
    ln_pair = pl.BlockSpec((2, 1, d), lambda i, j: (0, 0, 0))
    resident = dict(pipeline_mode=pl.Buffered(1))
    return pl.pallas_call(
        functools.partial(_proj_ffn_kernel, alpha=alpha, tf=MXU_N),
        grid=(b, s // tm),
        in_specs=[tile, pl.BlockSpec((d, d), lambda i, j: (0, 0), **resident), tile, per_b, ln_pair, ln_pair,
                  per_b, per_b,
                  pl.BlockSpec((d, f), lambda i, j: (0, 0), **resident),
                  pl.BlockSpec((d, f), lambda i, j: (0, 0), **resident),
                  pl.BlockSpec((f, d), lambda i, j: (0, 0), **resident),
                  per_b],
        out_specs=tile,
        out_shape=jax.ShapeDtypeStruct((b, s, d), F32),
        scratch_shapes=[pltpu.VMEM((tm, f), BF16)],
        compiler_params=_params("arbitrary", "arbitrary"),
        name="proj_ffn_dense",
    )(a, wo, x, g1m, lng, lnb, sc, sh, wg, wu, wd, g1f)


def _rec_in_kernel(x_ref, sc_ref, sh_ref, w_ref, lbl_ref, qf_ref, key_ref, v_ref, gs_ref, lf_ref,
                   *, layer_idx):
    u = (x_ref[0] * sc_ref[0] + sh_ref[0]).astype(BF16)
    d = x_ref.shape[2]
    lbl = lbl_ref[...]
    e = jnp.exp(lbl - jnp.max(lbl, axis=0, keepdims=True))
    sm = e / jnp.sum(e, axis=0, keepdims=True)
    lb = jnp.zeros((1, d), F32)
    for r in range(1, layer_idx + 1):
        lb = lb + sm[r:r + 1, :]
    for c in range(d // MXU_N):
        cols = slice(c * MXU_N, (c + 1) * MXU_N)
        q = _dot(u, w_ref[:, c * MXU_N:(c + 1) * MXU_N])
        qf_ref[0, :, cols] = _silu(q).astype(BF16)
        f = _dot(u, w_ref[:, d + c * MXU_N:d + (c + 1) * MXU_N])
        lbc = lb[:, cols]
        f_gate = lbc + (1.0 - lbc) * jax.nn.sigmoid(f)
        lf_ref[0, :, cols] = jnp.log(f_gate)
        key_ref[0, :, cols] = (1.0 - f_gate).astype(BF16)
        v = _dot(u, w_ref[:, 2 * d + c * MXU_N:2 * d + (c + 1) * MXU_N])
        v_ref[0, :, cols] = v.astype(BF16)
        g = _dot(u, w_ref[:, 3 * d + c * MXU_N:3 * d + (c + 1) * MXU_N])
        gs_ref[0, :, cols] = _silu(g).astype(BF16)


def _rec_in(x, sc, sh, w, lb_logits, layer_idx, ts):
    b, s, d = x.shape
    depth = lb_logits.shape[0]
    tile = pl.BlockSpec((1, ts, d), lambda i, j: (i, j, 0))
    per_b = pl.BlockSpec((1, 1, d), lambda i, j: (i, 0, 0))
    bf = jax.ShapeDtypeStruct((b, s, d), BF16)
    return pl.pallas_call(
        functools.partial(_rec_in_kernel, layer_idx=layer_idx),
        grid=(b, s // ts),
        in_specs=[tile, per_b, per_b,
                  pl.BlockSpec((d, 4 * d), lambda i, j: (0, 0)),
                  pl.BlockSpec((depth, d), lambda i, j: (0, 0))],
        out_specs=[tile] * 5,
        out_shape=[bf, bf, bf, bf, jax.ShapeDtypeStruct((b, s, d), F32)],
        compiler_params=_params("arbitrary", "arbitrary"),
        name="rec_in",
    )(x, sc, sh, w, lb_logits)


def _hgrn_kernel(qf_ref, key_ref, v_ref, lf_ref, gs_ref, nw_ref, o_ref, st_scr, *, n_heads):
    cs = HGRN_CHUNK
    dk = HGRN_EXPAND

    @pl.when(pl.program_id(1) == 0)
    def _():
        st_scr[...] = jnp.zeros(st_scr.shape, F32)

    row = lax.broadcasted_iota(jnp.int32, (cs, cs), 0)
    col = lax.broadcasted_iota(jnp.int32, (cs, cs), 1)
    causal = row >= col
    tri = jnp.where(causal, 1.0, 0.0).astype(BF16)
    n_chunks = qf_ref.shape[1] // cs

    def chunk(c, carry):
        rows = pl.ds(pl.multiple_of(c * cs, cs), cs)
        lf = lf_ref[0, rows, :]
        hi = lf.astype(BF16)
        r1 = lf - hi.astype(F32)
        mid = r1.astype(BF16)
        lo = (r1 - mid.astype(F32)).astype(BF16)
        g = _dot(tri, hi) + _dot(tri, mid) + _dot(tri, lo)
        g_last = g[cs - 1:cs, :]
        qf = qf_ref[0, rows, :].astype(F32)
        key = key_ref[0, rows, :].astype(F32)
        v = v_ref[0, rows, :]
        q_dec = (qf * jnp.exp(g)).astype(BF16)
        k_dec = (key * jnp.exp(-g)).astype(BF16)
        k_state = (key * jnp.exp(g_last - g)).astype(BF16)
        decay = jnp.exp(g_last)
        heads = [slice(h * dk, (h + 1) * dk) for h in range(n_heads)]
        a = [jnp.where(causal, _dot_nt(q_dec[:, c], k_dec[:, c]), 0.0).astype(BF16) for c in heads]
        st = [st_scr[h] for h in range(n_heads)]
        o = [_dot(a[h], v[:, c]) + _dot_nt(q_dec[:, c], st[h].astype(BF16))
             for h, c in enumerate(heads)]
        for h, c in enumerate(heads):
            v_t = v[:, c].astype(F32).T.astype(BF16)
            st_scr[h] = st[h] * decay[:, c] + _dot(v_t, k_state[:, c])
        on = jnp.concatenate(
            [oh * lax.rsqrt(jnp.mean(oh * oh, axis=-1, keepdims=True) + RMS_EPS) for oh in o], axis=1)
        o_ref[0, rows, :] = (on * nw_ref[...] * gs_ref[0, rows, :].astype(F32)).astype(BF16)
        return carry

    lax.fori_loop(0, n_chunks, chunk, 0, unroll=4)


def _hgrn_recurrence(qf, key, v, lf, gs, norm_w, ts):
    b, s, d = qf.shape
    h = d // HGRN_EXPAND
    tile = pl.BlockSpec((1, ts, d), lambda i, j: (i, j, 0))
    return pl.pallas_call(
        functools.partial(_hgrn_kernel, n_heads=h),
        grid=(b, s // ts),
        in_specs=[tile, tile, tile, tile, tile, pl.BlockSpec((1, d), lambda i, j: (0, 0))],
        out_specs=tile,
        out_shape=jax.ShapeDtypeStruct((b, s, d), BF16),
        scratch_shapes=[pltpu.VMEM((h, HGRN_EXPAND, HGRN_EXPAND), F32)],
        compiler_params=_params("arbitrary", "arbitrary"),
        name="hgrn_recurrence",
    )(qf, key, v, lf, gs, norm_w)


GATHER_UNROLL = 8


ROW_TILE = 8


def _start_row(idx_at, src_hbm, dst, sem, r, priority=0):
    first = pl.multiple_of(idx_at(r), ROW_TILE)
    pltpu.make_async_copy(src_hbm.at[pl.ds(first, ROW_TILE), :],
                          dst.at[pl.ds(r * ROW_TILE, ROW_TILE), :], sem).start(priority=priority)


def _start_row_gather(idx_at, src_hbm, dst, sem, n_rows, both_queues=False):
    ways = 2 if both_queues else 1

    def group(g, carry):
        for k in range(ways):
            _start_row(idx_at, src_hbm, dst, sem, g * ways + k, priority=k)
        return carry

    lax.fori_loop(0, n_rows // ways, group, 0, unroll=GATHER_UNROLL // ways)


def _wait_rows(buf, sem):
    pltpu.make_async_copy(buf, buf, sem).wait()


MOE_AHEAD = 2
MOE_XBUFS = MOE_AHEAD + 1


def _moe_ffn_kernel(te_ref, na_ref, src_t0_ref, src_t1_ref, src_ahead_ref, u_hbm, wg_ref, wu_ref, wd_ref,
                    o_ref, xbuf, xb_scr, h_scr, acc_scr, sems, *, tm, nf, n_tiles):
    i = pl.program_id(0)
    j = pl.program_id(1)
    rows_per_step = tm // nf
    n_active = na_ref[0]
    active = i < n_active
    slot = i % MOE_XBUFS
    ahead = (i + MOE_AHEAD) % MOE_XBUFS

    assert MOE_AHEAD == 2

    @pl.when((j == 0) & (i == 0))
    def _():
        _start_row_gather(lambda r: src_t0_ref[0, 0, r], u_hbm, xbuf.at[0], sems.at[0], tm)
        _start_row_gather(lambda r: src_t1_ref[0, 0, r], u_hbm, xbuf.at[1], sems.at[1], tm)

    @pl.when((j == 0) & (i <= n_active + 1))
    def _():
        _wait_rows(xbuf.at[slot], sems.at[slot])

    @pl.when(active & (j == 0))
    def _():
        xb_scr[...] = _load_row_tiles(xbuf.at[slot], tm, BF16)

    @pl.when(active)
    def _():
        for k in range(rows_per_step):
            _start_row(lambda r: src_ahead_ref[0, 0, r], u_hbm, xbuf.at[ahead], sems.at[ahead],
                       j * rows_per_step + k)
        x = xb_scr[...]
        for c in range(wg_ref.shape[2] // MXU_N):
            cols = slice(c * MXU_N, (c + 1) * MXU_N)
            hc = _silu(_dot(x, wg_ref[0, :, cols])) * _dot(x, wu_ref[0, :, cols])
            h_scr[:, cols] = hc.astype(BF16)
        y = _dot(h_scr[...], wd_ref[0])
        if nf == 1:
            _store_row_tiles(o_ref, y)
        else:
            @pl.when(j == 0)
            def _():
                acc_scr[...] = y

            if nf > 2:
                @pl.when((j > 0) & (j < nf - 1))
                def _():
                    acc_scr[...] += y

            @pl.when(j == nf - 1)
            def _():
                _store_row_tiles(o_ref, acc_scr[...] + y)

    @pl.when(jnp.logical_not(active) & (j == nf - 1) & (i < n_tiles))
    def _():
        o_ref[...] = jnp.zeros(o_ref.shape, F32)


def _moe_ffn(u_tiles, src, wg, wu, wd, tile_expert, n_active, tm, tf):
    d = wd.shape[2]
    sub = d // LANES
    assert sub == ROW_TILE
    n_tiles = src.shape[0] // tm
    assert tile_expert.shape[0] == n_tiles + MOE_AHEAD
    nf = wg.shape[2] // tf

    def frozen(j, i, na):
        return jnp.where(i < na[0], j, nf - 1)

    def idx_block(tile_of):
        return pl.BlockSpec((1, 1, tm), lambda i, j, te, na: (jnp.minimum(tile_of(i), n_tiles - 1), 0, 0),
                            memory_space=pltpu.SMEM)

    grid_spec = pltpu.PrefetchScalarGridSpec(
        num_scalar_prefetch=2,
        grid=(n_tiles + MOE_AHEAD, nf),
        in_specs=[idx_block(lambda i: 0), idx_block(lambda i: 1), idx_block(lambda i: i + MOE_AHEAD),
                  pl.BlockSpec(memory_space=pl.ANY),
                  pl.BlockSpec((1, d, tf), lambda i, j, te, na: (te[i], 0, frozen(j, i, na))),
                  pl.BlockSpec((1, d, tf), lambda i, j, te, na: (te[i], 0, frozen(j, i, na))),
                  pl.BlockSpec((1, tf, d), lambda i, j, te, na: (te[i], frozen(j, i, na), 0))],
        out_specs=pl.BlockSpec((tm * sub, LANES), lambda i, j, te, na: (jnp.minimum(i, n_tiles - 1), 0)),
        scratch_shapes=[pltpu.VMEM((MOE_XBUFS, tm * sub, LANES), F32),
                        pltpu.VMEM((tm, d), BF16),
                        pltpu.VMEM((tm, tf), BF16),
                        pltpu.VMEM((tm, d), F32),
                        pltpu.SemaphoreType.DMA((MOE_XBUFS,))],
    )
    src3 = src.reshape(n_tiles, 1, tm)
    return pl.pallas_call(
        functools.partial(_moe_ffn_kernel, tm=tm, nf=nf, n_tiles=n_tiles),
        grid_spec=grid_spec,
        out_shape=jax.ShapeDtypeStruct((n_tiles * tm * sub, LANES), F32),
        compiler_params=_params("arbitrary", "arbitrary"),
        name="moe_ffn",
    )(tile_expert, n_active, src3, src3, src3, u_tiles, wg, wu, wd)


def _combine_ln_kernel(sa_cur, sb_cur, sa_nxt, sb_nxt, ys_hbm, r_ref, x_ref, g1p_ref, lng_ref, lnb_ref,
                       xo_ref, ybuf, sems, *, alpha, tm):
    i = pl.program_id(0)
    slot = i % 2

    def start(sa, sb, s):
        _start_row_gather(lambda r: sa[r], ys_hbm, ybuf.at[s, 0], sems.at[s], tm, both_queues=True)
        _start_row_gather(lambda r: sb[r], ys_hbm, ybuf.at[s, 1], sems.at[s], tm, both_queues=True)

    @pl.when(i == 0)
    def _():
        start(sa_cur, sb_cur, 0)

    @pl.when(i + 1 < pl.num_programs(0))
    def _():
        start(sa_nxt, sb_nxt, 1 - slot)

    _wait_rows(ybuf.at[slot], sems.at[slot])
    r = r_ref[...]
    y = (r[:, 2:3] * _load_row_tiles(ybuf.at[slot, 0], tm, F32)
         + r[:, 3:4] * _load_row_tiles(ybuf.at[slot, 1], tm, F32))
    xo_ref[...] = _layer_norm(alpha * x_ref[...] + g1p_ref[0] * y, lng_ref[...], lnb_ref[...])


def _combine_ln(ys_tiles, slot_a, slot_b, route, x, g1p, lng, lnb, alpha, tm):
    b, s, d = x.shape
    n = b * s
    sub = d // LANES
    n_tiles = n // tm
    per_seq = s // tm
    cur = pl.BlockSpec((tm,), lambda i: (i,), memory_space=pltpu.SMEM)
    nxt = pl.BlockSpec((tm,), lambda i: (jnp.minimum(i + 1, n_tiles - 1),), memory_space=pltpu.SMEM)
    tile = pl.BlockSpec((tm, d), lambda i: (i, 0))
    vec = pl.BlockSpec((1, d), lambda i: (0, 0))
    out = pl.pallas_call(
        functools.partial(_combine_ln_kernel, alpha=alpha, tm=tm),
        grid=(n_tiles,),
        in_specs=[cur, cur, nxt, nxt, pl.BlockSpec(memory_space=pl.ANY),
                  pl.BlockSpec((tm, LANES), lambda i: (i, 0)), tile,
                  pl.BlockSpec((1, 1, d), lambda i: (i // per_seq, 0, 0)), vec, vec],
        out_specs=tile,
        out_shape=jax.ShapeDtypeStruct((n, d), F32),
        scratch_shapes=[pltpu.VMEM((2, 2, tm * sub, LANES), F32), pltpu.SemaphoreType.DMA((2,))],
        compiler_params=_params("arbitrary"),
        name="combine_ln",
    )(slot_a, slot_b, slot_a, slot_b, ys_tiles, route.reshape(n, LANES), x.reshape(n, d), g1p, lng, lnb)
    return out.reshape(b, s, d)


def _moe_plan(route, tm):
    n = route.shape[0]
    e = N_EXPERTS
    n_tiles = (TOP_K * n + e * (tm - 1)) // tm
    flat_e = route[:, :TOP_K].astype(jnp.int32).reshape(-1)
    onehot = (flat_e[:, None] == jnp.arange(e, dtype=jnp.int32)[None, :]).astype(jnp.int32)
    incl = jnp.cumsum(onehot, axis=0)
    counts = incl[-1]
    padded = (counts + tm - 1) // tm * tm
    ends = jnp.cumsum(padded)
    offs = ends - padded
    slot = jnp.sum(onehot * (offs[None, :] + incl - 1), axis=1)
    n_active = ends[-1] // tm
    tile_id = jnp.minimum(jnp.arange(n_tiles + MOE_AHEAD, dtype=jnp.int32), n_active - 1)
    tile_expert = jnp.minimum(jnp.sum((tile_id[:, None] * tm >= ends[None, :]).astype(jnp.int32), axis=1), e - 1)
    by_expert = jnp.argsort(flat_e, stable=True).astype(jnp.int32) // TOP_K
    by_expert = jnp.pad(by_expert, (0, n_tiles * tm - TOP_K * n))
    row_expert = jnp.repeat(tile_expert[:n_tiles], tm)
    shift = offs - (jnp.cumsum(counts) - counts)
    src = jnp.zeros((n_tiles * tm,), jnp.int32)
    for k in range(e):
        src = jnp.where(row_expert == k, jnp.roll(by_expert, shift[k]), src)
    return ((slot * ROW_TILE).reshape(n, TOP_K), src * ROW_TILE, tile_expert.astype(jnp.int32),
            n_active.reshape(1).astype(jnp.int32))


def _moe_swiglu_ln(u_tiles, route, wg, wu, wd, x, g1p, lng, lnb, alpha, tiles):
    assert TOP_K == 2
    b, s, d = x.shape
    n = b * s
    nf = tiles.moe_slabs
    tf = wg.shape[2] // nf
    assert tiles.moe_rows % nf == 0
    slot, src, tile_expert, n_active = _moe_plan(route.reshape(n, LANES), tiles.moe_rows)
    ys = _moe_ffn(u_tiles.reshape(n * (d // LANES), LANES), src, wg, wu, wd, tile_expert, n_active,
                  tiles.moe_rows, tf)
    return _combine_ln(ys, slot[:, 0], slot[:, 1], route, x, g1p, lng, lnb, alpha, tiles.combine_rows)


class _Tiles(NamedTuple):
    token_rows: int
    combine_rows: int
    attn_heads: int
    attn_chunk: int
    moe_rows: int
    moe_slabs: int


def _tiles(s, d, d_ff_expert):
    return _Tiles(token_rows=1024,
                  combine_rows=512,
                  attn_heads=math.gcd(d // HEAD_DIM, 8),
                  attn_chunk=math.gcd(s // MOBA_BLOCK, 4),
                  moe_rows=896,
                  moe_slabs=2 if d_ff_expert % (2 * MXU_N) == 0 else 1)


def kernel(x, c, positions, ada_w, ada_b, ln_g, ln_b, attn_w_in, attn_w_out, rec_w_in, rec_lb_logits,
           rec_norm_w, rec_w_out, ffn_w_gate, ffn_w_up, ffn_w_down, router_w, moe_w_gate, moe_w_up,
           moe_w_down):
    b, s, d = x.shape
    depth = ada_w.shape[0]
    alpha = (2.0 * depth) ** 0.25
    tiles = _tiles(s, d, moe_w_gate.shape[-1])
    ts = tiles.token_rows

    mods = _ada_mods(c, ada_w, ada_b).reshape(depth, b, 6, 1, d)
    one_plus = lambda t: 1.0 + t
    pos = positions.reshape(b, s, 1)

    for i in range(depth):
        shift_m, scale_m, gate_m, shift_f, scale_f, gate_f = (mods[i, :, r] for r in range(6))
        j = i // 2
        lng = ln_g[i].reshape(2, 1, d)
        lnb = ln_b[i].reshape(2, 1, d)
        if i % 2 == 0:
            qkv = _qkv_proj(x, one_plus(scale_m), shift_m, pos, attn_w_in[j].astype(BF16), ts)
            o = _moba_attention(qkv, d, n_grp=tiles.attn_heads, ch=tiles.attn_chunk)
            x = _proj_ffn_dense(o, attn_w_out[j].astype(BF16), x, one_plus(gate_m), lng, lnb,
                                one_plus(scale_f), shift_f, ffn_w_gate[j].astype(BF16),
                                ffn_w_up[j].astype(BF16), ffn_w_down[j].astype(BF16), one_plus(gate_f),
                                alpha, ts)
        else:
            qf, key, v, gs, lf = _rec_in(x, one_plus(scale_m), shift_m, rec_w_in[j].astype(BF16),
                                         rec_lb_logits, i, ts)
            o = _hgrn_recurrence(qf, key, v, lf, gs, rec_norm_w[j].reshape(1, d), ts)
            w_hi = router_w[j].astype(BF16)
            w_lo = (router_w[j] - w_hi.astype(F32)).astype(BF16)
            wr = (jnp.zeros((d, 2 * LANES), BF16).at[:, :N_EXPERTS].set(w_hi)
                  .at[:, LANES:LANES + N_EXPERTS].set(w_lo))
            x, u, route = _proj_ln_route(o, rec_w_out[j].astype(BF16), x, one_plus(gate_m), lng[0], lnb[0],
                                         one_plus(scale_f), shift_f, wr, alpha, ts)
            x = _moe_swiglu_ln(u, route, moe_w_gate[j].astype(BF16), moe_w_up[j].astype(BF16),
                               moe_w_down[j].astype(BF16), x, one_plus(gate_f), lng[1], lnb[1], alpha, tiles)
    return x
```

```python
import functools
import math
from typing import NamedTuple

import jax
import jax.numpy as jnp
from jax import lax
from jax.experimental import pallas as pl
from jax.experimental.pallas import tpu as pltpu

HEAD_DIM = 128
ROPE_DIM = HEAD_DIM // 4
ROPE_THETA = 500000.0
MOBA_BLOCK = 256
MOBA_TOPK = 3
HGRN_EXPAND = 128
HGRN_CHUNK = 64
N_EXPERTS = 8
TOP_K = 2
LN_EPS = 1e-5
RMS_EPS = 1e-6

LANES = 128
MXU_N = 256
VMEM_LIMIT = 56 * 1024 * 1024
NEG_BIG = -1e30

F32 = jnp.float32
BF16 = jnp.bfloat16
HIGHEST = lax.Precision.HIGHEST


def _params(*sem):
    return pltpu.CompilerParams(dimension_semantics=sem, vmem_limit_bytes=VMEM_LIMIT)


def _dot(a, b):
    return jnp.dot(a, b, preferred_element_type=F32)


def _dot_nt(a, b, precision=None):
    return lax.dot_general(a, b, (((1,), (1,)), ((), ())), precision=precision,
                           preferred_element_type=F32)


def _silu(x):
    return x * jax.nn.sigmoid(x)


def _layer_norm(z, g, b):
    mu = jnp.mean(z, axis=-1, keepdims=True)
    d = z - mu
    var = jnp.mean(d * d, axis=-1, keepdims=True)
    return d * lax.rsqrt(var + LN_EPS) * g + b


def _ada_kernel(c_ref, w_ref, b_ref, o_ref):
    a = _silu(c_ref[...])
    o_ref[0] = jnp.dot(a, w_ref[0], precision=HIGHEST, preferred_element_type=F32) + b_ref[0]


def _ada_mods(c, ada_w, ada_b):
    depth, d, m = ada_w.shape
    b = c.shape[0]
    tn = m // 4
    return pl.pallas_call(
        _ada_kernel,
        grid=(depth, m // tn),
        in_specs=[pl.BlockSpec((b, d), lambda l, j: (0, 0)),
                  pl.BlockSpec((1, d, tn), lambda l, j: (l, 0, j)),
                  pl.BlockSpec((1, 1, tn), lambda l, j: (l, 0, j))],
        out_specs=pl.BlockSpec((1, b, tn), lambda l, j: (l, 0, j)),
        out_shape=jax.ShapeDtypeStruct((depth, b, m), F32),
        compiler_params=_params("arbitrary", "arbitrary"),
        name="ada_mods",
    )(c, ada_w, ada_b.reshape(depth, 1, m))


def _qkv_kernel(x_ref, sc_ref, sh_ref, pos_ref, w_ref, o_ref, trig_scr, *, n_rot):
    u = (x_ref[0] * sc_ref[0] + sh_ref[0]).astype(BF16)
    half = ROPE_DIM // 2
    pack = LANES // ROPE_DIM
    rows = x_ref.shape[1] // pack
    lane = lax.broadcasted_iota(jnp.int32, (1, LANES), 1)
    inv = jnp.exp(-math.log(ROPE_THETA) * (lane % half).astype(F32) * (2.0 / ROPE_DIM))
    pos = jnp.zeros((rows, LANES), F32)
    for m in range(pack):
        pos = jnp.where(lane // ROPE_DIM == m, pos_ref[0, pl.ds(m, rows, stride=pack), :].astype(F32), pos)
    ang = pos * inv
    for t, trig in enumerate((jnp.cos(ang), jnp.sin(ang))):
        for m in range(pack):
            moved = trig if m == 0 else pltpu.roll(trig, LANES - ROPE_DIM * m, 1)
            trig_scr[t, pl.ds(m, rows, stride=pack), :] = moved
    cos_t = jnp.where(lane < ROPE_DIM, trig_scr[0], 1.0)
    sin_t = jnp.where(lane < ROPE_DIM, trig_scr[1], 0.0)
    sin_lo = jnp.where(lane < half, -sin_t, 0.0)
    sin_hi = jnp.where((lane >= half) & (lane < ROPE_DIM), sin_t, 0.0)
    n_cols = w_ref.shape[1]
    for c in range(n_cols // MXU_N):
        y = _dot(u, w_ref[:, c * MXU_N:(c + 1) * MXU_N])
        for s in range(MXU_N // HEAD_DIM):
            col = c * MXU_N + s * HEAD_DIM
            t = y[:, s * HEAD_DIM:(s + 1) * HEAD_DIM]
            if col < n_rot:
                t = (t * cos_t + pltpu.roll(t, HEAD_DIM - half, 1) * sin_lo
                     + pltpu.roll(t, half, 1) * sin_hi)
            if col < n_rot // 2:
                t = t * (HEAD_DIM ** -0.5 * math.log2(math.e))
            o_ref[0, :, col:col + HEAD_DIM] = t.astype(BF16)


def _qkv_proj(x, sc, sh, pos, w, ts):
    b, s, d = x.shape
    n = w.shape[1]
    return pl.pallas_call(
        functools.partial(_qkv_kernel, n_rot=2 * d),
        grid=(b, s // ts),
        in_specs=[pl.BlockSpec((1, ts, d), lambda i, j: (i, j, 0)),
                  pl.BlockSpec((1, 1, d), lambda i, j: (i, 0, 0)),
                  pl.BlockSpec((1, 1, d), lambda i, j: (i, 0, 0)),
                  pl.BlockSpec((1, ts, 1), lambda i, j: (i, j, 0)),
                  pl.BlockSpec((d, n), lambda i, j: (0, 0))],
        out_specs=pl.BlockSpec((1, ts, n), lambda i, j: (i, j, 0)),
        out_shape=jax.ShapeDtypeStruct((b, s, n), BF16),
        scratch_shapes=[pltpu.VMEM((2, ts, LANES), F32)],
        compiler_params=_params("arbitrary", "arbitrary"),
        name="qkv_proj",
    )(x, sc, sh, pos, w)


def _moba_kernel(q_ref, k_ref, v_ref, o_ref, kmean_scr, vt_scr, bias_scr, m_scr, l_scr, acc_scr,
                 *, nb, n_grp, ch):
    blk = MOBA_BLOCK
    hd = HEAD_DIM
    cw = ch * blk
    i = pl.program_id(2)

    @pl.when(i == 0)
    def _():
        for hh in range(n_grp):
            means = []
            for n in range(nb):
                kb = k_ref[0, n * blk:(n + 1) * blk, hh * hd:(hh + 1) * hd].astype(F32)
                means.append(jnp.mean(kb, axis=0, keepdims=True))
                vb = v_ref[0, n * blk:(n + 1) * blk, hh * hd:(hh + 1) * hd]
                vt_scr[hh, n // ch, :, (n % ch) * blk:(n % ch + 1) * blk] = vb.astype(F32).T.astype(BF16)
            km = jnp.concatenate(means, axis=0)
            km_hi = km.astype(BF16)
            km_lo = (km - km_hi.astype(F32)).astype(BF16)
            kmean_scr[hh] = jnp.concatenate([km_hi, km_lo], axis=0)

    def sweep(c, n_own):
        own = n_own > 0
        n_blk = n_own if own else ch
        k0 = pl.multiple_of(c * cw, cw)
        heads = [slice(hh * hd, (hh + 1) * hd) for hh in range(n_grp)]
        scores = [_dot_nt(k_ref[0, pl.ds(k0, n_blk * blk), cols], q_ref[0, :, cols])
                  for cols in heads]
        m_news, l_news, probs = [], [], []
        for hh, s in enumerate(scores):
            parts = [s[t * blk:(t + 1) * blk] + bias_scr[hh, pl.ds(c * ch + t, 1), :]
                     for t in range(n_blk - 1 if own else n_blk)]
            if own:
                s_own = s[(n_blk - 1) * blk:]
                kpos = lax.broadcasted_iota(jnp.int32, s_own.shape, 0)
                qpos = lax.broadcasted_iota(jnp.int32, s_own.shape, 1)
                parts.append(jnp.where(kpos <= qpos, s_own, NEG_BIG))
            m_blk = parts[0]
            for t in range(1, n_blk):
                m_blk = jnp.maximum(m_blk, parts[t])
            m_new = jnp.max(m_blk, axis=0, keepdims=True)
            if not own:
                m_new = jnp.maximum(m_new, m_scr[hh])
            ps = [jnp.exp2(part - m_new) for part in parts]
            l_new = ps[0]
            for t in range(1, n_blk):
                l_new = l_new + ps[t]
            m_news.append(m_new)
            l_news.append(jnp.sum(l_new, axis=0, keepdims=True))
            probs.append(jnp.concatenate([p.astype(BF16) for p in ps], axis=0) if n_blk > 1
                         else ps[0].astype(BF16))
        pvs = [_dot(vt_scr[hh, c, :, 0:n_blk * blk], probs[hh]) for hh in range(n_grp)]
        for hh in range(n_grp):
            if own:
                l_scr[hh] = l_news[hh]
                acc_scr[hh] = pvs[hh]
            else:
                alpha = jnp.exp2(m_scr[hh] - m_news[hh])
                l_scr[hh] = alpha * l_scr[hh] + l_news[hh]
                acc_scr[hh] = alpha * acc_scr[hh] + pvs[hh]
            m_scr[hh] = m_news[hh]

    c_own = i // ch
    for hh in range(n_grp):
        q = q_ref[0, :, hh * hd:(hh + 1) * hd]
        gate2 = _dot_nt(kmean_scr[hh], q)
        gate = gate2[:nb] + gate2[nb:]
        blk_id = lax.broadcasted_iota(jnp.int32, gate.shape, 0)
        past = blk_id < i
        g = jnp.where(past, gate, -jnp.inf)
        rank = jnp.zeros(gate.shape, jnp.int32)
        for m in range(nb):
            gm = g[m:m + 1, :]
            beats = jnp.where(gm > g, 1, jnp.where(gm == g, jnp.where(blk_id > m, 1, 0), 0))
            rank = rank + beats
        bias_scr[hh] = jnp.where(past, jnp.where(rank < MOBA_TOPK, 0.0, NEG_BIG), NEG_BIG)

    for r in range(ch):
        @pl.when(i % ch == r)
        def _():
            sweep(c_own, r + 1)

    def body(c, carry):
        sweep(c, 0)
        return carry

    lax.fori_loop(0, c_own, body, 0)
    for hh in range(n_grp):
        o_ref[0, :, hh * hd:(hh + 1) * hd] = (acc_scr[hh] / l_scr[hh]).T.astype(BF16)


def _moba_attention(qkv, d, n_grp, ch):
    b, s, _ = qkv.shape
    h = d // HEAD_DIM
    blk = MOBA_BLOCK
    nb = s // blk
    gw = n_grp * HEAD_DIM
    ng = h // n_grp
    return pl.pallas_call(
        functools.partial(_moba_kernel, nb=nb, n_grp=n_grp, ch=ch),
        grid=(b, ng, nb),
        in_specs=[pl.BlockSpec((1, blk, gw), lambda bi, hi, i: (bi, i, hi)),
                  pl.BlockSpec((1, s, gw), lambda bi, hi, i: (bi, 0, ng + hi)),
                  pl.BlockSpec((1, s, gw), lambda bi, hi, i: (bi, 0, 2 * ng + hi))],
        out_specs=pl.BlockSpec((1, blk, gw), lambda bi, hi, i: (bi, i, hi)),
        out_shape=jax.ShapeDtypeStruct((b, s, d), BF16),
        scratch_shapes=[pltpu.VMEM((n_grp, 2 * nb, HEAD_DIM), BF16),
                        pltpu.VMEM((n_grp, nb // ch, HEAD_DIM, ch * blk), BF16),
                        pltpu.VMEM((n_grp, nb, blk), F32),
                        pltpu.VMEM((n_grp, 1, blk), F32),
                        pltpu.VMEM((n_grp, 1, blk), F32),
                        pltpu.VMEM((n_grp, HEAD_DIM, blk), F32)],
        compiler_params=_params("arbitrary", "arbitrary", "arbitrary"),
        name="moba_attention",
    )(qkv, qkv, qkv)


def _store_row_tiles(ref, val):
    rows, d = val.shape
    sub = d // LANES
    for c in range(sub):
        ref[pl.ds(c, rows, stride=sub), :] = val[:, c * LANES:(c + 1) * LANES]


def _load_row_tiles(ref, rows, dtype):
    sub = ref.shape[0] // rows
    return jnp.concatenate([ref[pl.ds(c, rows, stride=sub), :].astype(dtype) for c in range(sub)], axis=1)


def _route_top2(u, wr_ref):
    u_hi = u.astype(BF16)
    u_lo = (u - u_hi.astype(F32)).astype(BF16)
    both = _dot(u_hi, wr_ref[...])
    logits = both[:, :LANES] + both[:, LANES:] + _dot(u_lo, wr_ref[:, :LANES])
    lane = lax.broadcasted_iota(jnp.int32, logits.shape, 1)
    logits = jnp.where(lane < N_EXPERTS, logits, -jnp.inf)
    m1 = jnp.max(logits, axis=-1, keepdims=True)
    i1 = jnp.min(jnp.where(logits == m1, lane, LANES), axis=-1, keepdims=True)
    rest = jnp.where(lane == i1, -jnp.inf, logits)
    m2 = jnp.max(rest, axis=-1, keepdims=True)
    i2 = jnp.min(jnp.where(rest == m2, lane, LANES), axis=-1, keepdims=True)
    e2 = jnp.exp(m2 - m1)
    w1 = 1.0 / (1.0 + e2)
    w2 = e2 / (1.0 + e2)
    return jnp.where(lane == 0, i1.astype(F32),
                     jnp.where(lane == 1, i2.astype(F32),
                               jnp.where(lane == 2, w1, jnp.where(lane == 3, w2, 0.0))))


def _proj_ln_route_kernel(a_ref, w_ref, x_ref, g1p_ref, lng_ref, lnb_ref, sc_ref, sh_ref, wr_ref,
                          xo_ref, uo_ref, ro_ref, *, alpha):
    y = _dot(a_ref[0], w_ref[...])
    xn = _layer_norm(alpha * x_ref[0] + g1p_ref[0] * y, lng_ref[...], lnb_ref[...])
    xo_ref[0] = xn
    u = xn * sc_ref[0] + sh_ref[0]
    _store_row_tiles(uo_ref.at[0], u)
    ro_ref[0] = _route_top2(u, wr_ref)


def _proj_ln_route(a, w, x, g1p, lng, lnb, sc, sh, wr, alpha, tm):
    b, s, d = x.shape
    sub = d // LANES
    tile = pl.BlockSpec((1, tm, d), lambda i, j: (i, j, 0))
    per_b = pl.BlockSpec((1, 1, d), lambda i, j: (i, 0, 0))
    vec = pl.BlockSpec((1, d), lambda i, j: (0, 0))
    return pl.pallas_call(
        functools.partial(_proj_ln_route_kernel, alpha=alpha),
        grid=(b, s // tm),
        in_specs=[tile, pl.BlockSpec((d, d), lambda i, j: (0, 0)), tile, per_b, vec, vec, per_b, per_b,
                  pl.BlockSpec((d, 2 * LANES), lambda i, j: (0, 0))],
        out_specs=[tile, pl.BlockSpec((1, tm * sub, LANES), lambda i, j: (i, j, 0)),
                   pl.BlockSpec((1, tm, LANES), lambda i, j: (i, j, 0))],
        out_shape=[jax.ShapeDtypeStruct((b, s, d), F32), jax.ShapeDtypeStruct((b, s * sub, LANES), F32),
                   jax.ShapeDtypeStruct((b, s, LANES), F32)],
        compiler_params=_params("arbitrary", "arbitrary"),
        name="proj_ln_route",
    )(a, w, x, g1p, lng, lnb, sc, sh, wr)


def _proj_ffn_kernel(a_ref, wo_ref, x_ref, g1m_ref, lng_ref, lnb_ref, sc_ref, sh_ref, wg_ref, wu_ref, wd_ref,
                     g1f_ref, xo_ref, h_scr, *, alpha, tf):
    x1 = _layer_norm(alpha * x_ref[0] + g1m_ref[0] * _dot(a_ref[0], wo_ref[...]), lng_ref[0], lnb_ref[0])
    u = (x1 * sc_ref[0] + sh_ref[0]).astype(BF16)
    f = wg_ref.shape[1]
    for j in range(f // tf):
        cols = slice(j * tf, (j + 1) * tf)
        hj = _silu(_dot(u, wg_ref[:, cols])) * _dot(u, wu_ref[:, cols])
        h_scr[:, cols] = hj.astype(BF16)
    y = _dot(h_scr[...], wd_ref[...])
    xo_ref[0] = _layer_norm(alpha * x1 + g1f_ref[0] * y, lng_ref[1], lnb_ref[1])


def _proj_ffn_dense(a, wo, x, g1m, lng, lnb, sc, sh, wg, wu, wd, g1f, alpha, tm):
    b, s, d = x.shape
    f = wg.shape[1]
    tile = pl.BlockSpec((1, tm, d), lambda i, j: (i, j, 0))
    per_b = pl.BlockSpec((1, 1, d), lambda i, j: (i, 0, 0))
    ln_pair = pl.BlockSpec((2, 1, d), lambda i, j: (0, 0, 0))
    resident = dict(pipeline_mode=pl.Buffered(1))
    return pl.pallas_call(
        functools.partial(_proj_ffn_kernel, alpha=alpha, tf=MXU_N),
        grid=(b, s // tm),
        in_specs=[tile, pl.BlockSpec((d, d), lambda i, j: (0, 0), **resident), tile, per_b, ln_pair, ln_pair,
                  per_b, per_b,
                  pl.BlockSpec((d, f), lambda i, j: (0, 0), **resident),
                  pl.BlockSpec((d, f), lambda i, j: (0, 0), **resident),
                  pl.BlockSpec((f, d), lambda i, j: (0, 0), **resident),
                  per_b],
        out_specs=tile,
        out_shape=jax.ShapeDtypeStruct((b, s, d), F32),
        scratch_shapes=[pltpu.VMEM((tm, f), BF16)],
        compiler_params=_params("arbitrary", "arbitrary"),
        name="proj_ffn_dense",
    )(a, wo, x, g1m, lng, lnb, sc, sh, wg, wu, wd, g1f)


def _rec_in_kernel(x_ref, sc_ref, sh_ref, w_ref, lbl_ref, qf_ref, key_ref, v_ref, gs_ref, lf_ref,
                   *, layer_idx):
    u = (x_ref[0] * sc_ref[0] + sh_ref[0]).astype(BF16)
    d = x_ref.shape[2]
    lbl = lbl_ref[...]
    e = jnp.exp(lbl - jnp.max(lbl, axis=0, keepdims=True))
    sm = e / jnp.sum(e, axis=0, keepdims=True)
    lb = jnp.zeros((1, d), F32)
    for r in range(1, layer_idx + 1):
        lb = lb + sm[r:r + 1, :]
    for c in range(d // MXU_N):
        cols = slice(c * MXU_N, (c + 1) * MXU_N)
        q = _dot(u, w_ref[:, c * MXU_N:(c + 1) * MXU_N])
        qf_ref[0, :, cols] = _silu(q).astype(BF16)
        f = _dot(u, w_ref[:, d + c * MXU_N:d + (c + 1) * MXU_N])
        lbc = lb[:, cols]
        f_gate = lbc + (1.0 - lbc) * jax.nn.sigmoid(f)
        lf_ref[0, :, cols] = jnp.log(f_gate)
        key_ref[0, :, cols] = (1.0 - f_gate).astype(BF16)
        v = _dot(u, w_ref[:, 2 * d + c * MXU_N:2 * d + (c + 1) * MXU_N])
        v_ref[0, :, cols] = v.astype(BF16)
        g = _dot(u, w_ref[:, 3 * d + c * MXU_N:3 * d + (c + 1) * MXU_N])
        gs_ref[0, :, cols] = _silu(g).astype(BF16)


def _rec_in(x, sc, sh, w, lb_logits, layer_idx, ts):
    b, s, d = x.shape
    depth = lb_logits.shape[0]
    tile = pl.BlockSpec((1, ts, d), lambda i, j: (i, j, 0))
    per_b = pl.BlockSpec((1, 1, d), lambda i, j: (i, 0, 0))
    bf = jax.ShapeDtypeStruct((b, s, d), BF16)
    return pl.pallas_call(
        functools.partial(_rec_in_kernel, layer_idx=layer_idx),
        grid=(b, s // ts),
        in_specs=[tile, per_b, per_b,
                  pl.BlockSpec((d, 4 * d), lambda i, j: (0, 0)),
                  pl.BlockSpec((depth, d), lambda i, j: (0, 0))],
        out_specs=[tile] * 5,
        out_shape=[bf, bf, bf, bf, jax.ShapeDtypeStruct((b, s, d), F32)],
        compiler_params=_params("arbitrary", "arbitrary"),
        name="rec_in",
    )(x, sc, sh, w, lb_logits)


def _hgrn_kernel(qf_ref, key_ref, v_ref, lf_ref, gs_ref, nw_ref, o_ref, st_scr, *, n_heads):
    cs = HGRN_CHUNK
    dk = HGRN_EXPAND

    @pl.when(pl.program_id(1) == 0)
    def _():
        st_scr[...] = jnp.zeros(st_scr.shape, F32)

    row = lax.broadcasted_iota(jnp.int32, (cs, cs), 0)
    col = lax.broadcasted_iota(jnp.int32, (cs, cs), 1)
    causal = row >= col
    tri = jnp.where(causal, 1.0, 0.0).astype(BF16)
    n_chunks = qf_ref.shape[1] // cs

    def chunk(c, carry):
        rows = pl.ds(pl.multiple_of(c * cs, cs), cs)
        lf = lf_ref[0, rows, :]
        hi = lf.astype(BF16)
        r1 = lf - hi.astype(F32)
        mid = r1.astype(BF16)
        lo = (r1 - mid.astype(F32)).astype(BF16)
        g = _dot(tri, hi) + _dot(tri, mid) + _dot(tri, lo)
        g_last = g[cs - 1:cs, :]
        qf = qf_ref[0, rows, :].astype(F32)
        key = key_ref[0, rows, :].astype(F32)
        v = v_ref[0, rows, :]
        q_dec = (qf * jnp.exp(g)).astype(BF16)
        k_dec = (key * jnp.exp(-g)).astype(BF16)
        k_state = (key * jnp.exp(g_last - g)).astype(BF16)
        decay = jnp.exp(g_last)
        heads = [slice(h * dk, (h + 1) * dk) for h in range(n_heads)]
        a = [jnp.where(causal, _dot_nt(q_dec[:, c], k_dec[:, c]), 0.0).astype(BF16) for c in heads]
        st = [st_scr[h] for h in range(n_heads)]
        o = [_dot(a[h], v[:, c]) + _dot_nt(q_dec[:, c], st[h].astype(BF16))
             for h, c in enumerate(heads)]
        for h, c in enumerate(heads):
            v_t = v[:, c].astype(F32).T.astype(BF16)
            st_scr[h] = st[h] * decay[:, c] + _dot(v_t, k_state[:, c])
        on = jnp.concatenate(
            [oh * lax.rsqrt(jnp.mean(oh * oh, axis=-1, keepdims=True) + RMS_EPS) for oh in o], axis=1)
        o_ref[0, rows, :] = (on * nw_ref[...] * gs_ref[0, rows, :].astype(F32)).astype(BF16)
        return carry

    lax.fori_loop(0, n_chunks, chunk, 0, unroll=4)


def _hgrn_recurrence(qf, key, v, lf, gs, norm_w, ts):
    b, s, d = qf.shape
    h = d // HGRN_EXPAND
    tile = pl.BlockSpec((1, ts, d), lambda i, j: (i, j, 0))
    return pl.pallas_call(
        functools.partial(_hgrn_kernel, n_heads=h),
        grid=(b, s // ts),
        in_specs=[tile, tile, tile, tile, tile, pl.BlockSpec((1, d), lambda i, j: (0, 0))],
        out_specs=tile,
        out_shape=jax.ShapeDtypeStruct((b, s, d), BF16),
        scratch_shapes=[pltpu.VMEM((h, HGRN_EXPAND, HGRN_EXPAND), F32)],
        compiler_params=_params("arbitrary", "arbitrary"),
        name="hgrn_recurrence",
    )(qf, key, v, lf, gs, norm_w)


GATHER_UNROLL = 8


ROW_TILE = 8


def _start_row(idx_at, src_hbm, dst, sem, r, priority=0):
    first = pl.multiple_of(idx_at(r), ROW_TILE)
    pltpu.make_async_copy(src_hbm.at[pl.ds(first, ROW_TILE), :],
                          dst.at[pl.ds(r * ROW_TILE, ROW_TILE), :], sem).start(priority=priority)


def _start_row_gather(idx_at, src_hbm, dst, sem, n_rows, both_queues=False):
    ways = 2 if both_queues else 1

    def group(g, carry):
        for k in range(ways):
            _start_row(idx_at, src_hbm, dst, sem, g * ways + k, priority=k)
        return carry

    lax.fori_loop(0, n_rows // ways, group, 0, unroll=GATHER_UNROLL // ways)


def _wait_rows(buf, sem):
    pltpu.make_async_copy(buf, buf, sem).wait()


MOE_AHEAD = 2
MOE_XBUFS = MOE_AHEAD + 1


def _moe_ffn_kernel(te_ref, na_ref, src_t0_ref, src_t1_ref, src_ahead_ref, u_hbm, wg_ref, wu_ref, wd_ref,
                    o_ref, xbuf, xb_scr, h_scr, acc_scr, sems, *, tm, nf, n_tiles):
    i = pl.program_id(0)
    j = pl.program_id(1)
    rows_per_step = tm // nf
    n_active = na_ref[0]
    active = i < n_active
    slot = i % MOE_XBUFS
    ahead = (i + MOE_AHEAD) % MOE_XBUFS

    assert MOE_AHEAD == 2

    @pl.when((j == 0) & (i == 0))
    def _():
        _start_row_gather(lambda r: src_t0_ref[0, 0, r], u_hbm, xbuf.at[0], sems.at[0], tm)
        _start_row_gather(lambda r: src_t1_ref[0, 0, r], u_hbm, xbuf.at[1], sems.at[1], tm)

    @pl.when((j == 0) & (i <= n_active + 1))
    def _():
        _wait_rows(xbuf.at[slot], sems.at[slot])

    @pl.when(active & (j == 0))
    def _():
        xb_scr[...] = _load_row_tiles(xbuf.at[slot], tm, BF16)

    @pl.when(active)
    def _():
        for k in range(rows_per_step):
            _start_row(lambda r: src_ahead_ref[0, 0, r], u_hbm, xbuf.at[ahead], sems.at[ahead],
                       j * rows_per_step + k, priority=k % 2)
        x = xb_scr[...]
        for c in range(wg_ref.shape[2] // MXU_N):
            cols = slice(c * MXU_N, (c + 1) * MXU_N)
            hc = _silu(_dot(x, wg_ref[0, :, cols])) * _dot(x, wu_ref[0, :, cols])
            h_scr[:, cols] = hc.astype(BF16)
        y = _dot(h_scr[...], wd_ref[0])
        if nf == 1:
            _store_row_tiles(o_ref, y)
        else:
            @pl.when(j == 0)
            def _():
                acc_scr[...] = y

            if nf > 2:
                @pl.when((j > 0) & (j < nf - 1))
                def _():
                    acc_scr[...] += y

            @pl.when(j == nf - 1)
            def _():
                _store_row_tiles(o_ref, acc_scr[...] + y)

    @pl.when(jnp.logical_not(active) & (j == nf - 1) & (i < n_tiles))
    def _():
        o_ref[...] = jnp.zeros(o_ref.shape, F32)


def _moe_ffn(u_tiles, src, wg, wu, wd, tile_expert, n_active, tm, tf):
    d = wd.shape[2]
    sub = d // LANES
    assert sub == ROW_TILE
    n_tiles = src.shape[0] // tm
    assert tile_expert.shape[0] == n_tiles + MOE_AHEAD
    nf = wg.shape[2] // tf

    def frozen(j, i, na):
        return jnp.where(i < na[0], j, nf - 1)

    def idx_block(tile_of):
        return pl.BlockSpec((1, 1, tm), lambda i, j, te, na: (jnp.minimum(tile_of(i), n_tiles - 1), 0, 0),
                            memory_space=pltpu.SMEM)

    grid_spec = pltpu.PrefetchScalarGridSpec(
        num_scalar_prefetch=2,
        grid=(n_tiles + MOE_AHEAD, nf),
        in_specs=[idx_block(lambda i: 0), idx_block(lambda i: 1), idx_block(lambda i: i + MOE_AHEAD),
                  pl.BlockSpec(memory_space=pl.ANY),
                  pl.BlockSpec((1, d, tf), lambda i, j, te, na: (te[i], 0, frozen(j, i, na))),
                  pl.BlockSpec((1, d, tf), lambda i, j, te, na: (te[i], 0, frozen(j, i, na))),
                  pl.BlockSpec((1, tf, d), lambda i, j, te, na: (te[i], frozen(j, i, na), 0))],
        out_specs=pl.BlockSpec((tm * sub, LANES), lambda i, j, te, na: (jnp.minimum(i, n_tiles - 1), 0)),
        scratch_shapes=[pltpu.VMEM((MOE_XBUFS, tm * sub, LANES), F32),
                        pltpu.VMEM((tm, d), BF16),
                        pltpu.VMEM((tm, tf), BF16),
                        pltpu.VMEM((tm, d), F32),
                        pltpu.SemaphoreType.DMA((MOE_XBUFS,))],
    )
    src3 = src.reshape(n_tiles, 1, tm)
    return pl.pallas_call(
        functools.partial(_moe_ffn_kernel, tm=tm, nf=nf, n_tiles=n_tiles),
        grid_spec=grid_spec,
        out_shape=jax.ShapeDtypeStruct((n_tiles * tm * sub, LANES), F32),
        compiler_params=_params("arbitrary", "arbitrary"),
        name="moe_ffn",
    )(tile_expert, n_active, src3, src3, src3, u_tiles, wg, wu, wd)


def _combine_ln_kernel(sa_cur, sb_cur, sa_nxt, sb_nxt, ys_hbm, r_ref, x_ref, g1p_ref, lng_ref, lnb_ref,
                       xo_ref, ybuf, sems, *, alpha, tm):
    i = pl.program_id(0)
    slot = i % 2

    def start(sa, sb, s):
        _start_row_gather(lambda r: sa[r], ys_hbm, ybuf.at[s, 0], sems.at[s], tm, both_queues=True)
        _start_row_gather(lambda r: sb[r], ys_hbm, ybuf.at[s, 1], sems.at[s], tm, both_queues=True)

    @pl.when(i == 0)
    def _():
        start(sa_cur, sb_cur, 0)

    @pl.when(i + 1 < pl.num_programs(0))
    def _():
        start(sa_nxt, sb_nxt, 1 - slot)

    _wait_rows(ybuf.at[slot], sems.at[slot])
    r = r_ref[...]
    y = (r[:, 2:3] * _load_row_tiles(ybuf.at[slot, 0], tm, F32)
         + r[:, 3:4] * _load_row_tiles(ybuf.at[slot, 1], tm, F32))
    xo_ref[...] = _layer_norm(alpha * x_ref[...] + g1p_ref[0] * y, lng_ref[...], lnb_ref[...])


def _combine_ln(ys_tiles, slot_a, slot_b, route, x, g1p, lng, lnb, alpha, tm):
    b, s, d = x.shape
    n = b * s
    sub = d // LANES
    n_tiles = n // tm
    per_seq = s // tm
    cur = pl.BlockSpec((tm,), lambda i: (i,), memory_space=pltpu.SMEM)
    nxt = pl.BlockSpec((tm,), lambda i: (jnp.minimum(i + 1, n_tiles - 1),), memory_space=pltpu.SMEM)
    tile = pl.BlockSpec((tm, d), lambda i: (i, 0))
    vec = pl.BlockSpec((1, d), lambda i: (0, 0))
    out = pl.pallas_call(
        functools.partial(_combine_ln_kernel, alpha=alpha, tm=tm),
        grid=(n_tiles,),
        in_specs=[cur, cur, nxt, nxt, pl.BlockSpec(memory_space=pl.ANY),
                  pl.BlockSpec((tm, LANES), lambda i: (i, 0)), tile,
                  pl.BlockSpec((1, 1, d), lambda i: (i // per_seq, 0, 0)), vec, vec],
        out_specs=tile,
        out_shape=jax.ShapeDtypeStruct((n, d), F32),
        scratch_shapes=[pltpu.VMEM((2, 2, tm * sub, LANES), F32), pltpu.SemaphoreType.DMA((2,))],
        compiler_params=_params("arbitrary"),
        name="combine_ln",
    )(slot_a, slot_b, slot_a, slot_b, ys_tiles, route.reshape(n, LANES), x.reshape(n, d), g1p, lng, lnb)
    return out.reshape(b, s, d)


def _moe_plan(route, tm):
    n = route.shape[0]
    e = N_EXPERTS
    n_tiles = (TOP_K * n + e * (tm - 1)) // tm
    flat_e = route[:, :TOP_K].astype(jnp.int32).reshape(-1)
    onehot = (flat_e[:, None] == jnp.arange(e, dtype=jnp.int32)[None, :]).astype(jnp.int32)
    incl = jnp.cumsum(onehot, axis=0)
    counts = incl[-1]
    padded = (counts + tm - 1) // tm * tm
    ends = jnp.cumsum(padded)
    offs = ends - padded
    slot = jnp.sum(onehot * (offs[None, :] + incl - 1), axis=1)
    n_active = ends[-1] // tm
    tile_id = jnp.minimum(jnp.arange(n_tiles + MOE_AHEAD, dtype=jnp.int32), n_active - 1)
    tile_expert = jnp.minimum(jnp.sum((tile_id[:, None] * tm >= ends[None, :]).astype(jnp.int32), axis=1), e - 1)
    by_expert = jnp.argsort(flat_e, stable=True).astype(jnp.int32) // TOP_K
    by_expert = jnp.pad(by_expert, (0, n_tiles * tm - TOP_K * n))
    row_expert = jnp.repeat(tile_expert[:n_tiles], tm)
    shift = offs - (jnp.cumsum(counts) - counts)
    src = jnp.zeros((n_tiles * tm,), jnp.int32)
    for k in range(e):
        src = jnp.where(row_expert == k, jnp.roll(by_expert, shift[k]), src)
    return ((slot * ROW_TILE).reshape(n, TOP_K), src * ROW_TILE, tile_expert.astype(jnp.int32),
            n_active.reshape(1).astype(jnp.int32))


def _moe_swiglu_ln(u_tiles, route, wg, wu, wd, x, g1p, lng, lnb, alpha, tiles):
    assert TOP_K == 2
    b, s, d = x.shape
    n = b * s
    nf = tiles.moe_slabs
    tf = wg.shape[2] // nf
    assert tiles.moe_rows % nf == 0
    slot, src, tile_expert, n_active = _moe_plan(route.reshape(n, LANES), tiles.moe_rows)
    ys = _moe_ffn(u_tiles.reshape(n * (d // LANES), LANES), src, wg, wu, wd, tile_expert, n_active,
                  tiles.moe_rows, tf)
    return _combine_ln(ys, slot[:, 0], slot[:, 1], route, x, g1p, lng, lnb, alpha, tiles.combine_rows)


class _Tiles(NamedTuple):
    token_rows: int
    combine_rows: int
    attn_heads: int
    attn_chunk: int
    moe_rows: int
    moe_slabs: int


def _tiles(s, d, d_ff_expert):
    return _Tiles(token_rows=1024,
                  combine_rows=512,
                  attn_heads=math.gcd(d // HEAD_DIM, 8),
                  attn_chunk=math.gcd(s // MOBA_BLOCK, 4),
                  moe_rows=896,
                  moe_slabs=2 if d_ff_expert % (2 * MXU_N) == 0 else 1)


def kernel(x, c, positions, ada_w, ada_b, ln_g, ln_b, attn_w_in, attn_w_out, rec_w_in, rec_lb_logits,
           rec_norm_w, rec_w_out, ffn_w_gate, ffn_w_up, ffn_w_down, router_w, moe_w_gate, moe_w_up,
           moe_w_down):
    b, s, d = x.shape
    depth = ada_w.shape[0]
    alpha = (2.0 * depth) ** 0.25
    tiles = _tiles(s, d, moe_w_gate.shape[-1])
    ts = tiles.token_rows

    mods = _ada_mods(c, ada_w, ada_b).reshape(depth, b, 6, 1, d)
    one_plus = lambda t: 1.0 + t
    pos = positions.reshape(b, s, 1)

    for i in range(depth):
        shift_m, scale_m, gate_m, shift_f, scale_f, gate_f = (mods[i, :, r] for r in range(6))
        j = i // 2
        lng = ln_g[i].reshape(2, 1, d)
        lnb = ln_b[i].reshape(2, 1, d)
        if i % 2 == 0:
            qkv = _qkv_proj(x, one_plus(scale_m), shift_m, pos, attn_w_in[j].astype(BF16), ts)
            o = _moba_attention(qkv, d, n_grp=tiles.attn_heads, ch=tiles.attn_chunk)
            x = _proj_ffn_dense(o, attn_w_out[j].astype(BF16), x, one_plus(gate_m), lng, lnb,
                                one_plus(scale_f), shift_f, ffn_w_gate[j].astype(BF16),
                                ffn_w_up[j].astype(BF16), ffn_w_down[j].astype(BF16), one_plus(gate_f),
                                alpha, ts)
        else:
            qf, key, v, gs, lf = _rec_in(x, one_plus(scale_m), shift_m, rec_w_in[j].astype(BF16),
                                         rec_lb_logits, i, ts)
            o = _hgrn_recurrence(qf, key, v, lf, gs, rec_norm_w[j].reshape(1, d), ts)
            w_hi = router_w[j].astype(BF16)
            w_lo = (router_w[j] - w_hi.astype(F32)).astype(BF16)
            wr = (jnp.zeros((d, 2 * LANES), BF16).at[:, :N_EXPERTS].set(w_hi)
                  .at[:, LANES:LANES + N_EXPERTS].set(w_lo))
            x, u, route = _proj_ln_route(o, rec_w_out[j].astype(BF16), x, one_plus(gate_m), lng[0], lnb[0],
                                         one_plus(scale_f), shift_f, wr, alpha, ts)
            x = _moe_swiglu_ln(u, route, moe_w_gate[j].astype(BF16), moe_w_up[j].astype(BF16),
                               moe_w_down[j].astype(BF16), x, one_plus(gate_f), lng[1], lnb[1], alpha, tiles)
    return x
```

```python
import functools
import math
from typing import NamedTuple

import jax
import jax.numpy as jnp
from jax import lax
from jax.experimental import pallas as pl
from jax.experimental.pallas import tpu as pltpu

HEAD_DIM = 128
ROPE_DIM = HEAD_DIM // 4
ROPE_THETA = 500000.0
MOBA_BLOCK = 256
MOBA_TOPK = 3
HGRN_EXPAND = 128
HGRN_CHUNK = 64
N_EXPERTS = 8
TOP_K = 2
LN_EPS = 1e-5
RMS_EPS = 1e-6

LANES = 128
MXU_N = 256
VMEM_LIMIT = 56 * 1024 * 1024
NEG_BIG = -1e30

F32 = jnp.float32
BF16 = jnp.bfloat16
HIGHEST = lax.Precision.HIGHEST


def _params(*sem):
    return pltpu.CompilerParams(dimension_semantics=sem, vmem_limit_bytes=VMEM_LIMIT)


def _dot(a, b):
    return jnp.dot(a, b, preferred_element_type=F32)


def _dot_nt(a, b, precision=None):
    return lax.dot_general(a, b, (((1,), (1,)), ((), ())), precision=precision,
                           preferred_element_type=F32)


def _silu(x):
    return x * jax.nn.sigmoid(x)


def _layer_norm(z, g, b):
    mu = jnp.mean(z, axis=-1, keepdims=True)
    d = z - mu
    var = jnp.mean(d * d, axis=-1, keepdims=True)
    return d * lax.rsqrt(var + LN_EPS) * g + b


def _ada_kernel(c_ref, w_ref, b_ref, o_ref):
    a = _silu(c_ref[...])
    o_ref[0] = jnp.dot(a, w_ref[0], precision=HIGHEST, preferred_element_type=F32) + b_ref[0]


def _ada_mods(c, ada_w, ada_b):
    depth, d, m = ada_w.shape
    b = c.shape[0]
    tn = m // 4
    return pl.pallas_call(
        _ada_kernel,
        grid=(depth, m // tn),
        in_specs=[pl.BlockSpec((b, d), lambda l, j: (0, 0)),
                  pl.BlockSpec((1, d, tn), lambda l, j: (l, 0, j)),
                  pl.BlockSpec((1, 1, tn), lambda l, j: (l, 0, j))],
        out_specs=pl.BlockSpec((1, b, tn), lambda l, j: (l, 0, j)),
        out_shape=jax.ShapeDtypeStruct((depth, b, m), F32),
        compiler_params=_params("arbitrary", "arbitrary"),
        name="ada_mods",
    )(c, ada_w, ada_b.reshape(depth, 1, m))


def _qkv_kernel(x_ref, sc_ref, sh_ref, pos_ref, w_ref, o_ref, trig_scr, *, n_rot):
    u = (x_ref[0] * sc_ref[0] + sh_ref[0]).astype(BF16)
    half = ROPE_DIM // 2
    pack = LANES // ROPE_DIM
    rows = x_ref.shape[1] // pack
    lane = lax.broadcasted_iota(jnp.int32, (1, LANES), 1)
    inv = jnp.exp(-math.log(ROPE_THETA) * (lane % half).astype(F32) * (2.0 / ROPE_DIM))
    pos = jnp.zeros((rows, LANES), F32)
    for m in range(pack):
        pos = jnp.where(lane // ROPE_DIM == m, pos_ref[0, pl.ds(m, rows, stride=pack), :].astype(F32), pos)
    ang = pos * inv
    for t, trig in enumerate((jnp.cos(ang), jnp.sin(ang))):
        for m in range(pack):
            moved = trig if m == 0 else pltpu.roll(trig, LANES - ROPE_DIM * m, 1)
            trig_scr[t, pl.ds(m, rows, stride=pack), :] = moved
    cos_t = jnp.where(lane < ROPE_DIM, trig_scr[0], 1.0)
    sin_t = jnp.where(lane < ROPE_DIM, trig_scr[1], 0.0)
    sin_lo = jnp.where(lane < half, -sin_t, 0.0)
    sin_hi = jnp.where((lane >= half) & (lane < ROPE_DIM), sin_t, 0.0)
    n_cols = w_ref.shape[1]
    for c in range(n_cols // MXU_N):
        y = _dot(u, w_ref[:, c * MXU_N:(c + 1) * MXU_N])
        for s in range(MXU_N // HEAD_DIM):
            col = c * MXU_N + s * HEAD_DIM
            t = y[:, s * HEAD_DIM:(s + 1) * HEAD_DIM]
            if col < n_rot:
                t = (t * cos_t + pltpu.roll(t, HEAD_DIM - half, 1) * sin_lo
                     + pltpu.roll(t, half, 1) * sin_hi)
            if col < n_rot // 2:
                t = t * (HEAD_DIM ** -0.5 * math.log2(math.e))
            o_ref[0, :, col:col + HEAD_DIM] = t.astype(BF16)


def _qkv_proj(x, sc, sh, pos, w, ts):
    b, s, d = x.shape
    n = w.shape[1]
    return pl.pallas_call(
        functools.partial(_qkv_kernel, n_rot=2 * d),
        grid=(b, s // ts),
        in_specs=[pl.BlockSpec((1, ts, d), lambda i, j: (i, j, 0)),
                  pl.BlockSpec((1, 1, d), lambda i, j: (i, 0, 0)),
                  pl.BlockSpec((1, 1, d), lambda i, j: (i, 0, 0)),
                  pl.BlockSpec((1, ts, 1), lambda i, j: (i, j, 0)),
                  pl.BlockSpec((d, n), lambda i, j: (0, 0))],
        out_specs=pl.BlockSpec((1, ts, n), lambda i, j: (i, j, 0)),
        out_shape=jax.ShapeDtypeStruct((b, s, n), BF16),
        scratch_shapes=[pltpu.VMEM((2, ts, LANES), F32)],
        compiler_params=_params("arbitrary", "arbitrary"),
        name="qkv_proj",
    )(x, sc, sh, pos, w)


def _moba_kernel(q_ref, k_ref, v_ref, o_ref, kmean_scr, vt_scr, bias_scr, m_scr, l_scr, acc_scr,
                 *, nb, n_grp, ch):
    blk = MOBA_BLOCK
    hd = HEAD_DIM
    cw = ch * blk
    i = pl.program_id(2)

    @pl.when(i == 0)
    def _():
        for hh in range(n_grp):
            means = []
            for n in range(nb):
                kb = k_ref[0, n * blk:(n + 1) * blk, hh * hd:(hh + 1) * hd].astype(F32)
                means.append(jnp.mean(kb, axis=0, keepdims=True))
                vb = v_ref[0, n * blk:(n + 1) * blk, hh * hd:(hh + 1) * hd]
                vt_scr[hh, n // ch, :, (n % ch) * blk:(n % ch + 1) * blk] = vb.astype(F32).T.astype(BF16)
            km = jnp.concatenate(means, axis=0)
            km_hi = km.astype(BF16)
            km_lo = (km - km_hi.astype(F32)).astype(BF16)
            kmean_scr[hh] = jnp.concatenate([km_hi, km_lo], axis=0)

    def sweep(c, n_own):
        own = n_own > 0
        n_blk = n_own if own else ch
        k0 = pl.multiple_of(c * cw, cw)
        heads = [slice(hh * hd, (hh + 1) * hd) for hh in range(n_grp)]
        scores = [_dot_nt(k_ref[0, pl.ds(k0, n_blk * blk), cols], q_ref[0, :, cols])
                  for cols in heads]
        m_news, l_news, probs = [], [], []
        for hh, s in enumerate(scores):
            parts = [s[t * blk:(t + 1) * blk] + bias_scr[hh, pl.ds(c * ch + t, 1), :]
                     for t in range(n_blk - 1 if own else n_blk)]
            if own:
                s_own = s[(n_blk - 1) * blk:]
                kpos = lax.broadcasted_iota(jnp.int32, s_own.shape, 0)
                qpos = lax.broadcasted_iota(jnp.int32, s_own.shape, 1)
                parts.append(jnp.where(kpos <= qpos, s_own, NEG_BIG))
            m_blk = parts[0]
            for t in range(1, n_blk):
                m_blk = jnp.maximum(m_blk, parts[t])
            m_new = jnp.max(m_blk, axis=0, keepdims=True)
            if not own:
                m_new = jnp.maximum(m_new, m_scr[hh])
            ps = [jnp.exp2(part - m_new) for part in parts]
            l_new = ps[0]
            for t in range(1, n_blk):
                l_new = l_new + ps[t]
            m_news.append(m_new)
            l_news.append(jnp.sum(l_new, axis=0, keepdims=True))
            probs.append(jnp.concatenate([p.astype(BF16) for p in ps], axis=0) if n_blk > 1
                         else ps[0].astype(BF16))
        pvs = [_dot(vt_scr[hh, c, :, 0:n_blk * blk], probs[hh]) for hh in range(n_grp)]
        for hh in range(n_grp):
            if own:
                l_scr[hh] = l_news[hh]
                acc_scr[hh] = pvs[hh]
            else:
                alpha = jnp.exp2(m_scr[hh] - m_news[hh])
                l_scr[hh] = alpha * l_scr[hh] + l_news[hh]
                acc_scr[hh] = alpha * acc_scr[hh] + pvs[hh]
            m_scr[hh] = m_news[hh]

    c_own = i // ch
    for hh in range(n_grp):
        q = q_ref[0, :, hh * hd:(hh + 1) * hd]
        gate2 = _dot_nt(kmean_scr[hh], q)
        gate = gate2[:nb] + gate2[nb:]
        blk_id = lax.broadcasted_iota(jnp.int32, gate.shape, 0)
        g = jnp.where(blk_id < i, gate, -jnp.inf)
        bias = jnp.full(gate.shape, NEG_BIG, F32)
        for _ in range(MOBA_TOPK):
            top = jnp.max(g, axis=0, keepdims=True)
            first = jnp.min(jnp.where(g == top, blk_id, nb), axis=0, keepdims=True)
            pick = blk_id == jnp.where(top > -jnp.inf, first, nb)
            bias = jnp.where(pick, 0.0, bias)
            g = jnp.where(pick, -jnp.inf, g)
        bias_scr[hh] = bias

    for r in range(ch):
        @pl.when(i % ch == r)
        def _():
            sweep(c_own, r + 1)

    def body(c, carry):
        sweep(c, 0)
        return carry

    lax.fori_loop(0, c_own, body, 0)
    for hh in range(n_grp):
        o_ref[0, :, hh * hd:(hh + 1) * hd] = (acc_scr[hh] / l_scr[hh]).T.astype(BF16)


def _moba_attention(qkv, d, n_grp, ch):
    b, s, _ = qkv.shape
    h = d // HEAD_DIM
    blk = MOBA_BLOCK
    nb = s // blk
    gw = n_grp * HEAD_DIM
    ng = h // n_grp
    return pl.pallas_call(
        functools.partial(_moba_kernel, nb=nb, n_grp=n_grp, ch=ch),
        grid=(b, ng, nb),
        in_specs=[pl.BlockSpec((1, blk, gw), lambda bi, hi, i: (bi, i, hi)),
                  pl.BlockSpec((1, s, gw), lambda bi, hi, i: (bi, 0, ng + hi)),
                  pl.BlockSpec((1, s, gw), lambda bi, hi, i: (bi, 0, 2 * ng + hi))],
        out_specs=pl.BlockSpec((1, blk, gw), lambda bi, hi, i: (bi, i, hi)),
        out_shape=jax.ShapeDtypeStruct((b, s, d), BF16),
        scratch_shapes=[pltpu.VMEM((n_grp, 2 * nb, HEAD_DIM), BF16),
                        pltpu.VMEM((n_grp, nb // ch, HEAD_DIM, ch * blk), BF16),
                        pltpu.VMEM((n_grp, nb, blk), F32),
                        pltpu.VMEM((n_grp, 1, blk), F32),
                        pltpu.VMEM((n_grp, 1, blk), F32),
                        pltpu.VMEM((n_grp, HEAD_DIM, blk), F32)],
        compiler_params=_params("arbitrary", "arbitrary", "arbitrary"),
        name="moba_attention",
    )(qkv, qkv, qkv)


def _store_row_tiles(ref, val):
    rows, d = val.shape
    sub = d // LANES
    for c in range(sub):
        ref[pl.ds(c, rows, stride=sub), :] = val[:, c * LANES:(c + 1) * LANES]


def _load_row_tiles(ref, rows, dtype):
    sub = ref.shape[0] // rows
    return jnp.concatenate([ref[pl.ds(c, rows, stride=sub), :].astype(dtype) for c in range(sub)], axis=1)


def _route_top2(u, wr_ref):
    u_hi = u.astype(BF16)
    u_lo = (u - u_hi.astype(F32)).astype(BF16)
    both = _dot(u_hi, wr_ref[...])
    logits = both[:, :LANES] + both[:, LANES:] + _dot(u_lo, wr_ref[:, :LANES])
    lane = lax.broadcasted_iota(jnp.int32, logits.shape, 1)
    logits = jnp.where(lane < N_EXPERTS, logits, -jnp.inf)
    m1 = jnp.max(logits, axis=-1, keepdims=True)
    i1 = jnp.min(jnp.where(logits == m1, lane, LANES), axis=-1, keepdims=True)
    rest = jnp.where(lane == i1, -jnp.inf, logits)
    m2 = jnp.max(rest, axis=-1, keepdims=True)
    i2 = jnp.min(jnp.where(rest == m2, lane, LANES), axis=-1, keepdims=True)
    e2 = jnp.exp(m2 - m1)
    w1 = 1.0 / (1.0 + e2)
    w2 = e2 / (1.0 + e2)
    return jnp.where(lane == 0, i1.astype(F32),
                     jnp.where(lane == 1, i2.astype(F32),
                               jnp.where(lane == 2, w1, jnp.where(lane == 3, w2, 0.0))))


def _proj_ln_route_kernel(a_ref, w_ref, x_ref, g1p_ref, lng_ref, lnb_ref, sc_ref, sh_ref, wr_ref,
                          xo_ref, uo_ref, ro_ref, *, alpha):
    y = _dot(a_ref[0], w_ref[...])
    xn = _layer_norm(alpha * x_ref[0] + g1p_ref[0] * y, lng_ref[...], lnb_ref[...])
    xo_ref[0] = xn
    u = xn * sc_ref[0] + sh_ref[0]
    _store_row_tiles(uo_ref.at[0], u)
    ro_ref[0] = _route_top2(u, wr_ref)


def _proj_ln_route(a, w, x, g1p, lng, lnb, sc, sh, wr, alpha, tm):
    b, s, d = x.shape
    sub = d // LANES
    tile = pl.BlockSpec((1, tm, d), lambda i, j: (i, j, 0))
    per_b = pl.BlockSpec((1, 1, d), lambda i, j: (i, 0, 0))
    vec = pl.BlockSpec((1, d), lambda i, j: (0, 0))
    return pl.pallas_call(
        functools.partial(_proj_ln_route_kernel, alpha=alpha),
        grid=(b, s // tm),
        in_specs=[tile, pl.BlockSpec((d, d), lambda i, j: (0, 0)), tile, per_b, vec, vec, per_b, per_b,
                  pl.BlockSpec((d, 2 * LANES), lambda i, j: (0, 0))],
        out_specs=[tile, pl.BlockSpec((1, tm * sub, LANES), lambda i, j: (i, j, 0)),
                   pl.BlockSpec((1, tm, LANES), lambda i, j: (i, j, 0))],
        out_shape=[jax.ShapeDtypeStruct((b, s, d), F32), jax.ShapeDtypeStruct((b, s * sub, LANES), F32),
                   jax.ShapeDtypeStruct((b, s, LANES), F32)],
        compiler_params=_params("arbitrary", "arbitrary"),
        name="proj_ln_route",
    )(a, w, x, g1p, lng, lnb, sc, sh, wr)


def _proj_ffn_kernel(a_ref, wo_ref, x_ref, g1m_ref, lng_ref, lnb_ref, sc_ref, sh_ref, wg_ref, wu_ref, wd_ref,
                     g1f_ref, xo_ref, h_scr, *, alpha, tf):
    x1 = _layer_norm(alpha * x_ref[0] + g1m_ref[0] * _dot(a_ref[0], wo_ref[...]), lng_ref[0], lnb_ref[0])
    u = (x1 * sc_ref[0] + sh_ref[0]).astype(BF16)
    f = wg_ref.shape[1]
    for j in range(f // tf):
        cols = slice(j * tf, (j + 1) * tf)
        hj = _silu(_dot(u, wg_ref[:, cols])) * _dot(u, wu_ref[:, cols])
        h_scr[:, cols] = hj.astype(BF16)
    y = _dot(h_scr[...], wd_ref[...])
    xo_ref[0] = _layer_norm(alpha * x1 + g1f_ref[0] * y, lng_ref[1], lnb_ref[1])


def _proj_ffn_dense(a, wo, x, g1m, lng, lnb, sc, sh, wg, wu, wd, g1f, alpha, tm):
    b, s, d = x.shape
    f = wg.shape[1]
    tile = pl.BlockSpec((1, tm, d), lambda i, j: (i, j, 0))
    per_b = pl.BlockSpec((1, 1, d), lambda i, j: (i, 0, 0))
    ln_pair = pl.BlockSpec((2, 1, d), lambda i, j: (0, 0, 0))
    resident = dict(pipeline_mode=pl.Buffered(1))
    return pl.pallas_call(
        functools.partial(_proj_ffn_kernel, alpha=alpha, tf=MXU_N),
        grid=(b, s // tm),
        in_specs=[tile, pl.BlockSpec((d, d), lambda i, j: (0, 0), **resident), tile, per_b, ln_pair, ln_pair,
                  per_b, per_b,
                  pl.BlockSpec((d, f), lambda i, j: (0, 0), **resident),
                  pl.BlockSpec((d, f), lambda i, j: (0, 0), **resident),
                  pl.BlockSpec((f, d), lambda i, j: (0, 0), **resident),
                  per_b],
        out_specs=tile,
        out_shape=jax.ShapeDtypeStruct((b, s, d), F32),
        scratch_shapes=[pltpu.VMEM((tm, f), BF16)],
        compiler_params=_params("arbitrary", "arbitrary"),
        name="proj_ffn_dense",
    )(a, wo, x, g1m, lng, lnb, sc, sh, wg, wu, wd, g1f)


def _rec_in_kernel(x_ref, sc_ref, sh_ref, w_ref, lbl_ref, qf_ref, key_ref, v_ref, gs_ref, lf_ref,
                   *, layer_idx):
    u = (x_ref[0] * sc_ref[0] + sh_ref[0]).astype(BF16)
    d = x_ref.shape[2]
    lbl = lbl_ref[...]
    e = jnp.exp(lbl - jnp.max(lbl, axis=0, keepdims=True))
    sm = e / jnp.sum(e, axis=0, keepdims=True)
    lb = jnp.zeros((1, d), F32)
    for r in range(1, layer_idx + 1):
        lb = lb + sm[r:r + 1, :]
    for c in range(d // MXU_N):
        cols = slice(c * MXU_N, (c + 1) * MXU_N)
        q = _dot(u, w_ref[:, c * MXU_N:(c + 1) * MXU_N])
        qf_ref[0, :, cols] = _silu(q).astype(BF16)
        f = _dot(u, w_ref[:, d + c * MXU_N:d + (c + 1) * MXU_N])
        lbc = lb[:, cols]
        f_gate = lbc + (1.0 - lbc) * jax.nn.sigmoid(f)
        lf_ref[0, :, cols] = jnp.log(f_gate)
        key_ref[0, :, cols] = (1.0 - f_gate).astype(BF16)
        v = _dot(u, w_ref[:, 2 * d + c * MXU_N:2 * d + (c + 1) * MXU_N])
        v_ref[0, :, cols] = v.astype(BF16)
        g = _dot(u, w_ref[:, 3 * d + c * MXU_N:3 * d + (c + 1) * MXU_N])
        gs_ref[0, :, cols] = _silu(g).astype(BF16)


def _rec_in(x, sc, sh, w, lb_logits, layer_idx, ts):
    b, s, d = x.shape
    depth = lb_logits.shape[0]
    tile = pl.BlockSpec((1, ts, d), lambda i, j: (i, j, 0))
    per_b = pl.BlockSpec((1, 1, d), lambda i, j: (i, 0, 0))
    bf = jax.ShapeDtypeStruct((b, s, d), BF16)
    return pl.pallas_call(
        functools.partial(_rec_in_kernel, layer_idx=layer_idx),
        grid=(b, s // ts),
        in_specs=[tile, per_b, per_b,
                  pl.BlockSpec((d, 4 * d), lambda i, j: (0, 0)),
                  pl.BlockSpec((depth, d), lambda i, j: (0, 0))],
        out_specs=[tile] * 5,
        out_shape=[bf, bf, bf, bf, jax.ShapeDtypeStruct((b, s, d), F32)],
        compiler_params=_params("arbitrary", "arbitrary"),
        name="rec_in",
    )(x, sc, sh, w, lb_logits)


def _hgrn_kernel(qf_ref, key_ref, v_ref, lf_ref, gs_ref, nw_ref, o_ref, st_scr, *, n_heads):
    cs = HGRN_CHUNK
    dk = HGRN_EXPAND

    @pl.when(pl.program_id(1) == 0)
    def _():
        st_scr[...] = jnp.zeros(st_scr.shape, F32)

    row = lax.broadcasted_iota(jnp.int32, (cs, cs), 0)
    col = lax.broadcasted_iota(jnp.int32, (cs, cs), 1)
    causal = row >= col
    tri = jnp.where(causal, 1.0, 0.0).astype(BF16)
    n_chunks = qf_ref.shape[1] // cs

    def chunk(c, carry):
        rows = pl.ds(pl.multiple_of(c * cs, cs), cs)
        lf = lf_ref[0, rows, :]
        hi = lf.astype(BF16)
        r1 = lf - hi.astype(F32)
        mid = r1.astype(BF16)
        lo = (r1 - mid.astype(F32)).astype(BF16)
        g = _dot(tri, hi) + _dot(tri, mid) + _dot(tri, lo)
        g_last = g[cs - 1:cs, :]
        qf = qf_ref[0, rows, :].astype(F32)
        key = key_ref[0, rows, :].astype(F32)
        v = v_ref[0, rows, :]
        q_dec = (qf * jnp.exp(g)).astype(BF16)
        k_dec = (key * jnp.exp(-g)).astype(BF16)
        k_state = (key * jnp.exp(g_last - g)).astype(BF16)
        decay = jnp.exp(g_last)
        heads = [slice(h * dk, (h + 1) * dk) for h in range(n_heads)]
        a = [jnp.where(causal, _dot_nt(q_dec[:, c], k_dec[:, c]), 0.0).astype(BF16) for c in heads]
        st = [st_scr[h] for h in range(n_heads)]
        o = [_dot(a[h], v[:, c]) + _dot_nt(q_dec[:, c], st[h].astype(BF16))
             for h, c in enumerate(heads)]
        for h, c in enumerate(heads):
            v_t = v[:, c].astype(F32).T.astype(BF16)
            st_scr[h] = st[h] * decay[:, c] + _dot(v_t, k_state[:, c])
        on = jnp.concatenate(
            [oh * lax.rsqrt(jnp.mean(oh * oh, axis=-1, keepdims=True) + RMS_EPS) for oh in o], axis=1)
        o_ref[0, rows, :] = (on * nw_ref[...] * gs_ref[0, rows, :].astype(F32)).astype(BF16)
        return carry

    lax.fori_loop(0, n_chunks, chunk, 0, unroll=4)


def _hgrn_recurrence(qf, key, v, lf, gs, norm_w, ts):
    b, s, d = qf.shape
    h = d // HGRN_EXPAND
    tile = pl.BlockSpec((1, ts, d), lambda i, j: (i, j, 0))
    return pl.pallas_call(
        functools.partial(_hgrn_kernel, n_heads=h),
        grid=(b, s // ts),
        in_specs=[tile, tile, tile, tile, tile, pl.BlockSpec((1, d), lambda i, j: (0, 0))],
        out_specs=tile,
        out_shape=jax.ShapeDtypeStruct((b, s, d), BF16),
        scratch_shapes=[pltpu.VMEM((h, HGRN_EXPAND, HGRN_EXPAND), F32)],
        compiler_params=_params("arbitrary", "arbitrary"),
        name="hgrn_recurrence",
    )(qf, key, v, lf, gs, norm_w)


GATHER_UNROLL = 8


ROW_TILE = 8


def _start_row(idx_at, src_hbm, dst, sem, r, priority=0):
    first = pl.multiple_of(idx_at(r), ROW_TILE)
    pltpu.make_async_copy(src_hbm.at[pl.ds(first, ROW_TILE), :],
                          dst.at[pl.ds(r * ROW_TILE, ROW_TILE), :], sem).start(priority=priority)


def _start_row_gather(idx_at, src_hbm, dst, sem, n_rows, both_queues=False):
    ways = 2 if both_queues else 1

    def group(g, carry):
        for k in range(ways):
            _start_row(idx_at, src_hbm, dst, sem, g * ways + k, priority=k)
        return carry

    lax.fori_loop(0, n_rows // ways, group, 0, unroll=GATHER_UNROLL // ways)


def _wait_rows(buf, sem):
    pltpu.make_async_copy(buf, buf, sem).wait()


MOE_AHEAD = 2
MOE_XBUFS = MOE_AHEAD + 1


def _moe_ffn_kernel(te_ref, na_ref, src_t0_ref, src_t1_ref, src_ahead_ref, u_hbm, wg_ref, wu_ref, wd_ref,
                    o_ref, xbuf, xb_scr, h_scr, acc_scr, sems, *, tm, nf, n_tiles):
    i = pl.program_id(0)
    j = pl.program_id(1)
    rows_per_step = tm // nf
    n_active = na_ref[0]
    active = i < n_active
    slot = i % MOE_XBUFS
    ahead = (i + MOE_AHEAD) % MOE_XBUFS

    assert MOE_AHEAD == 2

    @pl.when((j == 0) & (i == 0))
    def _():
        _start_row_gather(lambda r: src_t0_ref[0, 0, r], u_hbm, xbuf.at[0], sems.at[0], tm)
        _start_row_gather(lambda r: src_t1_ref[0, 0, r], u_hbm, xbuf.at[1], sems.at[1], tm)

    @pl.when((j == 0) & (i <= n_active + 1))
    def _():
        _wait_rows(xbuf.at[slot], sems.at[slot])

    @pl.when(active & (j == 0))
    def _():
        xb_scr[...] = _load_row_tiles(xbuf.at[slot], tm, BF16)

    @pl.when(active)
    def _():
        for k in range(rows_per_step):
            _start_row(lambda r: src_ahead_ref[0, 0, r], u_hbm, xbuf.at[ahead], sems.at[ahead],
                       j * rows_per_step + k)
        x = xb_scr[...]
        for c in range(wg_ref.shape[2] // MXU_N):
            cols = slice(c * MXU_N, (c + 1) * MXU_N)
            hc = _silu(_dot(x, wg_ref[0, :, cols])) * _dot(x, wu_ref[0, :, cols])
            h_scr[:, cols] = hc.astype(BF16)
        y = _dot(h_scr[...], wd_ref[0])
        if nf == 1:
            _store_row_tiles(o_ref, y)
        else:
            @pl.when(j == 0)
            def _():
                acc_scr[...] = y

            if nf > 2:
                @pl.when((j > 0) & (j < nf - 1))
                def _():
                    acc_scr[...] += y

            @pl.when(j == nf - 1)
            def _():
                _store_row_tiles(o_ref, acc_scr[...] + y)

    @pl.when(jnp.logical_not(active) & (j == nf - 1) & (i < n_tiles))
    def _():
        o_ref[...] = jnp.zeros(o_ref.shape, F32)


def _moe_ffn(u_tiles, src, wg, wu, wd, tile_expert, n_active, tm, tf):
    d = wd.shape[2]
    sub = d // LANES
    assert sub == ROW_TILE
    n_tiles = src.shape[0] // tm
    assert tile_expert.shape[0] == n_tiles + MOE_AHEAD
    nf = wg.shape[2] // tf

    def frozen(j, i, na):
        return jnp.where(i < na[0], j, nf - 1)

    def idx_block(tile_of):
        return pl.BlockSpec((1, 1, tm), lambda i, j, te, na: (jnp.minimum(tile_of(i), n_tiles - 1), 0, 0),
                            memory_space=pltpu.SMEM)

    grid_spec = pltpu.PrefetchScalarGridSpec(
        num_scalar_prefetch=2,
        grid=(n_tiles + MOE_AHEAD, nf),
        in_specs=[idx_block(lambda i: 0), idx_block(lambda i: 1), idx_block(lambda i: i + MOE_AHEAD),
                  pl.BlockSpec(memory_space=pl.ANY),
                  pl.BlockSpec((1, d, tf), lambda i, j, te, na: (te[i], 0, frozen(j, i, na))),
                  pl.BlockSpec((1, d, tf), lambda i, j, te, na: (te[i], 0, frozen(j, i, na))),
                  pl.BlockSpec((1, tf, d), lambda i, j, te, na: (te[i], frozen(j, i, na), 0))],
        out_specs=pl.BlockSpec((tm * sub, LANES), lambda i, j, te, na: (jnp.minimum(i, n_tiles - 1), 0)),
        scratch_shapes=[pltpu.VMEM((MOE_XBUFS, tm * sub, LANES), F32),
                        pltpu.VMEM((tm, d), BF16),
                        pltpu.VMEM((tm, tf), BF16),
                        pltpu.VMEM((tm, d), F32),
                        pltpu.SemaphoreType.DMA((MOE_XBUFS,))],
    )
    src3 = src.reshape(n_tiles, 1, tm)
    return pl.pallas_call(
        functools.partial(_moe_ffn_kernel, tm=tm, nf=nf, n_tiles=n_tiles),
        grid_spec=grid_spec,
        out_shape=jax.ShapeDtypeStruct((n_tiles * tm * sub, LANES), F32),
        compiler_params=_params("arbitrary", "arbitrary"),
        name="moe_ffn",
    )(tile_expert, n_active, src3, src3, src3, u_tiles, wg, wu, wd)


def _combine_ln_kernel(sa_cur, sb_cur, sa_nxt, sb_nxt, ys_hbm, r_ref, x_ref, g1p_ref, lng_ref, lnb_ref,
                       xo_ref, ybuf, sems, *, alpha, tm):
    i = pl.program_id(0)
    slot = i % 2

    def start(sa, sb, s):
        _start_row_gather(lambda r: sa[r], ys_hbm, ybuf.at[s, 0], sems.at[s], tm, both_queues=True)
        _start_row_gather(lambda r: sb[r], ys_hbm, ybuf.at[s, 1], sems.at[s], tm, both_queues=True)

    @pl.when(i == 0)
    def _():
        start(sa_cur, sb_cur, 0)

    @pl.when(i + 1 < pl.num_programs(0))
    def _():
        start(sa_nxt, sb_nxt, 1 - slot)

    _wait_rows(ybuf.at[slot], sems.at[slot])
    r = r_ref[...]
    y = (r[:, 2:3] * _load_row_tiles(ybuf.at[slot, 0], tm, F32)
         + r[:, 3:4] * _load_row_tiles(ybuf.at[slot, 1], tm, F32))
    xo_ref[...] = _layer_norm(alpha * x_ref[...] + g1p_ref[0] * y, lng_ref[...], lnb_ref[...])


def _combine_ln(ys_tiles, slot_a, slot_b, route, x, g1p, lng, lnb, alpha, tm):
    b, s, d = x.shape
    n = b * s
    sub = d // LANES
    n_tiles = n // tm
    per_seq = s // tm
    cur = pl.BlockSpec((tm,), lambda i: (i,), memory_space=pltpu.SMEM)
    nxt = pl.BlockSpec((tm,), lambda i: (jnp.minimum(i + 1, n_tiles - 1),), memory_space=pltpu.SMEM)
    tile = pl.BlockSpec((tm, d), lambda i: (i, 0))
    vec = pl.BlockSpec((1, d), lambda i: (0, 0))
    out = pl.pallas_call(
        functools.partial(_combine_ln_kernel, alpha=alpha, tm=tm),
        grid=(n_tiles,),
        in_specs=[cur, cur, nxt, nxt, pl.BlockSpec(memory_space=pl.ANY),
                  pl.BlockSpec((tm, LANES), lambda i: (i, 0)), tile,
                  pl.BlockSpec((1, 1, d), lambda i: (i // per_seq, 0, 0)), vec, vec],
        out_specs=tile,
        out_shape=jax.ShapeDtypeStruct((n, d), F32),
        scratch_shapes=[pltpu.VMEM((2, 2, tm * sub, LANES), F32), pltpu.SemaphoreType.DMA((2,))],
        compiler_params=_params("arbitrary"),
        name="combine_ln",
    )(slot_a, slot_b, slot_a, slot_b, ys_tiles, route.reshape(n, LANES), x.reshape(n, d), g1p, lng, lnb)
    return out.reshape(b, s, d)


def _moe_plan(route, tm):
    n = route.shape[0]
    e = N_EXPERTS
    n_tiles = (TOP_K * n + e * (tm - 1)) // tm
    flat_e = route[:, :TOP_K].astype(jnp.int32).reshape(-1)
    onehot = (flat_e[:, None] == jnp.arange(e, dtype=jnp.int32)[None, :]).astype(jnp.int32)
    incl = jnp.cumsum(onehot, axis=0)
    counts = incl[-1]
    padded = (counts + tm - 1) // tm * tm
    ends = jnp.cumsum(padded)
    offs = ends - padded
    slot = jnp.sum(onehot * (offs[None, :] + incl - 1), axis=1)
    n_active = ends[-1] // tm
    tile_id = jnp.minimum(jnp.arange(n_tiles + MOE_AHEAD, dtype=jnp.int32), n_active - 1)
    tile_expert = jnp.minimum(jnp.sum((tile_id[:, None] * tm >= ends[None, :]).astype(jnp.int32), axis=1), e - 1)
    by_expert = jnp.argsort(flat_e, stable=True).astype(jnp.int32) // TOP_K
    by_expert = jnp.pad(by_expert, (0, n_tiles * tm - TOP_K * n))
    row_expert = jnp.repeat(tile_expert[:n_tiles], tm)
    shift = offs - (jnp.cumsum(counts) - counts)
    src = jnp.zeros((n_tiles * tm,), jnp.int32)
    for k in range(e):
        src = jnp.where(row_expert == k, jnp.roll(by_expert, shift[k]), src)
    return ((slot * ROW_TILE).reshape(n, TOP_K), src * ROW_TILE, tile_expert.astype(jnp.int32),
            n_active.reshape(1).astype(jnp.int32))


def _moe_swiglu_ln(u_tiles, route, wg, wu, wd, x, g1p, lng, lnb, alpha, tiles):
    assert TOP_K == 2
    b, s, d = x.shape
    n = b * s
    nf = tiles.moe_slabs
    tf = wg.shape[2] // nf
    assert tiles.moe_rows % nf == 0
    slot, src, tile_expert, n_active = _moe_plan(route.reshape(n, LANES), tiles.moe_rows)
    ys = _moe_ffn(u_tiles.reshape(n * (d // LANES), LANES), src, wg, wu, wd, tile_expert, n_active,
                  tiles.moe_rows, tf)
    return _combine_ln(ys, slot[:, 0], slot[:, 1], route, x, g1p, lng, lnb, alpha, tiles.combine_rows)


class _Tiles(NamedTuple):
    token_rows: int
    combine_rows: int
    attn_heads: int
    attn_chunk: int
    moe_rows: int
    moe_slabs: int


def _tiles(s, d, d_ff_expert):
    return _Tiles(token_rows=1024,
                  combine_rows=512,
                  attn_heads=math.gcd(d // HEAD_DIM, 8),
                  attn_chunk=math.gcd(s // MOBA_BLOCK, 4),
                  moe_rows=896,
                  moe_slabs=2 if d_ff_expert % (2 * MXU_N) == 0 else 1)


def kernel(x, c, positions, ada_w, ada_b, ln_g, ln_b, attn_w_in, attn_w_out, rec_w_in, rec_lb_logits,
           rec_norm_w, rec_w_out, ffn_w_gate, ffn_w_up, ffn_w_down, router_w, moe_w_gate, moe_w_up,
           moe_w_down):
    b, s, d = x.shape
    depth = ada_w.shape[0]
    alpha = (2.0 * depth) ** 0.25
    tiles = _tiles(s, d, moe_w_gate.shape[-1])
    ts = tiles.token_rows

    mods = _ada_mods(c, ada_w, ada_b).reshape(depth, b, 6, 1, d)
    one_plus = lambda t: 1.0 + t
    pos = positions.reshape(b, s, 1)

    for i in range(depth):
        shift_m, scale_m, gate_m, shift_f, scale_f, gate_f = (mods[i, :, r] for r in range(6))
        j = i // 2
        lng = ln_g[i].reshape(2, 1, d)
        lnb = ln_b[i].reshape(2, 1, d)
        if i % 2 == 0:
            qkv = _qkv_proj(x, one_plus(scale_m), shift_m, pos, attn_w_in[j].astype(BF16), ts)
            o = _moba_attention(qkv, d, n_grp=tiles.attn_heads, ch=tiles.attn_chunk)
            x = _proj_ffn_dense(o, attn_w_out[j].astype(BF16), x, one_plus(gate_m), lng, lnb,
                                one_plus(scale_f), shift_f, ffn_w_gate[j].astype(BF16),
                                ffn_w_up[j].astype(BF16), ffn_w_down[j].astype(BF16), one_plus(gate_f),
                                alpha, ts)
        else:
            qf, key, v, gs, lf = _rec_in(x, one_plus(scale_m), shift_m, rec_w_in[j].astype(BF16),
                                         rec_lb_logits, i, ts)
            o = _hgrn_recurrence(qf, key, v, lf, gs, rec_norm_w[j].reshape(1, d), ts)
            w_hi = router_w[j].astype(BF16)
            w_lo = (router_w[j] - w_hi.astype(F32)).astype(BF16)
            wr = (jnp.zeros((d, 2 * LANES), BF16).at[:, :N_EXPERTS].set(w_hi)
                  .at[:, LANES:LANES + N_EXPERTS].set(w_lo))
            x, u, route = _proj_ln_route(o, rec_w_out[j].astype(BF16), x, one_plus(gate_m), lng[0], lnb[0],
                                         one_plus(scale_f), shift_f, wr, alpha, ts)
            x = _moe_swiglu_ln(u, route, moe_w_gate[j].astype(BF16), moe_w_up[j].astype(BF16),
                               moe_w_down[j].astype(BF16), x, one_plus(gate_f), lng[1], lnb[1], alpha, tiles)
    return x
```
